```python
import jax, jax.numpy as jnp
from jax import lax
import numpy as np

D_MODEL = 4096
BATCH = 2
SEQ = 8192
DEPTH = 1
DEC_BATCH = 8
DEC_SEQ = 32
PAST_LEN = 1024

CHUNK = 64
HG_HEADS = 16
HG_DK = 128
HG_DV = 128
HG_WIDTH = HG_HEADS * HG_DK
CONV_WIDTH = 2048
CONV_K = 3
N_MEM = 256
XA_HEADS = 4
XA_HEAD_DIM = D_MODEL // XA_HEADS
FFN_HIDDEN = ((8 * D_MODEL + 3 * 256 - 1) // (3 * 256)) * 256
PROJ_WIDTH = 4 * HG_WIDTH + 3 * CONV_WIDTH + 2 * D_MODEL
EPS = 1e-6

kernel_name = 'hgrn2_shortconv_gated_streaming_encoder_step'


def rmsnorm(x, g):
    xf = x.astype(jnp.float32)
    y = xf * lax.rsqrt(jnp.mean(xf * xf, axis=-1, keepdims=True) + EPS)
    return (y * g.astype(jnp.float32)).astype(x.dtype)


def hgrn2_recurrence(q, k, v, logf, s0):
    bsz, L, H, _ = q.shape
    C = min(CHUNK, L)
    n = L // C

    def to_chunks(t):
        return t.reshape(bsz, n, C, H, t.shape[-1]).transpose(1, 0, 3, 2, 4)

    causal = jnp.tril(jnp.ones((C, C), dtype=bool))[:, :, None]

    def step(S, inp):
        qc, kc, vc, gc = inp
        b = jnp.cumsum(gc, axis=2)
        o_inter = jnp.einsum('bhtk,bhkv->bhtv', qc * jnp.exp(b), S)
        diff = b[:, :, :, None, :] - b[:, :, None, :, :]
        decay = jnp.exp(jnp.where(causal, diff, -jnp.inf))
        scores = jnp.einsum('bhtk,bhtsk,bhsk->bhts', qc, decay, kc)
        o_intra = jnp.einsum('bhts,bhsv->bhtv', scores, vc)
        b_last = b[:, :, -1:, :]
        S_new = jnp.exp(b_last[:, :, 0, :])[..., None] * S + jnp.einsum('bhsk,bhsv->bhkv', kc * jnp.exp(b_last - b), vc)
        return S_new, o_inter + o_intra

    S_fin, o = lax.scan(step, s0, (to_chunks(q), to_chunks(k), to_chunks(v), to_chunks(logf)))
    o = o.transpose(1, 0, 3, 2, 4).reshape(bsz, L, H, v.shape[-1])
    return o, S_fin


def gated_mixer(h, w_in, lb, hg_norm, conv_w, w_a, w_b, w_o, S0, conv_buf):
    bsz, L, _ = h.shape
    f32 = jnp.float32
    sizes = [HG_WIDTH] * 4 + [CONV_WIDTH] * 3 + [D_MODEL] * 2
    idx = np.cumsum(sizes)[:-1].tolist()
    q, fpre, iv, og, ch, cb, cc, ga, gb = jnp.split(h @ w_in, idx, axis=-1)
    f = lb + (1.0 - lb) * jax.nn.sigmoid(fpre.astype(f32))
    logf = jnp.log(f)
    k = 1.0 - f
    heads = lambda t: t.reshape(bsz, L, HG_HEADS, t.shape[-1] // HG_HEADS)
    o, S_fin = hgrn2_recurrence(heads(q.astype(f32)), heads(k), heads(iv.astype(f32)), heads(logf), S0.astype(f32))
    o = rmsnorm(o, hg_norm.reshape(HG_HEADS, HG_DV)).reshape(bsz, L, HG_WIDTH).astype(h.dtype)
    a = (o * jax.nn.silu(og)) @ w_a
    u = cc * ch
    up = jnp.concatenate([conv_buf.astype(u.dtype), u], axis=1)
    z = sum(conv_w[j] * up[:, j:j + L] for j in range(CONV_K))
    bb = (cb * z) @ w_b
    merged = jax.nn.sigmoid(ga) * a + jax.nn.sigmoid(gb) * bb
    new_buf = up[:, up.shape[1] - (CONV_K - 1):]
    return merged @ w_o, S_fin, new_buf


def memory_kv(mem, g_mem, w_xk, w_xv):
    bsz = mem.shape[0]
    m = rmsnorm(mem, g_mem)
    mk = (m @ w_xk).reshape(bsz, N_MEM, XA_HEADS, XA_HEAD_DIM)
    mv = (m @ w_xv).reshape(bsz, N_MEM, XA_HEADS, XA_HEAD_DIM)
    return mk, mv


def cross_attention(h, mk, mv, w_xq, w_xo):
    bsz, L, _ = h.shape
    q = (h @ w_xq).reshape(bsz, L, XA_HEADS, XA_HEAD_DIM)
    s = jnp.einsum('blhd,bmhd->bhlm', q, mk.astype(q.dtype)).astype(jnp.float32) * (XA_HEAD_DIM ** -0.5)
    p = jax.nn.softmax(s, axis=-1).astype(h.dtype)
    o = jnp.einsum('bhlm,bmhd->blhd', p, mv.astype(h.dtype)).reshape(bsz, L, D_MODEL)
    return o @ w_xo


def swiglu(h, w_gate, w_up, w_down):
    return (jax.nn.silu(h @ w_gate) * (h @ w_up)) @ w_down


def layer(x, g_mix, w_in, lb, hg_norm, conv_w, w_a, w_b, w_o, g_xa, w_xq, w_xo, g_ffn, w_gate, w_up, w_down, S0, conv_buf, mk, mv):
    m, S_fin, new_buf = gated_mixer(rmsnorm(x, g_mix), w_in, lb, hg_norm, conv_w, w_a, w_b, w_o, S0, conv_buf)
    x = x + m
    x = x + cross_attention(rmsnorm(x, g_xa), mk, mv, w_xq, w_xo)
    x = x + swiglu(rmsnorm(x, g_ffn), w_gate, w_up, w_down)
    return x, S_fin, new_buf


def setup_inputs(seed: int = 0) -> dict:
    key = jax.random.key(seed)
    ks = jax.random.split(key, 32)
    f32 = jnp.float32
    nrm = lambda k, shape, scale: jax.random.normal(k, shape, f32) * scale
    gain = lambda k, shape: 1.0 + 0.02 * jax.random.normal(k, shape, f32)
    return {
        'x_prompt': nrm(ks[0], (BATCH, SEQ, D_MODEL), 1.0),
        'x_sample': nrm(ks[1], (DEC_BATCH, DEC_SEQ, D_MODEL), 1.0),
        'cache_mem_k': nrm(ks[2], (DEPTH, DEC_BATCH, N_MEM, XA_HEADS, XA_HEAD_DIM), 1.0),
        'cache_mem_v': nrm(ks[3], (DEPTH, DEC_BATCH, N_MEM, XA_HEADS, XA_HEAD_DIM), 1.0),
        'state_hgrn': nrm(ks[4], (DEPTH, DEC_BATCH, HG_HEADS, HG_DK, HG_DV), 1.0),
        'state_conv': nrm(ks[5], (DEPTH, DEC_BATCH, CONV_K - 1, CONV_WIDTH), 0.5),
        'mem_prompt': nrm(ks[6], (BATCH, N_MEM, D_MODEL), 1.0),
        'norm_mix': gain(ks[7], (DEPTH, D_MODEL)),
        'w_in': nrm(ks[8], (DEPTH, D_MODEL, PROJ_WIDTH), D_MODEL ** -0.5),
        'lb_logits': nrm(ks[9], (DEPTH + 1, HG_WIDTH), 0.5),
        'hg_norm': gain(ks[10], (DEPTH, HG_WIDTH)),
        'conv_w': nrm(ks[11], (DEPTH, CONV_K, CONV_WIDTH), CONV_K ** -0.5),
        'w_a': nrm(ks[12], (DEPTH, HG_WIDTH, D_MODEL), HG_WIDTH ** -0.5),
        'w_b': nrm(ks[13], (DEPTH, CONV_WIDTH, D_MODEL), CONV_WIDTH ** -0.5),
        'w_o': nrm(ks[14], (DEPTH, D_MODEL, D_MODEL), D_MODEL ** -0.5),
        'norm_xattn': gain(ks[15], (DEPTH, D_MODEL)),
        'norm_mem': gain(ks[16], (DEPTH, D_MODEL)),
        'w_xq': nrm(ks[17], (DEPTH, D_MODEL, D_MODEL), D_MODEL ** -0.5),
        'w_xk': nrm(ks[18], (DEPTH, D_MODEL, D_MODEL), D_MODEL ** -0.5),
        'w_xv': nrm(ks[19], (DEPTH, D_MODEL, D_MODEL), D_MODEL ** -0.5),
        'w_xo': nrm(ks[20], (DEPTH, D_MODEL, D_MODEL), D_MODEL ** -0.5),
        'norm_ffn': gain(ks[21], (DEPTH, D_MODEL)),
        'w_gate': nrm(ks[22], (DEPTH, D_MODEL, FFN_HIDDEN), D_MODEL ** -0.5),
        'w_up': nrm(ks[23], (DEPTH, D_MODEL, FFN_HIDDEN), D_MODEL ** -0.5),
        'w_down': nrm(ks[24], (DEPTH, FFN_HIDDEN, D_MODEL), FFN_HIDDEN ** -0.5),
        'norm_final': gain(ks[25], (D_MODEL,)),
    }


def reference(x_prompt, x_sample, cache_mem_k, cache_mem_v, state_hgrn, state_conv, mem_prompt,
              norm_mix, w_in, lb_logits, hg_norm, conv_w, w_a, w_b, w_o,
              norm_xattn, norm_mem, w_xq, w_xk, w_xv, w_xo,
              norm_ffn, w_gate, w_up, w_down, norm_final):
    f32 = jnp.float32
    lb_all = jnp.cumsum(jax.nn.softmax(lb_logits.astype(f32), axis=0), axis=0)
    bp = x_prompt.shape[0]
    xp, xs = x_prompt, x_sample
    mk_list, mv_list, sp_list, cp_list, ss_list, cs_list = [], [], [], [], [], []
    for l in range(DEPTH):
        shared = (norm_mix[l], w_in[l], lb_all[l], hg_norm[l], conv_w[l], w_a[l], w_b[l], w_o[l],
                  norm_xattn[l], w_xq[l], w_xo[l], norm_ffn[l], w_gate[l], w_up[l], w_down[l])
        mk_p, mv_p = memory_kv(mem_prompt, norm_mem[l], w_xk[l], w_xv[l])
        S0 = jnp.zeros((bp, HG_HEADS, HG_DK, HG_DV), f32)
        buf0 = jnp.zeros((bp, CONV_K - 1, CONV_WIDTH), xp.dtype)
        xp, S_p, buf_p = layer(xp, *shared, S0, buf0, mk_p, mv_p)
        xs, S_s, buf_s = layer(xs, *shared, state_hgrn[l], state_conv[l], cache_mem_k[l], cache_mem_v[l])
        mk_list.append(mk_p)
        mv_list.append(mv_p)
        sp_list.append(S_p)
        cp_list.append(buf_p)
        ss_list.append(S_s)
        cs_list.append(buf_s)
    y_prompt = rmsnorm(xp, norm_final)
    y_sample = rmsnorm(xs, norm_final)
    return (y_prompt, y_sample, jnp.stack(mk_list), jnp.stack(mv_list), jnp.stack(sp_list), jnp.stack(cp_list), jnp.stack(ss_list), jnp.stack(cs_list))
```

```python
import functools

import numpy as np
import jax
import jax.numpy as jnp
from jax import lax
from jax.experimental import pallas as pl
from jax.experimental.pallas import tpu as pltpu

EPS = 1e-6
LANES = 128
HEAD_DIM_HGRN = 128
CHUNK = 64
VMEM_LIMIT_BYTES = 56 * 1024 * 1024
BF16 = jnp.bfloat16
F32 = jnp.float32


def _params(*semantics):
    return pltpu.CompilerParams(dimension_semantics=semantics, vmem_limit_bytes=VMEM_LIMIT_BYTES)


def _tile(n, pref, mult):
    if n <= pref:
        return n
    t = (pref // mult) * mult
    while t >= mult:
        if n % t == 0:
            return t
        t -= mult
    raise ValueError(f"no tile for {n} (pref {pref}, mult {mult})")


def _rmsnorm_kernel(x_ref, g_ref, o_ref):
    x = x_ref[...].astype(F32)
    y = x * lax.rsqrt(jnp.mean(x * x, axis=-1, keepdims=True) + EPS)
    o_ref[...] = (y * g_ref[...]).astype(o_ref.dtype)


def _rmsnorm(x, g, out_dtype):
    m, d = x.shape
    bm = _tile(m, 512, 8)
    return pl.pallas_call(
        _rmsnorm_kernel,
        grid=(m // bm,),
        in_specs=[pl.BlockSpec((bm, d), lambda i: (i, 0)),
                  pl.BlockSpec((1, d), lambda i: (0, 0))],
        out_specs=pl.BlockSpec((bm, d), lambda i: (i, 0)),
        out_shape=jax.ShapeDtypeStruct((m, d), out_dtype),
        compiler_params=_params("parallel"),
        name="rmsnorm",
    )(x, g.reshape(1, d).astype(F32))


def _mm_kernel(x_ref, w_ref, o_ref):
    o_ref[...] = jnp.dot(x_ref[...], w_ref[...], preferred_element_type=F32).astype(o_ref.dtype)


def _mm_res_kernel(x_ref, w_ref, r_ref, o_ref):
    o_ref[...] = r_ref[...] + jnp.dot(x_ref[...], w_ref[...], preferred_element_type=F32)


def _mm_gated_kernel(a_ref, b_ref, wa_ref, wb_ref, ga_ref, gb_ref, o_ref):
    a = jnp.dot(a_ref[...], wa_ref[...], preferred_element_type=F32)
    b = jnp.dot(b_ref[...], wb_ref[...], preferred_element_type=F32)
    ga = jax.nn.sigmoid(ga_ref[...].astype(F32))
    gb = jax.nn.sigmoid(gb_ref[...].astype(F32))
    o_ref[...] = (ga * a + gb * b).astype(o_ref.dtype)


def _mm_swiglu_kernel(x_ref, wg_ref, wu_ref, o_ref):
    x = x_ref[...]
    g = jnp.dot(x, wg_ref[...], preferred_element_type=F32)
    u = jnp.dot(x, wu_ref[...], preferred_element_type=F32)
    o_ref[...] = (g * jax.nn.sigmoid(g) * u).astype(o_ref.dtype)


def _matmul(x, w, out_dtype, *, bm_pref=1024, bn_pref=512, res=None):
    m, k = x.shape
    n = w.shape[1]
    bm = _tile(m, bm_pref, 8)
    bn = _tile(n, bn_pref, LANES)
    in_specs = [pl.BlockSpec((bm, k), lambda i, j: (i, 0)),
                pl.BlockSpec((k, bn), lambda i, j: (0, j))]
    args = [x, w]
    body = _mm_kernel
    if res is not None:
        in_specs.append(pl.BlockSpec((bm, bn), lambda i, j: (i, j)))
        args.append(res)
        body = _mm_res_kernel
    return pl.pallas_call(
        body,
        grid=(m // bm, n // bn),
        in_specs=in_specs,
        out_specs=pl.BlockSpec((bm, bn), lambda i, j: (i, j)),
        out_shape=jax.ShapeDtypeStruct((m, n), out_dtype),
        compiler_params=_params("parallel", "parallel"),
        name="matmul_res" if res is not None else "matmul",
    )(*args)


def _matmul_gated(a, b, wa, wb, proj, ga_col, gb_col):
    m, ka = a.shape
    kb = b.shape[1]
    n = wa.shape[1]
    bm = _tile(m, 1024, 8)
    bn = _tile(n, 512, LANES)
    assert ga_col % bn == 0 and gb_col % bn == 0
    ga_blk, gb_blk = ga_col // bn, gb_col // bn
    return pl.pallas_call(
        _mm_gated_kernel,
        grid=(m // bm, n // bn),
        in_specs=[pl.BlockSpec((bm, ka), lambda i, j: (i, 0)),
                  pl.BlockSpec((bm, kb), lambda i, j: (i, 0)),
                  pl.BlockSpec((ka, bn), lambda i, j: (0, j)),
                  pl.BlockSpec((kb, bn), lambda i, j: (0, j)),
                  pl.BlockSpec((bm, bn), lambda i, j: (i, ga_blk + j)),
                  pl.BlockSpec((bm, bn), lambda i, j: (i, gb_blk + j))],
        out_specs=pl.BlockSpec((bm, bn), lambda i, j: (i, j)),
        out_shape=jax.ShapeDtypeStruct((m, n), BF16),
        compiler_params=_params("parallel", "parallel"),
        name="matmul_gated",
    )(a, b, wa, wb, proj, proj)


def _matmul_swiglu(x, wg, wu):
    m, k = x.shape
    n = wg.shape[1]
    bm = _tile(m, 1024, 8)
    bn = _tile(n, 256, LANES)
    return pl.pallas_call(
        _mm_swiglu_kernel,
        grid=(m // bm, n // bn),
        in_specs=[pl.BlockSpec((bm, k), lambda i, j: (i, 0)),
                  pl.BlockSpec((k, bn), lambda i, j: (0, j)),
                  pl.BlockSpec((k, bn), lambda i, j: (0, j))],
        out_specs=pl.BlockSpec((bm, bn), lambda i, j: (i, j)),
        out_shape=jax.ShapeDtypeStruct((m, n), BF16),
        compiler_params=_params("parallel", "parallel"),
        name="matmul_swiglu",
    )(x, wg, wu)


def _hgrn_tables(c):
    levels = []
    l = 1
    while l <= c:
        levels.append(l)
        l *= 2
    t = np.arange(c)[:, None]
    j = np.arange(c)[None, :]
    rows, masks = [], []
    for l in levels:
        same = (t // l) == (j // l)
        rows.append(same & (j <= t))
    for l in levels[1:]:
        same = (t // l) == (j // l)
        rows.append(same & (j > t))
    for l in levels[:-1]:
        masks.append(((t // l) % 2 == 1) & ((j // l) == (t // l) - 1))
    masks.append(t == j)
    return (levels, np.concatenate(rows, 0).astype(np.float32),
            np.stack(masks, 0).astype(np.float32))


def _hgrn_kernel(mc_ref, mask_ref, lbl_ref, q_ref, f_ref, v_ref, og_ref, gn_ref, s0_ref,
                 o_ref, sfin_ref, st_ref, e_ref, *, c, n_chunks, n_levels, layer):
    t_idx = pl.program_id(2)
    dk = HEAD_DIM_HGRN
    nt = (((1,), (1,)), ((), ()))
    tn = (((0,), (0,)), ((), ()))

    @pl.when(t_idx == 0)
    def _():
        st_ref[...] = s0_ref[0, 0].T

    logits = lbl_ref[...].astype(F32)
    ex = jnp.exp(logits - jnp.max(logits, axis=0, keepdims=True))
    lb = jnp.sum(ex[:layer + 1], axis=0, keepdims=True) / jnp.sum(ex, axis=0, keepdims=True)
    gain = gn_ref[...].astype(F32)

    def level_exp(idx):
        rows = pl.ds(idx * c, c)
        e = e_ref[rows, 0:dk] + e_ref[rows, dk:2 * dk] + e_ref[rows, 2 * dk:3 * dk]
        return jnp.exp(e)

    def chunk(ci, carry):
        r0 = pl.multiple_of(ci * c, c)
        rows = pl.ds(r0, c)
        f = lb + (1.0 - lb) * jax.nn.sigmoid(f_ref[0, rows, :])
        g = jnp.log(f)
        k = 1.0 - f
        q = q_ref[0, rows, :].astype(F32)
        v = v_ref[0, rows, :]

        g_hi = g.astype(BF16)
        r1 = g - g_hi.astype(F32)
        g_mid = r1.astype(BF16)
        g_lo = (r1 - g_mid.astype(F32)).astype(BF16)
        g3 = jnp.concatenate([g_hi, g_mid, g_lo], axis=1)
        e_ref[...] = jnp.dot(mc_ref[...], g3, preferred_element_type=F32)

        a = mask_ref[n_levels - 1] * lax.dot_general(
            q.astype(BF16), k.astype(BF16), nt, preferred_element_type=F32)
        for li in range(n_levels - 1):
            qd = (q * level_exp(li)).astype(BF16)
            kd = k.astype(BF16) if li == 0 else (k * level_exp(n_levels + li - 1)).astype(BF16)
            a = a + mask_ref[li] * lax.dot_general(qd, kd, nt, preferred_element_type=F32)

        dec_q = level_exp(n_levels - 1)
        q_in = (q * dec_q).astype(BF16)
        st = st_ref[...]
        o = lax.dot_general(q_in, st.astype(BF16), nt, preferred_element_type=F32)
        o = o + jnp.dot(a.astype(BF16), v, preferred_element_type=F32)

        if c > 1:
            k_out = (k * level_exp(2 * n_levels - 2)).astype(BF16)
        else:
            k_out = k.astype(BF16)
        st_ref[...] = st * dec_q[c - 1:c, :] + lax.dot_general(v, k_out, tn, preferred_element_type=F32)

        y = o * lax.rsqrt(jnp.mean(o * o, axis=-1, keepdims=True) + EPS) * gain
        og = og_ref[0, rows, :].astype(F32)
        o_ref[0, rows, :] = (y * (og * jax.nn.sigmoid(og))).astype(o_ref.dtype)
        return carry

    lax.fori_loop(0, n_chunks, chunk, 0)

    @pl.when(t_idx == pl.num_programs(2) - 1)
    def _():
        sfin_ref[0, 0] = st_ref[...].T


def _hgrn(proj3, fpre3, lb_logits, gain, s0, layer, q_col, v_col, og_col):
    bsz, t, _ = proj3.shape
    hw = fpre3.shape[-1]
    d = HEAD_DIM_HGRN
    nh = hw // d
    c = CHUNK if t % CHUNK == 0 else t
    assert c & (c - 1) == 0 and t % c == 0
    tt = _tile(t, 1024, c)
    levels, mc, masks = _hgrn_tables(c)
    n_levels = len(levels)
    rows = mc.shape[0]
    qb, vb, ob = q_col // d, v_col // d, og_col // d
    kern = functools.partial(_hgrn_kernel, c=c, n_chunks=tt // c, n_levels=n_levels, layer=layer)
    nl = lb_logits.shape[0]
    return pl.pallas_call(
        kern,
        grid=(bsz, nh, t // tt),
        in_specs=[pl.BlockSpec((rows, c), lambda b, h, i: (0, 0)),
                  pl.BlockSpec((n_levels, c, c), lambda b, h, i: (0, 0, 0)),
                  pl.BlockSpec((nl, d), lambda b, h, i: (0, h)),
                  pl.BlockSpec((1, tt, d), lambda b, h, i: (b, i, qb + h)),
                  pl.BlockSpec((1, tt, d), lambda b, h, i: (b, i, h)),
                  pl.BlockSpec((1, tt, d), lambda b, h, i: (b, i, vb + h)),
                  pl.BlockSpec((1, tt, d), lambda b, h, i: (b, i, ob + h)),
                  pl.BlockSpec((1, d), lambda b, h, i: (0, h)),
                  pl.BlockSpec((1, 1, d, d), lambda b, h, i: (b, h, 0, 0))],
        out_specs=[pl.BlockSpec((1, tt, d), lambda b, h, i: (b, i, h)),
                   pl.BlockSpec((1, 1, d, d), lambda b, h, i: (b, h, 0, 0))],
        out_shape=[jax.ShapeDtypeStruct((bsz, t, hw), BF16),
                   jax.ShapeDtypeStruct((bsz, nh, d, d), F32)],
        scratch_shapes=[pltpu.VMEM((d, d), F32), pltpu.VMEM((rows, 3 * d), F32)],
        compiler_params=_params("parallel", "parallel", "arbitrary"),
        name="hgrn2",
    )(jnp.asarray(mc, BF16), jnp.asarray(masks, F32), lb_logits.astype(F32),
      proj3, fpre3, proj3, proj3, gain.reshape(1, hw).astype(F32), s0.astype(F32))


CONV_PAD = 8


def _conv_kernel(cc_ref, ch_ref, cb_ref, w_ref, buf_ref, o_ref, nbuf_ref, u_ref, *, t, k, rb):
    u_ref[CONV_PAD - (k - 1):CONV_PAD, :] = buf_ref[0].astype(F32)
    for r in range(0, t, rb):
        u_ref[CONV_PAD + r:CONV_PAD + r + rb, :] = (
            cc_ref[0, r:r + rb, :].astype(F32) * ch_ref[0, r:r + rb, :].astype(F32))
    w = w_ref[...].astype(F32)
    for r in range(0, t, rb):
        z = w[0:1, :] * u_ref[CONV_PAD - (k - 1) + r:CONV_PAD - (k - 1) + r + rb, :]
        for j in range(1, k):
            s = CONV_PAD - (k - 1) + j + r
            z = z + w[j:j + 1, :] * u_ref[s:s + rb, :]
        o_ref[0, r:r + rb, :] = (cb_ref[0, r:r + rb, :].astype(F32) * z).astype(o_ref.dtype)
    nbuf_ref[0] = u_ref[CONV_PAD + t - (k - 1):CONV_PAD + t, :]


def _conv(proj3, conv_w, buf, ch_col, cb_col, cc_col):
    bsz, t, _ = proj3.shape
    k, cw = conv_w.shape
    assert k - 1 <= CONV_PAD
    d = LANES
    rb = _tile(t, 512, 8)
    kern = functools.partial(_conv_kernel, t=t, k=k, rb=rb)
    return pl.pallas_call(
        kern,
        grid=(bsz, cw // d),
        in_specs=[pl.BlockSpec((1, t, d), lambda b, j: (b, 0, cc_col // d + j)),
                  pl.BlockSpec((1, t, d), lambda b, j: (b, 0, ch_col // d + j)),
                  pl.BlockSpec((1, t, d), lambda b, j: (b, 0, cb_col // d + j)),
                  pl.BlockSpec((k, d), lambda b, j: (0, j)),
                  pl.BlockSpec((1, k - 1, d), lambda b, j: (b, 0, j))],
        out_specs=[pl.BlockSpec((1, t, d), lambda b, j: (b, 0, j)),
                   pl.BlockSpec((1, k - 1, d), lambda b, j: (b, 0, j))],
        out_shape=[jax.ShapeDtypeStruct((bsz, t, cw), BF16),
                   jax.ShapeDtypeStruct((bsz, k - 1, cw), F32)],
        scratch_shapes=[pltpu.VMEM((CONV_PAD + t, d), F32)],
        compiler_params=_params("parallel", "parallel"),
        name="short_conv",
    )(proj3, proj3, proj3, conv_w.astype(F32), buf.astype(F32))


def _attn_kernel(q_ref, k_ref, v_ref, o_ref, *, scale):
    nt = (((1,), (1,)), ((), ()))
    s = lax.dot_general(q_ref[0], k_ref[0], nt, preferred_element_type=F32) * scale
    s = s - jnp.max(s, axis=-1, keepdims=True)
    p = jnp.exp(s)
    p = p / jnp.sum(p, axis=-1, keepdims=True)
    o_ref[0] = jnp.dot(p.astype(BF16), v_ref[0], preferred_element_type=F32).astype(o_ref.dtype)


def _attention(q3, mk3, mv3, n_heads):
    bsz, t, dm = q3.shape
    n_mem = mk3.shape[1]
    hd = dm // n_heads
    bt = _tile(t, 512, 8)
    kern = functools.partial(_attn_kernel, scale=float(hd) ** -0.5)
    return pl.pallas_call(
        kern,
        grid=(bsz, t // bt, n_heads),
        in_specs=[pl.BlockSpec((1, bt, hd), lambda b, i, h: (b, i, h)),
                  pl.BlockSpec((1, n_mem, hd), lambda b, i, h: (b, 0, h)),
                  pl.BlockSpec((1, n_mem, hd), lambda b, i, h: (b, 0, h))],
        out_specs=pl.BlockSpec((1, bt, hd), lambda b, i, h: (b, i, h)),
        out_shape=jax.ShapeDtypeStruct((bsz, t, dm), BF16),
        compiler_params=_params("parallel", "parallel", "parallel"),
        name="cross_attention",
    )(q3, mk3, mv3)


def _layer(x, bsz, t, wts, lb_logits, layer, s0, buf, mk3, mv3, n_xa_heads):
    m, dm = x.shape
    hw = wts["hg_norm"].shape[0]
    cw = wts["conv_w"].shape[1]
    q_col, v_col, og_col = 0, hw, 2 * hw
    ch_col, cb_col, cc_col = 3 * hw, 3 * hw + cw, 3 * hw + 2 * cw
    ga_col, gb_col = 3 * hw + 3 * cw, 3 * hw + 3 * cw + dm

    h = _rmsnorm(x, wts["norm_mix"], BF16)
    proj = _matmul(h, wts["w_in_rest"], BF16)
    fpre = _matmul(h, wts["w_in_f"], F32)
    proj3 = proj.reshape(bsz, t, proj.shape[1])
    a_in, s_fin = _hgrn(proj3, fpre.reshape(bsz, t, hw), lb_logits, wts["hg_norm"], s0, layer,
                        q_col, v_col, og_col)
    bz, new_buf = _conv(proj3, wts["conv_w"], buf, ch_col, cb_col, cc_col)
    merged = _matmul_gated(a_in.reshape(m, hw), bz.reshape(m, cw), wts["w_a"], wts["w_b"],
                           proj, ga_col, gb_col)
    x = _matmul(merged, wts["w_o"], F32, res=x)

    hq = _rmsnorm(x, wts["norm_xattn"], BF16)
    qx = _matmul(hq, wts["w_xq"], BF16)
    att = _attention(qx.reshape(bsz, t, dm), mk3, mv3, n_xa_heads)
    x = _matmul(att.reshape(m, dm), wts["w_xo"], F32, res=x)

    hf = _rmsnorm(x, wts["norm_ffn"], BF16)
    act = _matmul_swiglu(hf, wts["w_gate"], wts["w_up"])
    x = _matmul(act, wts["w_down"], F32, bm_pref=512, bn_pref=256, res=x)
    return x, s_fin, new_buf


def kernel(x_prompt, x_sample, cache_mem_k, cache_mem_v, state_hgrn, state_conv, mem_prompt, norm_mix, w_in, lb_logits, hg_norm, conv_w, w_a, w_b, w_o, norm_xattn, norm_mem, w_xq, w_xk, w_xv, w_xo, norm_ffn, w_gate, w_up, w_down, norm_final):
    depth = norm_mix.shape[0]
    bp, tp, dm = x_prompt.shape
    bs, ts, _ = x_sample.shape
    hw = hg_norm.shape[1]
    n_mem = mem_prompt.shape[1]
    n_xa_heads = cache_mem_k.shape[3]
    nh, dk, dv = state_hgrn.shape[2:]
    assert dk == HEAD_DIM_HGRN and dv == HEAD_DIM_HGRN and nh * dk == hw

    xp = x_prompt.reshape(bp * tp, dm)
    xs = x_sample.reshape(bs * ts, dm)
    outs = {k: [] for k in ("mk", "mv", "sp", "cp", "ss", "cs")}
    for l in range(depth):
        w_in_l = w_in[l]
        wts = {
            "norm_mix": norm_mix[l], "hg_norm": hg_norm[l], "conv_w": conv_w[l],
            "norm_xattn": norm_xattn[l], "norm_ffn": norm_ffn[l],
            "w_in_f": w_in_l[:, hw:2 * hw].astype(BF16),
            "w_in_rest": jnp.concatenate([w_in_l[:, :hw], w_in_l[:, 2 * hw:]], axis=1).astype(BF16),
            "w_a": w_a[l].astype(BF16), "w_b": w_b[l].astype(BF16), "w_o": w_o[l].astype(BF16),
            "w_xq": w_xq[l].astype(BF16), "w_xo": w_xo[l].astype(BF16),
            "w_gate": w_gate[l].astype(BF16), "w_up": w_up[l].astype(BF16),
            "w_down": w_down[l].astype(BF16),
        }
        mem_n = _rmsnorm(mem_prompt.reshape(bp * n_mem, dm), norm_mem[l], BF16)
        mk_p = _matmul(mem_n, w_xk[l].astype(BF16), F32)
        mv_p = _matmul(mem_n, w_xv[l].astype(BF16), F32)
        s0 = jnp.zeros((bp, nh, dk, dv), F32)
        buf0 = jnp.zeros((bp, conv_w.shape[1] - 1, conv_w.shape[2]), F32)
        xp, s_p, buf_p = _layer(xp, bp, tp, wts, lb_logits, l, s0, buf0,
                                mk_p.astype(BF16).reshape(bp, n_mem, dm),
                                mv_p.astype(BF16).reshape(bp, n_mem, dm), n_xa_heads)
        xs, s_s, buf_s = _layer(xs, bs, ts, wts, lb_logits, l, state_hgrn[l], state_conv[l],
                                cache_mem_k[l].astype(BF16).reshape(bs, n_mem, dm),
                                cache_mem_v[l].astype(BF16).reshape(bs, n_mem, dm), n_xa_heads)
        outs["mk"].append(mk_p.reshape(bp, n_mem, n_xa_heads, dm // n_xa_heads))
        outs["mv"].append(mv_p.reshape(bp, n_mem, n_xa_heads, dm // n_xa_heads))
        outs["sp"].append(s_p)
        outs["cp"].append(buf_p)
        outs["ss"].append(s_s)
        outs["cs"].append(buf_s)
    y_prompt = _rmsnorm(xp, norm_final, F32).reshape(bp, tp, dm)
    y_sample = _rmsnorm(xs, norm_final, F32).reshape(bs, ts, dm)
    return (y_prompt, y_sample, jnp.stack(outs["mk"]), jnp.stack(outs["mv"]), jnp.stack(outs["sp"]),
            jnp.stack(outs["cp"]), jnp.stack(outs["ss"]), jnp.stack(outs["cs"]))
```

```python
import functools

import numpy as np
import jax
import jax.numpy as jnp
from jax import lax
from jax.experimental import pallas as pl
from jax.experimental.pallas import tpu as pltpu

EPS = 1e-6
LANES = 128
SUBLANES = 8
HEAD_DIM_HGRN = 128
CHUNK = 64
VMEM_LIMIT_BYTES = 56 * 1024 * 1024
BF16 = jnp.bfloat16
F32 = jnp.float32


def _params(*semantics):
    return pltpu.CompilerParams(dimension_semantics=semantics, vmem_limit_bytes=VMEM_LIMIT_BYTES)


def _tile(n, pref, mult):
    if n <= pref:
        return n
    t = (pref // mult) * mult
    while t >= mult:
        if n % t == 0:
            return t
        t -= mult
    raise ValueError(f"no tile for {n} (pref {pref}, mult {mult})")


def _rmsnorm_kernel(x_ref, g_ref, o_ref):
    x = x_ref[...].astype(F32)
    y = x * lax.rsqrt(jnp.mean(x * x, axis=-1, keepdims=True) + EPS)
    o_ref[...] = (y * g_ref[...]).astype(o_ref.dtype)


def _rmsnorm(x, g, out_dtype):
    m, d = x.shape
    bm = _tile(m, 512, 8)
    return pl.pallas_call(
        _rmsnorm_kernel,
        grid=(m // bm,),
        in_specs=[pl.BlockSpec((bm, d), lambda i: (i, 0)),
                  pl.BlockSpec((1, d), lambda i: (0, 0))],
        out_specs=pl.BlockSpec((bm, d), lambda i: (i, 0)),
        out_shape=jax.ShapeDtypeStruct((m, d), out_dtype),
        compiler_params=_params("parallel"),
        name="rmsnorm",
    )(x, g.reshape(1, d).astype(F32))


def _mm_kernel(x_ref, w_ref, o_ref):
    o_ref[...] = jnp.dot(x_ref[...], w_ref[...], preferred_element_type=F32).astype(o_ref.dtype)


def _mm_res_kernel(x_ref, w_ref, r_ref, o_ref):
    o_ref[...] = r_ref[...] + jnp.dot(x_ref[...], w_ref[...], preferred_element_type=F32)


def _mm_gated_kernel(a_ref, b_ref, wa_ref, wb_ref, ga_ref, gb_ref, o_ref):
    a = jnp.dot(a_ref[...], wa_ref[...], preferred_element_type=F32)
    b = jnp.dot(b_ref[...], wb_ref[...], preferred_element_type=F32)
    ga = jax.nn.sigmoid(ga_ref[...].astype(F32))
    gb = jax.nn.sigmoid(gb_ref[...].astype(F32))
    o_ref[...] = (ga * a + gb * b).astype(o_ref.dtype)


def _mm_swiglu_kernel(x_ref, wg_ref, wu_ref, o_ref):
    x = x_ref[...]
    g = jnp.dot(x, wg_ref[...], preferred_element_type=F32)
    u = jnp.dot(x, wu_ref[...], preferred_element_type=F32)
    o_ref[...] = (g * jax.nn.sigmoid(g) * u).astype(o_ref.dtype)


def _matmul(x, w, out_dtype, *, bm_pref=1024, bn_pref=512, res=None):
    m, k = x.shape
    n = w.shape[1]
    bm = _tile(m, bm_pref, 8)
    bn = _tile(n, bn_pref, LANES)
    in_specs = [pl.BlockSpec((bm, k), lambda i, j: (i, 0)),
                pl.BlockSpec((k, bn), lambda i, j: (0, j))]
    args = [x, w]
    body = _mm_kernel
    if res is not None:
        in_specs.append(pl.BlockSpec((bm, bn), lambda i, j: (i, j)))
        args.append(res)
        body = _mm_res_kernel
    return pl.pallas_call(
        body,
        grid=(m // bm, n // bn),
        in_specs=in_specs,
        out_specs=pl.BlockSpec((bm, bn), lambda i, j: (i, j)),
        out_shape=jax.ShapeDtypeStruct((m, n), out_dtype),
        compiler_params=_params("parallel", "parallel"),
        name="matmul_res" if res is not None else "matmul",
    )(*args)


def _matmul_gated(a, b, wa, wb, proj, ga_col, gb_col):
    m, ka = a.shape
    kb = b.shape[1]
    n = wa.shape[1]
    bm = _tile(m, 1024, 8)
    bn = _tile(n, 512, LANES)
    assert ga_col % bn == 0 and gb_col % bn == 0
    ga_blk, gb_blk = ga_col // bn, gb_col // bn
    return pl.pallas_call(
        _mm_gated_kernel,
        grid=(m // bm, n // bn),
        in_specs=[pl.BlockSpec((bm, ka), lambda i, j: (i, 0)),
                  pl.BlockSpec((bm, kb), lambda i, j: (i, 0)),
                  pl.BlockSpec((ka, bn), lambda i, j: (0, j)),
                  pl.BlockSpec((kb, bn), lambda i, j: (0, j)),
                  pl.BlockSpec((bm, bn), lambda i, j: (i, ga_blk + j)),
                  pl.BlockSpec((bm, bn), lambda i, j: (i, gb_blk + j))],
        out_specs=pl.BlockSpec((bm, bn), lambda i, j: (i, j)),
        out_shape=jax.ShapeDtypeStruct((m, n), BF16),
        compiler_params=_params("parallel", "parallel"),
        name="matmul_gated",
    )(a, b, wa, wb, proj, proj)


def _matmul_swiglu(x, wg, wu):
    m, k = x.shape
    n = wg.shape[1]
    bm = _tile(m, 1024, 8)
    bn = _tile(n, 256, LANES)
    return pl.pallas_call(
        _mm_swiglu_kernel,
        grid=(m // bm, n // bn),
        in_specs=[pl.BlockSpec((bm, k), lambda i, j: (i, 0)),
                  pl.BlockSpec((k, bn), lambda i, j: (0, j)),
                  pl.BlockSpec((k, bn), lambda i, j: (0, j))],
        out_specs=pl.BlockSpec((bm, bn), lambda i, j: (i, j)),
        out_shape=jax.ShapeDtypeStruct((m, n), BF16),
        compiler_params=_params("parallel", "parallel"),
        name="matmul_swiglu",
    )(x, wg, wu)


def _hgrn_tables(c, rows, pb):
    t = np.arange(rows)[:, None]
    s = np.arange(rows)[None, :]
    tri = ((t // c) == (s // c)) & (s <= t)
    t, s = t[:pb, :pb], s[:pb, :pb]
    masks = []
    l = 1
    while l < c:
        masks.append(((t // l) % 2 == 1) & ((s // l) == (t // l) - 1))
        l *= 2
    masks.append(t == s)
    return tri.astype(np.float32), np.stack(masks, 0).astype(np.float32)


def _hgrn_kernel(tri_ref, mask_ref, lbl_ref, q_ref, f_ref, v_ref, og_ref, gn_ref, s0_ref,
                 o_ref, sfin_ref, st_ref, oi_ref, qin_ref, u_ref, dec_ref,
                 *, c, n_chunks, rows, pb, layer):
    t_idx = pl.program_id(2)
    d = HEAD_DIM_HGRN
    n_lv = c.bit_length() - 1
    g_chunks = rows // c
    n_pb = rows // pb
    nt = (((1,), (1,)), ((), ()))
    tn = (((0,), (0,)), ((), ()))

    @pl.when(t_idx == 0)
    def _():
        st_ref[...] = s0_ref[0, 0].T

    logits = lbl_ref[...].astype(F32)
    ex = jnp.exp(logits - jnp.max(logits, axis=0, keepdims=True))
    lb = jnp.sum(ex[:layer + 1], axis=0, keepdims=True) / jnp.sum(ex, axis=0, keepdims=True)
    gain = gn_ref[...].astype(F32)
    shape3 = (rows // SUBLANES, SUBLANES, d)
    sub = lax.broadcasted_iota(jnp.int32, shape3, 1)

    def block_rows(x, first, step, length):
        pieces = [jnp.broadcast_to(x[first + step * p:first + step * p + 1, :], (length, d))
                  for p in range(rows // length)]
        return pieces[0] if len(pieces) == 1 else jnp.concatenate(pieces, axis=0)

    def pair_ref(b, l):
        if l >= SUBLANES:
            return block_rows(b, l - 1, 2 * l, 2 * l)
        b3 = b.reshape(shape3)
        pick = lambda i: jnp.broadcast_to(b3[:, i:i + 1, :], shape3)
        r3 = pick(3) if l == 4 else jnp.where(sub < 4, pick(1), pick(5))
        return r3.reshape(rows, d)

    def odd_half(x_odd, x_even, l):
        if l >= SUBLANES:
            pieces = [(x_odd if p % 2 else x_even)[p * l:(p + 1) * l] for p in range(rows // l)]
            return jnp.concatenate(pieces, axis=0)
        return jnp.where((sub & l) != 0, x_odd.reshape(shape3), x_even.reshape(shape3)).reshape(rows, d)

    def phase_a(bi, carry):
        rsel = pl.ds(pl.multiple_of(bi * rows, rows), rows)
        f = lb + (1.0 - lb) * jax.nn.sigmoid(f_ref[0, rsel, :])
        g = jnp.log(f)
        k = 1.0 - f
        qb = q_ref[0, rsel, :]
        q = qb.astype(F32)
        v = v_ref[0, rsel, :]

        g_hi = g.astype(BF16)
        r1 = g - g_hi.astype(F32)
        g_mid = r1.astype(BF16)
        g_lo = (r1 - g_mid.astype(F32)).astype(BF16)
        b3 = jnp.dot(tri_ref[...], jnp.concatenate([g_hi, g_mid, g_lo], axis=1),
                     preferred_element_type=F32)
        b = b3[:, 0:d] + b3[:, d:2 * d] + b3[:, 2 * d:3 * d]

        def scores(zl, zr, li):
            p = lax.dot_general(zl, zr, nt, preferred_element_type=F32)
            return [mask_ref[li] * p[i * pb:(i + 1) * pb, i * pb:(i + 1) * pb] for i in range(n_pb)]

        a = scores(qb, k.astype(BF16), n_lv)
        z = odd_half(q * f, k, 1).astype(BF16)
        a = [x + y for x, y in zip(a, scores(z, z, 0))]
        for li in range(1, n_lv):
            l = 1 << li
            w = jnp.exp(-jnp.abs(b - pair_ref(b, l)))
            z = (odd_half(q, k, l) * w).astype(BF16)
            a = [x + y for x, y in zip(a, scores(z, z, li))]
        for i in range(n_pb):
            oi_ref[pl.ds(pl.multiple_of(bi * rows + i * pb, pb), pb), :] = jnp.dot(
                a[i].astype(BF16), v[i * pb:(i + 1) * pb], preferred_element_type=F32)

        dec = jnp.exp(b)
        qin_ref[rsel, :] = (q * dec).astype(BF16)
        k_out = (k * jnp.exp(block_rows(b, c - 1, c, c) - b)).astype(BF16)
        for ci in range(g_chunks):
            cr = slice(ci * c, (ci + 1) * c)
            u_ref[bi * g_chunks + ci] = lax.dot_general(v[cr], k_out[cr], tn, preferred_element_type=F32)
            dec_ref[pl.ds(pl.multiple_of((bi * g_chunks + ci) * SUBLANES, SUBLANES), SUBLANES), :] = (
                jnp.broadcast_to(dec[ci * c + c - 1:ci * c + c, :], (SUBLANES, d)))
        return carry

    def phase_b(ci, carry):
        rsel = pl.ds(pl.multiple_of(ci * c, c), c)
        st = st_ref[...]
        oi_ref[rsel, :] = oi_ref[rsel, :] + lax.dot_general(
            qin_ref[rsel, :], st.astype(BF16), nt, preferred_element_type=F32)
        st_ref[...] = st * dec_ref[pl.ds(pl.multiple_of(ci * SUBLANES, SUBLANES), 1), :] + u_ref[ci]
        return carry

    def phase_c(ci, carry):
        rsel = pl.ds(pl.multiple_of(ci * c, c), c)
        o = oi_ref[rsel, :]
        y = o * lax.rsqrt(jnp.mean(o * o, axis=-1, keepdims=True) + EPS) * gain
        og = og_ref[0, rsel, :].astype(F32)
        o_ref[0, rsel, :] = (y * (og * jax.nn.sigmoid(og))).astype(o_ref.dtype)
        return carry

    unroll = max(u for u in (1, 2, 4, 8) if n_chunks % u == 0)
    lax.fori_loop(0, n_chunks // g_chunks, phase_a, 0)
    lax.fori_loop(0, n_chunks, phase_b, 0, unroll=unroll)
    lax.fori_loop(0, n_chunks, phase_c, 0, unroll=unroll)

    @pl.when(t_idx == pl.num_programs(2) - 1)
    def _():
        sfin_ref[0, 0] = st_ref[...].T


def _hgrn(proj3, fpre3, lb_logits, gain, s0, layer, q_col, v_col, og_col):
    bsz, t, _ = proj3.shape
    hw = fpre3.shape[-1]
    d = HEAD_DIM_HGRN
    nh = hw // d
    c = CHUNK if t % CHUNK == 0 else t
    assert c & (c - 1) == 0 and c >= 2 * SUBLANES and t % c == 0
    tt = _tile(t, 1024, c)
    n_chunks = tt // c
    rows = _tile(tt, 2 * LANES, c)
    pb = min(rows, LANES)
    tri, masks = _hgrn_tables(c, rows, pb)
    qb, vb, ob = q_col // d, v_col // d, og_col // d
    kern = functools.partial(_hgrn_kernel, c=c, n_chunks=n_chunks, rows=rows, pb=pb, layer=layer)
    nl = lb_logits.shape[0]
    return pl.pallas_call(
        kern,
        grid=(bsz, nh, t // tt),
        in_specs=[pl.BlockSpec((rows, rows), lambda b, h, i: (0, 0)),
                  pl.BlockSpec(masks.shape, lambda b, h, i: (0, 0, 0)),
                  pl.BlockSpec((nl, d), lambda b, h, i: (0, h)),
                  pl.BlockSpec((1, tt, d), lambda b, h, i: (b, i, qb + h)),
                  pl.BlockSpec((1, tt, d), lambda b, h, i: (b, i, h)),
                  pl.BlockSpec((1, tt, d), lambda b, h, i: (b, i, vb + h)),
                  pl.BlockSpec((1, tt, d), lambda b, h, i: (b, i, ob + h)),
                  pl.BlockSpec((1, d), lambda b, h, i: (0, h)),
                  pl.BlockSpec((1, 1, d, d), lambda b, h, i: (b, h, 0, 0))],
        out_specs=[pl.BlockSpec((1, tt, d), lambda b, h, i: (b, i, h)),
                   pl.BlockSpec((1, 1, d, d), lambda b, h, i: (b, h, 0, 0))],
        out_shape=[jax.ShapeDtypeStruct((bsz, t, hw), BF16),
                   jax.ShapeDtypeStruct((bsz, nh, d, d), F32)],
        scratch_shapes=[pltpu.VMEM((d, d), F32),
                        pltpu.VMEM((tt, d), F32),
                        pltpu.VMEM((tt, d), BF16),
                        pltpu.VMEM((n_chunks, d, d), F32),
                        pltpu.VMEM((n_chunks * SUBLANES, d), F32)],
        compiler_params=_params("parallel", "parallel", "arbitrary"),
        name="hgrn2",
    )(jnp.asarray(tri, BF16), jnp.asarray(masks, F32), lb_logits.astype(F32),
      proj3, fpre3, proj3, proj3, gain.reshape(1, hw).astype(F32), s0.astype(F32))


CONV_PAD = 8


def _conv_kernel(cc_ref, ch_ref, cb_ref, w_ref, buf_ref, o_ref, nbuf_ref, u_ref, *, t, k, rb):
    u_ref[CONV_PAD - (k - 1):CONV_PAD, :] = buf_ref[0].astype(F32)
    for r in range(0, t, rb):
        u_ref[CONV_PAD + r:CONV_PAD + r + rb, :] = (
            cc_ref[0, r:r + rb, :].astype(F32) * ch_ref[0, r:r + rb, :].astype(F32))
    w = w_ref[...].astype(F32)
    for r in range(0, t, rb):
        z = w[0:1, :] * u_ref[CONV_PAD - (k - 1) + r:CONV_PAD - (k - 1) + r + rb, :]
        for j in range(1, k):
            s = CONV_PAD - (k - 1) + j + r
            z = z + w[j:j + 1, :] * u_ref[s:s + rb, :]
        o_ref[0, r:r + rb, :] = (cb_ref[0, r:r + rb, :].astype(F32) * z).astype(o_ref.dtype)
    nbuf_ref[0] = u_ref[CONV_PAD + t - (k - 1):CONV_PAD + t, :]


def _conv(proj3, conv_w, buf, ch_col, cb_col, cc_col):
    bsz, t, _ = proj3.shape
    k, cw = conv_w.shape
    assert k - 1 <= CONV_PAD
    d = LANES
    rb = _tile(t, 512, 8)
    kern = functools.partial(_conv_kernel, t=t, k=k, rb=rb)
    return pl.pallas_call(
        kern,
        grid=(bsz, cw // d),
        in_specs=[pl.BlockSpec((1, t, d), lambda b, j: (b, 0, cc_col // d + j)),
                  pl.BlockSpec((1, t, d), lambda b, j: (b, 0, ch_col // d + j)),
                  pl.BlockSpec((1, t, d), lambda b, j: (b, 0, cb_col // d + j)),
                  pl.BlockSpec((k, d), lambda b, j: (0, j)),
                  pl.BlockSpec((1, k - 1, d), lambda b, j: (b, 0, j))],
        out_specs=[pl.BlockSpec((1, t, d), lambda b, j: (b, 0, j)),
                   pl.BlockSpec((1, k - 1, d), lambda b, j: (b, 0, j))],
        out_shape=[jax.ShapeDtypeStruct((bsz, t, cw), BF16),
                   jax.ShapeDtypeStruct((bsz, k - 1, cw), F32)],
        scratch_shapes=[pltpu.VMEM((CONV_PAD + t, d), F32)],
        compiler_params=_params("parallel", "parallel"),
        name="short_conv",
    )(proj3, proj3, proj3, conv_w.astype(F32), buf.astype(F32))


def _attn_kernel(q_ref, k_ref, v_ref, o_ref, *, scale):
    nt = (((1,), (1,)), ((), ()))
    s = lax.dot_general(q_ref[0], k_ref[0], nt, preferred_element_type=F32) * scale
    s = s - jnp.max(s, axis=-1, keepdims=True)
    p = jnp.exp(s)
    p = p / jnp.sum(p, axis=-1, keepdims=True)
    o_ref[0] = jnp.dot(p.astype(BF16), v_ref[0], preferred_element_type=F32).astype(o_ref.dtype)


def _attention(q3, mk3, mv3, n_heads):
    bsz, t, dm = q3.shape
    n_mem = mk3.shape[1]
    hd = dm // n_heads
    bt = _tile(t, 512, 8)
    kern = functools.partial(_attn_kernel, scale=float(hd) ** -0.5)
    return pl.pallas_call(
        kern,
        grid=(bsz, t // bt, n_heads),
        in_specs=[pl.BlockSpec((1, bt, hd), lambda b, i, h: (b, i, h)),
                  pl.BlockSpec((1, n_mem, hd), lambda b, i, h: (b, 0, h)),
                  pl.BlockSpec((1, n_mem, hd), lambda b, i, h: (b, 0, h))],
        out_specs=pl.BlockSpec((1, bt, hd), lambda b, i, h: (b, i, h)),
        out_shape=jax.ShapeDtypeStruct((bsz, t, dm), BF16),
        compiler_params=_params("parallel", "parallel", "parallel"),
        name="cross_attention",
    )(q3, mk3, mv3)


def _layer(x, bsz, t, wts, lb_logits, layer, s0, buf, mk3, mv3, n_xa_heads):
    m, dm = x.shape
    hw = wts["hg_norm"].shape[0]
    cw = wts["conv_w"].shape[1]
    q_col, v_col, og_col = 0, hw, 2 * hw
    ch_col, cb_col, cc_col = 3 * hw, 3 * hw + cw, 3 * hw + 2 * cw
    ga_col, gb_col = 3 * hw + 3 * cw, 3 * hw + 3 * cw + dm

    h = _rmsnorm(x, wts["norm_mix"], BF16)
    proj = _matmul(h, wts["w_in_rest"], BF16)
    fpre = _matmul(h, wts["w_in_f"], F32)
    proj3 = proj.reshape(bsz, t, proj.shape[1])
    a_in, s_fin = _hgrn(proj3, fpre.reshape(bsz, t, hw), lb_logits, wts["hg_norm"], s0, layer,
                        q_col, v_col, og_col)
    bz, new_buf = _conv(proj3, wts["conv_w"], buf, ch_col, cb_col, cc_col)
    merged = _matmul_gated(a_in.reshape(m, hw), bz.reshape(m, cw), wts["w_a"], wts["w_b"],
                           proj, ga_col, gb_col)
    x = _matmul(merged, wts["w_o"], F32, res=x)

    hq = _rmsnorm(x, wts["norm_xattn"], BF16)
    qx = _matmul(hq, wts["w_xq"], BF16)
    att = _attention(qx.reshape(bsz, t, dm), mk3, mv3, n_xa_heads)
    x = _matmul(att.reshape(m, dm), wts["w_xo"], F32, res=x)

    hf = _rmsnorm(x, wts["norm_ffn"], BF16)
    act = _matmul_swiglu(hf, wts["w_gate"], wts["w_up"])
    x = _matmul(act, wts["w_down"], F32, bm_pref=512, bn_pref=256, res=x)
    return x, s_fin, new_buf


def kernel(x_prompt, x_sample, cache_mem_k, cache_mem_v, state_hgrn, state_conv, mem_prompt, norm_mix, w_in, lb_logits, hg_norm, conv_w, w_a, w_b, w_o, norm_xattn, norm_mem, w_xq, w_xk, w_xv, w_xo, norm_ffn, w_gate, w_up, w_down, norm_final):
    depth = norm_mix.shape[0]
    bp, tp, dm = x_prompt.shape
    bs, ts, _ = x_sample.shape
    hw = hg_norm.shape[1]
    n_mem = mem_prompt.shape[1]
    n_xa_heads = cache_mem_k.shape[3]
    nh, dk, dv = state_hgrn.shape[2:]
    assert dk == HEAD_DIM_HGRN and dv == HEAD_DIM_HGRN and nh * dk == hw

    xp = x_prompt.reshape(bp * tp, dm)
    xs = x_sample.reshape(bs * ts, dm)
    outs = {k: [] for k in ("mk", "mv", "sp", "cp", "ss", "cs")}
    for l in range(depth):
        w_in_l = w_in[l]
        wts = {
            "norm_mix": norm_mix[l], "hg_norm": hg_norm[l], "conv_w": conv_w[l],
            "norm_xattn": norm_xattn[l], "norm_ffn": norm_ffn[l],
            "w_in_f": w_in_l[:, hw:2 * hw].astype(BF16),
            "w_in_rest": jnp.concatenate([w_in_l[:, :hw], w_in_l[:, 2 * hw:]], axis=1).astype(BF16),
            "w_a": w_a[l].astype(BF16), "w_b": w_b[l].astype(BF16), "w_o": w_o[l].astype(BF16),
            "w_xq": w_xq[l].astype(BF16), "w_xo": w_xo[l].astype(BF16),
            "w_gate": w_gate[l].astype(BF16), "w_up": w_up[l].astype(BF16),
            "w_down": w_down[l].astype(BF16),
        }
        mem_n = _rmsnorm(mem_prompt.reshape(bp * n_mem, dm), norm_mem[l], BF16)
        mk_p = _matmul(mem_n, w_xk[l].astype(BF16), F32)
        mv_p = _matmul(mem_n, w_xv[l].astype(BF16), F32)
        s0 = jnp.zeros((bp, nh, dk, dv), F32)
        buf0 = jnp.zeros((bp, conv_w.shape[1] - 1, conv_w.shape[2]), F32)
        xp, s_p, buf_p = _layer(xp, bp, tp, wts, lb_logits, l, s0, buf0,
                                mk_p.astype(BF16).reshape(bp, n_mem, dm),
                                mv_p.astype(BF16).reshape(bp, n_mem, dm), n_xa_heads)
        xs, s_s, buf_s = _layer(xs, bs, ts, wts, lb_logits, l, state_hgrn[l], state_conv[l],
                                cache_mem_k[l].astype(BF16).reshape(bs, n_mem, dm),
                                cache_mem_v[l].astype(BF16).reshape(bs, n_mem, dm), n_xa_heads)
        outs["mk"].append(mk_p.reshape(bp, n_mem, n_xa_heads, dm // n_xa_heads))
        outs["mv"].append(mv_p.reshape(bp, n_mem, n_xa_heads, dm // n_xa_heads))
        outs["sp"].append(s_p)
        outs["cp"].append(buf_p)
        outs["ss"].append(s_s)
        outs["cs"].append(buf_s)
    y_prompt = _rmsnorm(xp, norm_final, F32).reshape(bp, tp, dm)
    y_sample = _rmsnorm(xs, norm_final, F32).reshape(bs, ts, dm)
    return (y_prompt, y_sample, jnp.stack(outs["mk"]), jnp.stack(outs["mv"]), jnp.stack(outs["sp"]),
            jnp.stack(outs["cp"]), jnp.stack(outs["ss"]), jnp.stack(outs["cs"]))
```

```python
import functools

import numpy as np
import jax
import jax.numpy as jnp
from jax import lax
from jax.experimental import pallas as pl
from jax.experimental.pallas import tpu as pltpu

EPS = 1e-6
LANES = 128
SUBLANES = 8
HEAD_DIM_HGRN = 128
CHUNK = 64
VMEM_LIMIT_BYTES = 56 * 1024 * 1024
BF16 = jnp.bfloat16
F32 = jnp.float32


def _params(*semantics):
    return pltpu.CompilerParams(dimension_semantics=semantics, vmem_limit_bytes=VMEM_LIMIT_BYTES)


def _tile(n, pref, mult):
    if n <= pref:
        return n
    t = (pref // mult) * mult
    while t >= mult:
        if n % t == 0:
            return t
        t -= mult
    raise ValueError(f"no tile for {n} (pref {pref}, mult {mult})")


def _rmsnorm_kernel(x_ref, g_ref, o_ref):
    x = x_ref[...].astype(F32)
    y = x * lax.rsqrt(jnp.mean(x * x, axis=-1, keepdims=True) + EPS)
    o_ref[...] = (y * g_ref[...]).astype(o_ref.dtype)


def _rmsnorm(x, g, out_dtype):
    m, d = x.shape
    bm = _tile(m, 512, 8)
    return pl.pallas_call(
        _rmsnorm_kernel,
        grid=(m // bm,),
        in_specs=[pl.BlockSpec((bm, d), lambda i: (i, 0)),
                  pl.BlockSpec((1, d), lambda i: (0, 0))],
        out_specs=pl.BlockSpec((bm, d), lambda i: (i, 0)),
        out_shape=jax.ShapeDtypeStruct((m, d), out_dtype),
        compiler_params=_params("parallel"),
        name="rmsnorm",
    )(x, g.reshape(1, d).astype(F32))


def _mm_kernel(x_ref, w_ref, o_ref):
    o_ref[...] = jnp.dot(x_ref[...], w_ref[...], preferred_element_type=F32).astype(o_ref.dtype)


def _mm_res_kernel(x_ref, w_ref, r_ref, o_ref):
    o_ref[...] = r_ref[...] + jnp.dot(x_ref[...], w_ref[...], preferred_element_type=F32)


def _mm_gated_kernel(a_ref, b_ref, wa_ref, wb_ref, ga_ref, gb_ref, o_ref):
    a = jnp.dot(a_ref[...], wa_ref[...], preferred_element_type=F32)
    b = jnp.dot(b_ref[...], wb_ref[...], preferred_element_type=F32)
    ga = jax.nn.sigmoid(ga_ref[...].astype(F32))
    gb = jax.nn.sigmoid(gb_ref[...].astype(F32))
    o_ref[...] = (ga * a + gb * b).astype(o_ref.dtype)


def _mm_swiglu_kernel(x_ref, wg_ref, wu_ref, o_ref):
    x = x_ref[...]
    g = jnp.dot(x, wg_ref[...], preferred_element_type=F32)
    u = jnp.dot(x, wu_ref[...], preferred_element_type=F32)
    o_ref[...] = (g * jax.nn.sigmoid(g) * u).astype(o_ref.dtype)


def _matmul(x, w, out_dtype, *, bm_pref=1024, bn_pref=1024, res=None, cols=None):
    m, k = x.shape
    cols = cols or [(0, w.shape[1])]
    n = sum(width for _, width in cols)
    bm = _tile(m, bm_pref, 8)
    bn = _tile(int(functools.reduce(np.gcd, [v for seg in cols for v in seg if v])), bn_pref, LANES)
    steps, out_blk = [], 0
    for start, width in cols:
        steps.append((out_blk, start // bn - out_blk))
        out_blk += width // bn

    def w_block(j):
        blk = j + steps[0][1]
        for (first, shift), (_, prev) in zip(steps[1:], steps[:-1]):
            blk = blk + jnp.where(j >= first, shift - prev, 0)
        return blk

    in_specs = [pl.BlockSpec((bm, k), lambda i, j: (i, 0)),
                pl.BlockSpec((k, bn), lambda i, j: (0, w_block(j)))]
    args = [x, w]
    body = _mm_kernel
    if res is not None:
        in_specs.append(pl.BlockSpec((bm, bn), lambda i, j: (i, j)))
        args.append(res)
        body = _mm_res_kernel
    return pl.pallas_call(
        body,
        grid=(m // bm, n // bn),
        in_specs=in_specs,
        out_specs=pl.BlockSpec((bm, bn), lambda i, j: (i, j)),
        out_shape=jax.ShapeDtypeStruct((m, n), out_dtype),
        compiler_params=_params("parallel", "parallel"),
        name="matmul_res" if res is not None else "matmul",
    )(*args)


def _matmul_gated(a, b, wa, wb, proj, ga_col, gb_col):
    m, ka = a.shape
    kb = b.shape[1]
    n = wa.shape[1]
    bm = _tile(m, 1024, 8)
    bn = _tile(n, 1024, LANES)
    assert ga_col % bn == 0 and gb_col % bn == 0
    ga_blk, gb_blk = ga_col // bn, gb_col // bn
    return pl.pallas_call(
        _mm_gated_kernel,
        grid=(m // bm, n // bn),
        in_specs=[pl.BlockSpec((bm, ka), lambda i, j: (i, 0)),
                  pl.BlockSpec((bm, kb), lambda i, j: (i, 0)),
                  pl.BlockSpec((ka, bn), lambda i, j: (0, j)),
                  pl.BlockSpec((kb, bn), lambda i, j: (0, j)),
                  pl.BlockSpec((bm, bn), lambda i, j: (i, ga_blk + j)),
                  pl.BlockSpec((bm, bn), lambda i, j: (i, gb_blk + j))],
        out_specs=pl.BlockSpec((bm, bn), lambda i, j: (i, j)),
        out_shape=jax.ShapeDtypeStruct((m, n), BF16),
        compiler_params=_params("parallel", "parallel"),
        name="matmul_gated",
    )(a, b, wa, wb, proj, proj)


def _matmul_swiglu(x, wg, wu):
    m, k = x.shape
    n = wg.shape[1]
    bm = _tile(m, 2048, 8)
    bn = _tile(n, 256, LANES)
    return pl.pallas_call(
        _mm_swiglu_kernel,
        grid=(m // bm, n // bn),
        in_specs=[pl.BlockSpec((bm, k), lambda i, j: (i, 0)),
                  pl.BlockSpec((k, bn), lambda i, j: (0, j)),
                  pl.BlockSpec((k, bn), lambda i, j: (0, j))],
        out_specs=pl.BlockSpec((bm, bn), lambda i, j: (i, j)),
        out_shape=jax.ShapeDtypeStruct((m, n), BF16),
        compiler_params=_params("parallel", "parallel"),
        name="matmul_swiglu",
    )(x, wg, wu)


def _hgrn_tables(c, rows, pb):
    t = np.arange(rows)[:, None]
    s = np.arange(rows)[None, :]
    tri = ((t // c) == (s // c)) & (s <= t)
    t, s = t[:pb, :pb], s[:pb, :pb]
    masks = []
    l = 1
    while l < c:
        masks.append(((t // l) % 2 == 1) & ((s // l) == (t // l) - 1))
        l *= 2
    masks.append(t == s)
    return tri.astype(np.float32), np.stack(masks, 0).astype(np.float32)


def _hgrn_kernel(tri_ref, mask_ref, lbl_ref, q_ref, f_ref, v_ref, og_ref, gn_ref, s0_ref,
                 o_ref, sfin_ref, st_ref, oi_ref, qin_ref, u_ref, dec_ref,
                 *, c, n_chunks, rows, pb, layer):
    t_idx = pl.program_id(2)
    d = HEAD_DIM_HGRN
    n_lv = c.bit_length() - 1
    g_chunks = rows // c
    n_pb = rows // pb
    nt = (((1,), (1,)), ((), ()))
    tn = (((0,), (0,)), ((), ()))

    @pl.when(t_idx == 0)
    def _():
        st_ref[...] = s0_ref[0, 0].T

    logits = lbl_ref[...].astype(F32)
    ex = jnp.exp(logits - jnp.max(logits, axis=0, keepdims=True))
    lb = jnp.sum(ex[:layer + 1], axis=0, keepdims=True) / jnp.sum(ex, axis=0, keepdims=True)
    gain = gn_ref[...].astype(F32)
    shape3 = (rows // SUBLANES, SUBLANES, d)
    sub = lax.broadcasted_iota(jnp.int32, shape3, 1)

    def block_rows(x, first, step, length):
        pieces = [jnp.broadcast_to(x[first + step * p:first + step * p + 1, :], (length, d))
                  for p in range(rows // length)]
        return pieces[0] if len(pieces) == 1 else jnp.concatenate(pieces, axis=0)

    def pair_ref(b, l):
        if l >= SUBLANES:
            return block_rows(b, l - 1, 2 * l, 2 * l)
        b3 = b.reshape(shape3)
        pick = lambda i: jnp.broadcast_to(b3[:, i:i + 1, :], shape3)
        r3 = pick(3) if l == 4 else jnp.where(sub < 4, pick(1), pick(5))
        return r3.reshape(rows, d)

    def odd_half(x_odd, x_even, l):
        if l >= SUBLANES:
            pieces = [(x_odd if p % 2 else x_even)[p * l:(p + 1) * l] for p in range(rows // l)]
            return jnp.concatenate(pieces, axis=0)
        return jnp.where((sub & l) != 0, x_odd.reshape(shape3), x_even.reshape(shape3)).reshape(rows, d)

    def phase_a(bi, carry):
        rsel = pl.ds(pl.multiple_of(bi * rows, rows), rows)
        f = lb + (1.0 - lb) * jax.nn.sigmoid(f_ref[0, rsel, :])
        g = jnp.log(f)
        k = 1.0 - f
        qb = q_ref[0, rsel, :]
        q = qb.astype(F32)
        v = v_ref[0, rsel, :]

        g_hi = g.astype(BF16)
        r1 = g - g_hi.astype(F32)
        g_mid = r1.astype(BF16)
        g_lo = (r1 - g_mid.astype(F32)).astype(BF16)
        b3 = jnp.dot(tri_ref[...], jnp.concatenate([g_hi, g_mid, g_lo], axis=1),
                     preferred_element_type=F32)
        b = b3[:, 0:d] + b3[:, d:2 * d] + b3[:, 2 * d:3 * d]

        def scores(zl, zr, li):
            p = lax.dot_general(zl, zr, nt, preferred_element_type=F32)
            return [mask_ref[li] * p[i * pb:(i + 1) * pb, i * pb:(i + 1) * pb] for i in range(n_pb)]

        a = scores(qb, k.astype(BF16), n_lv)
        z = odd_half(q * f, k, 1).astype(BF16)
        a = [x + y for x, y in zip(a, scores(z, z, 0))]
        for li in range(1, n_lv):
            l = 1 << li
            w = jnp.exp(-jnp.abs(b - pair_ref(b, l)))
            z = (odd_half(q, k, l) * w).astype(BF16)
            a = [x + y for x, y in zip(a, scores(z, z, li))]
        for i in range(n_pb):
            oi_ref[pl.ds(pl.multiple_of(bi * rows + i * pb, pb), pb), :] = jnp.dot(
                a[i].astype(BF16), v[i * pb:(i + 1) * pb], preferred_element_type=F32)

        dec = jnp.exp(b)
        qin_ref[rsel, :] = (q * dec).astype(BF16)
        k_out = (k * jnp.exp(block_rows(b, c - 1, c, c) - b)).astype(BF16)
        for ci in range(g_chunks):
            cr = slice(ci * c, (ci + 1) * c)
            u_ref[bi * g_chunks + ci] = lax.dot_general(v[cr], k_out[cr], tn, preferred_element_type=F32)
            dec_ref[pl.ds(pl.multiple_of((bi * g_chunks + ci) * SUBLANES, SUBLANES), SUBLANES), :] = (
                jnp.broadcast_to(dec[ci * c + c - 1:ci * c + c, :], (SUBLANES, d)))
        return carry

    def phase_b(ci, carry):
        rsel = pl.ds(pl.multiple_of(ci * c, c), c)
        st = st_ref[...]
        oi_ref[rsel, :] = oi_ref[rsel, :] + lax.dot_general(
            qin_ref[rsel, :], st.astype(BF16), nt, preferred_element_type=F32)
        st_ref[...] = st * dec_ref[pl.ds(pl.multiple_of(ci * SUBLANES, SUBLANES), 1), :] + u_ref[ci]
        return carry

    def phase_c(ci, carry):
        rsel = pl.ds(pl.multiple_of(ci * c, c), c)
        o = oi_ref[rsel, :]
        y = o * lax.rsqrt(jnp.mean(o * o, axis=-1, keepdims=True) + EPS) * gain
        og = og_ref[0, rsel, :].astype(F32)
        o_ref[0, rsel, :] = (y * (og * jax.nn.sigmoid(og))).astype(o_ref.dtype)
        return carry

    unroll = max(u for u in (1, 2, 4, 8) if n_chunks % u == 0)
    lax.fori_loop(0, n_chunks // g_chunks, phase_a, 0)
    lax.fori_loop(0, n_chunks, phase_b, 0, unroll=unroll)
    lax.fori_loop(0, n_chunks, phase_c, 0, unroll=unroll)

    @pl.when(t_idx == pl.num_programs(2) - 1)
    def _():
        sfin_ref[0, 0] = st_ref[...].T


def _hgrn(proj3, fpre3, lb_logits, gain, s0, layer, q_col, v_col, og_col):
    bsz, t, _ = proj3.shape
    hw = fpre3.shape[-1]
    d = HEAD_DIM_HGRN
    nh = hw // d
    c = CHUNK if t % CHUNK == 0 else t
    assert c & (c - 1) == 0 and c >= 2 * SUBLANES and t % c == 0
    tt = _tile(t, 1024, c)
    n_chunks = tt // c
    rows = _tile(tt, 2 * LANES, c)
    pb = min(rows, LANES)
    tri, masks = _hgrn_tables(c, rows, pb)
    qb, vb, ob = q_col // d, v_col // d, og_col // d
    kern = functools.partial(_hgrn_kernel, c=c, n_chunks=n_chunks, rows=rows, pb=pb, layer=layer)
    nl = lb_logits.shape[0]
    return pl.pallas_call(
        kern,
        grid=(bsz, nh, t // tt),
        in_specs=[pl.BlockSpec((rows, rows), lambda b, h, i: (0, 0)),
                  pl.BlockSpec(masks.shape, lambda b, h, i: (0, 0, 0)),
                  pl.BlockSpec((nl, d), lambda b, h, i: (0, h)),
                  pl.BlockSpec((1, tt, d), lambda b, h, i: (b, i, qb + h)),
                  pl.BlockSpec((1, tt, d), lambda b, h, i: (b, i, h)),
                  pl.BlockSpec((1, tt, d), lambda b, h, i: (b, i, vb + h)),
                  pl.BlockSpec((1, tt, d), lambda b, h, i: (b, i, ob + h)),
                  pl.BlockSpec((1, d), lambda b, h, i: (0, h)),
                  pl.BlockSpec((1, 1, d, d), lambda b, h, i: (b, h, 0, 0))],
        out_specs=[pl.BlockSpec((1, tt, d), lambda b, h, i: (b, i, h)),
                   pl.BlockSpec((1, 1, d, d), lambda b, h, i: (b, h, 0, 0))],
        out_shape=[jax.ShapeDtypeStruct((bsz, t, hw), BF16),
                   jax.ShapeDtypeStruct((bsz, nh, d, d), F32)],
        scratch_shapes=[pltpu.VMEM((d, d), F32),
                        pltpu.VMEM((tt, d), F32),
                        pltpu.VMEM((tt, d), BF16),
                        pltpu.VMEM((n_chunks, d, d), F32),
                        pltpu.VMEM((n_chunks * SUBLANES, d), F32)],
        compiler_params=_params("parallel", "parallel", "arbitrary"),
        name="hgrn2",
    )(jnp.asarray(tri, BF16), jnp.asarray(masks, F32), lb_logits.astype(F32),
      proj3, fpre3, proj3, proj3, gain.reshape(1, hw).astype(F32), s0.astype(F32))


CONV_PAD = 8
CONV_BLOCK_ELEMS = 1024 * 1024


def _conv_kernel(cc_ref, ch_ref, cb_ref, w_ref, buf_ref, o_ref, nbuf_ref, u_ref, *, t, k, rb):
    u_ref[CONV_PAD - (k - 1):CONV_PAD, :] = buf_ref[0].astype(F32)
    for r in range(0, t, rb):
        u_ref[CONV_PAD + r:CONV_PAD + r + rb, :] = (
            cc_ref[0, r:r + rb, :].astype(F32) * ch_ref[0, r:r + rb, :].astype(F32))
    w = w_ref[...].astype(F32)
    for r in range(0, t, rb):
        z = w[0:1, :] * u_ref[CONV_PAD - (k - 1) + r:CONV_PAD - (k - 1) + r + rb, :]
        for j in range(1, k):
            s = CONV_PAD - (k - 1) + j + r
            z = z + w[j:j + 1, :] * u_ref[s:s + rb, :]
        o_ref[0, r:r + rb, :] = (cb_ref[0, r:r + rb, :].astype(F32) * z).astype(o_ref.dtype)
    nbuf_ref[0] = u_ref[CONV_PAD + t - (k - 1):CONV_PAD + t, :]


def _conv(proj3, conv_w, buf, ch_col, cb_col, cc_col):
    bsz, t, _ = proj3.shape
    k, cw = conv_w.shape
    assert k - 1 <= CONV_PAD
    d = _tile(cw, max(LANES, CONV_BLOCK_ELEMS // t), LANES)
    rb = _tile(t, max(8, 512 * LANES // d), 8)
    kern = functools.partial(_conv_kernel, t=t, k=k, rb=rb)
    return pl.pallas_call(
        kern,
        grid=(bsz, cw // d),
        in_specs=[pl.BlockSpec((1, t, d), lambda b, j: (b, 0, cc_col // d + j)),
                  pl.BlockSpec((1, t, d), lambda b, j: (b, 0, ch_col // d + j)),
                  pl.BlockSpec((1, t, d), lambda b, j: (b, 0, cb_col // d + j)),
                  pl.BlockSpec((k, d), lambda b, j: (0, j)),
                  pl.BlockSpec((1, k - 1, d), lambda b, j: (b, 0, j))],
        out_specs=[pl.BlockSpec((1, t, d), lambda b, j: (b, 0, j)),
                   pl.BlockSpec((1, k - 1, d), lambda b, j: (b, 0, j))],
        out_shape=[jax.ShapeDtypeStruct((bsz, t, cw), BF16),
                   jax.ShapeDtypeStruct((bsz, k - 1, cw), F32)],
        scratch_shapes=[pltpu.VMEM((CONV_PAD + t, d), F32)],
        compiler_params=_params("parallel", "parallel"),
        name="short_conv",
    )(proj3, proj3, proj3, conv_w.astype(F32), buf.astype(F32))


def _attn_kernel(q_ref, k_ref, v_ref, o_ref, *, scale):
    nt = (((1,), (1,)), ((), ()))
    s = lax.dot_general(q_ref[0], k_ref[0], nt, preferred_element_type=F32) * scale
    s = s - jnp.max(s, axis=-1, keepdims=True)
    p = jnp.exp(s)
    p = p / jnp.sum(p, axis=-1, keepdims=True)
    o_ref[0] = jnp.dot(p.astype(BF16), v_ref[0], preferred_element_type=F32).astype(o_ref.dtype)


def _attention(q3, mk3, mv3, n_heads):
    bsz, t, dm = q3.shape
    n_mem = mk3.shape[1]
    hd = dm // n_heads
    bt = _tile(t, 512, 8)
    kern = functools.partial(_attn_kernel, scale=float(hd) ** -0.5)
    return pl.pallas_call(
        kern,
        grid=(bsz, t // bt, n_heads),
        in_specs=[pl.BlockSpec((1, bt, hd), lambda b, i, h: (b, i, h)),
                  pl.BlockSpec((1, n_mem, hd), lambda b, i, h: (b, 0, h)),
                  pl.BlockSpec((1, n_mem, hd), lambda b, i, h: (b, 0, h))],
        out_specs=pl.BlockSpec((1, bt, hd), lambda b, i, h: (b, i, h)),
        out_shape=jax.ShapeDtypeStruct((bsz, t, dm), BF16),
        compiler_params=_params("parallel", "parallel", "parallel"),
        name="cross_attention",
    )(q3, mk3, mv3)


def _layer(x, bsz, t, wts, lb_logits, layer, s0, buf, mk3, mv3, n_xa_heads):
    m, dm = x.shape
    hw = wts["hg_norm"].shape[0]
    cw = wts["conv_w"].shape[1]
    q_col, v_col, og_col = 0, hw, 2 * hw
    ch_col, cb_col, cc_col = 3 * hw, 3 * hw + cw, 3 * hw + 2 * cw
    ga_col, gb_col = 3 * hw + 3 * cw, 3 * hw + 3 * cw + dm

    h = _rmsnorm(x, wts["norm_mix"], BF16)
    pw = wts["w_in"].shape[1]
    proj = _matmul(h, wts["w_in"], BF16, cols=[(0, hw), (2 * hw, pw - 2 * hw)])
    fpre = _matmul(h, wts["w_in"], F32, cols=[(hw, hw)])
    proj3 = proj.reshape(bsz, t, proj.shape[1])
    a_in, s_fin = _hgrn(proj3, fpre.reshape(bsz, t, hw), lb_logits, wts["hg_norm"], s0, layer,
                        q_col, v_col, og_col)
    bz, new_buf = _conv(proj3, wts["conv_w"], buf, ch_col, cb_col, cc_col)
    merged = _matmul_gated(a_in.reshape(m, hw), bz.reshape(m, cw), wts["w_a"], wts["w_b"],
                           proj, ga_col, gb_col)
    x = _matmul(merged, wts["w_o"], F32, res=x)

    hq = _rmsnorm(x, wts["norm_xattn"], BF16)
    qx = _matmul(hq, wts["w_xq"], BF16)
    att = _attention(qx.reshape(bsz, t, dm), mk3, mv3, n_xa_heads)
    x = _matmul(att.reshape(m, dm), wts["w_xo"], F32, res=x)

    hf = _rmsnorm(x, wts["norm_ffn"], BF16)
    act = _matmul_swiglu(hf, wts["w_gate"], wts["w_up"])
    x = _matmul(act, wts["w_down"], F32, bm_pref=512, bn_pref=512, res=x)
    return x, s_fin, new_buf


def kernel(x_prompt, x_sample, cache_mem_k, cache_mem_v, state_hgrn, state_conv, mem_prompt, norm_mix, w_in, lb_logits, hg_norm, conv_w, w_a, w_b, w_o, norm_xattn, norm_mem, w_xq, w_xk, w_xv, w_xo, norm_ffn, w_gate, w_up, w_down, norm_final):
    depth = norm_mix.shape[0]
    bp, tp, dm = x_prompt.shape
    bs, ts, _ = x_sample.shape
    hw = hg_norm.shape[1]
    n_mem = mem_prompt.shape[1]
    n_xa_heads = cache_mem_k.shape[3]
    nh, dk, dv = state_hgrn.shape[2:]
    assert dk == HEAD_DIM_HGRN and dv == HEAD_DIM_HGRN and nh * dk == hw

    xp = x_prompt.reshape(bp * tp, dm)
    xs = x_sample.reshape(bs * ts, dm)
    outs = {k: [] for k in ("mk", "mv", "sp", "cp", "ss", "cs")}
    for l in range(depth):
        w_in_l = w_in[l]
        wts = {
            "norm_mix": norm_mix[l], "hg_norm": hg_norm[l], "conv_w": conv_w[l],
            "norm_xattn": norm_xattn[l], "norm_ffn": norm_ffn[l],
            "w_in": w_in_l.astype(BF16),
            "w_a": w_a[l].astype(BF16), "w_b": w_b[l].astype(BF16), "w_o": w_o[l].astype(BF16),
            "w_xq": w_xq[l].astype(BF16), "w_xo": w_xo[l].astype(BF16),
            "w_gate": w_gate[l].astype(BF16), "w_up": w_up[l].astype(BF16),
            "w_down": w_down[l].astype(BF16),
        }
        mem_n = _rmsnorm(mem_prompt.reshape(bp * n_mem, dm), norm_mem[l], BF16)
        mk_p = _matmul(mem_n, w_xk[l].astype(BF16), F32)
        mv_p = _matmul(mem_n, w_xv[l].astype(BF16), F32)
        s0 = jnp.zeros((bp, nh, dk, dv), F32)
        buf0 = jnp.zeros((bp, conv_w.shape[1] - 1, conv_w.shape[2]), F32)
        xp, s_p, buf_p = _layer(xp, bp, tp, wts, lb_logits, l, s0, buf0,
                                mk_p.astype(BF16).reshape(bp, n_mem, dm),
                                mv_p.astype(BF16).reshape(bp, n_mem, dm), n_xa_heads)
        xs, s_s, buf_s = _layer(xs, bs, ts, wts, lb_logits, l, state_hgrn[l], state_conv[l],
                                cache_mem_k[l].astype(BF16).reshape(bs, n_mem, dm),
                                cache_mem_v[l].astype(BF16).reshape(bs, n_mem, dm), n_xa_heads)
        outs["mk"].append(mk_p.reshape(bp, n_mem, n_xa_heads, dm // n_xa_heads))
        outs["mv"].append(mv_p.reshape(bp, n_mem, n_xa_heads, dm // n_xa_heads))
        outs["sp"].append(s_p)
        outs["cp"].append(buf_p)
        outs["ss"].append(s_s)
        outs["cs"].append(buf_s)
    y_prompt = _rmsnorm(xp, norm_final, F32).reshape(bp, tp, dm)
    y_sample = _rmsnorm(xs, norm_final, F32).reshape(bs, ts, dm)
    return (y_prompt, y_sample, jnp.stack(outs["mk"]), jnp.stack(outs["mv"]), jnp.stack(outs["sp"]),
            jnp.stack(outs["cp"]), jnp.stack(outs["ss"]), jnp.stack(outs["cs"]))
```

```python
import functools

import numpy as np
import jax
import jax.numpy as jnp
from jax import lax
from jax.experimental import pallas as pl
from jax.experimental.pallas import tpu as pltpu

EPS = 1e-6
LANES = 128
SUBLANES = 8
HEAD_DIM_HGRN = 128
CHUNK = 64
VMEM_LIMIT_BYTES = 56 * 1024 * 1024
BF16 = jnp.bfloat16
F32 = jnp.float32


def _params(*semantics):
    return pltpu.CompilerParams(dimension_semantics=semantics, vmem_limit_bytes=VMEM_LIMIT_BYTES)


def _tile(n, pref, mult):
    if n <= pref:
        return n
    t = (pref // mult) * mult
    while t >= mult:
        if n % t == 0:
            return t
        t -= mult
    raise ValueError(f"no tile for {n} (pref {pref}, mult {mult})")


def _rmsnorm_kernel(x_ref, g_ref, o_ref):
    x = x_ref[...].astype(F32)
    y = x * lax.rsqrt(jnp.mean(x * x, axis=-1, keepdims=True) + EPS)
    o_ref[...] = (y * g_ref[...]).astype(o_ref.dtype)


def _rmsnorm(x, g, out_dtype):
    m, d = x.shape
    bm = _tile(m, 512, 8)
    return pl.pallas_call(
        _rmsnorm_kernel,
        grid=(m // bm,),
        in_specs=[pl.BlockSpec((bm, d), lambda i: (i, 0)),
                  pl.BlockSpec((1, d), lambda i: (0, 0))],
        out_specs=pl.BlockSpec((bm, d), lambda i: (i, 0)),
        out_shape=jax.ShapeDtypeStruct((m, d), out_dtype),
        compiler_params=_params("parallel"),
        name="rmsnorm",
    )(x, g.reshape(1, d).astype(F32))


def _mm_kernel(x_ref, w_ref, o_ref):
    o_ref[...] = jnp.dot(x_ref[...], w_ref[...], preferred_element_type=F32).astype(o_ref.dtype)


def _mm_res_kernel(x_ref, w_ref, r_ref, o_ref):
    o_ref[...] = r_ref[...] + jnp.dot(x_ref[...], w_ref[...], preferred_element_type=F32)


def _mm_gated_kernel(a_ref, b_ref, wa_ref, wb_ref, ga_ref, gb_ref, o_ref):
    a = jnp.dot(a_ref[...], wa_ref[...], preferred_element_type=F32)
    b = jnp.dot(b_ref[...], wb_ref[...], preferred_element_type=F32)
    ga = jax.nn.sigmoid(ga_ref[...].astype(F32))
    gb = jax.nn.sigmoid(gb_ref[...].astype(F32))
    o_ref[...] = (ga * a + gb * b).astype(o_ref.dtype)


def _mm_swiglu_kernel(x_ref, wg_ref, wu_ref, o_ref):
    x = x_ref[...]
    g = jnp.dot(x, wg_ref[...], preferred_element_type=F32)
    u = jnp.dot(x, wu_ref[...], preferred_element_type=F32)
    o_ref[...] = (g * jax.nn.sigmoid(g) * u).astype(o_ref.dtype)


def _matmul(x, w, out_dtype, *, bm_pref=1024, bn_pref=1024, res=None, cols=None):
    m, k = x.shape
    cols = cols or [(0, w.shape[1])]
    n = sum(width for _, width in cols)
    bm = _tile(m, bm_pref, 8)
    bn = _tile(int(functools.reduce(np.gcd, [v for seg in cols for v in seg if v])), bn_pref, LANES)
    steps, out_blk = [], 0
    for start, width in cols:
        steps.append((out_blk, start // bn - out_blk))
        out_blk += width // bn

    def w_block(j):
        blk = j + steps[0][1]
        for (first, shift), (_, prev) in zip(steps[1:], steps[:-1]):
            blk = blk + jnp.where(j >= first, shift - prev, 0)
        return blk

    in_specs = [pl.BlockSpec((bm, k), lambda i, j: (i, 0)),
                pl.BlockSpec((k, bn), lambda i, j: (0, w_block(j)))]
    args = [x, w]
    body = _mm_kernel
    if res is not None:
        in_specs.append(pl.BlockSpec((bm, bn), lambda i, j: (i, j)))
        args.append(res)
        body = _mm_res_kernel
    return pl.pallas_call(
        body,
        grid=(m // bm, n // bn),
        in_specs=in_specs,
        out_specs=pl.BlockSpec((bm, bn), lambda i, j: (i, j)),
        out_shape=jax.ShapeDtypeStruct((m, n), out_dtype),
        compiler_params=_params("parallel", "parallel"),
        name="matmul_res" if res is not None else "matmul",
    )(*args)


def _matmul_gated(a, b, wa, wb, proj, ga_col, gb_col):
    m, ka = a.shape
    kb = b.shape[1]
    n = wa.shape[1]
    bm = _tile(m, 1024, 8)
    bn = _tile(n, 1024, LANES)
    assert ga_col % bn == 0 and gb_col % bn == 0
    ga_blk, gb_blk = ga_col // bn, gb_col // bn
    return pl.pallas_call(
        _mm_gated_kernel,
        grid=(m // bm, n // bn),
        in_specs=[pl.BlockSpec((bm, ka), lambda i, j: (i, 0)),
                  pl.BlockSpec((bm, kb), lambda i, j: (i, 0)),
                  pl.BlockSpec((ka, bn), lambda i, j: (0, j)),
                  pl.BlockSpec((kb, bn), lambda i, j: (0, j)),
                  pl.BlockSpec((bm, bn), lambda i, j: (i, ga_blk + j)),
                  pl.BlockSpec((bm, bn), lambda i, j: (i, gb_blk + j))],
        out_specs=pl.BlockSpec((bm, bn), lambda i, j: (i, j)),
        out_shape=jax.ShapeDtypeStruct((m, n), BF16),
        compiler_params=_params("parallel", "parallel"),
        name="matmul_gated",
    )(a, b, wa, wb, proj, proj)


def _matmul_swiglu(x, wg, wu):
    m, k = x.shape
    n = wg.shape[1]
    bm = _tile(m, 2048, 8)
    bn = _tile(n, 256, LANES)
    return pl.pallas_call(
        _mm_swiglu_kernel,
        grid=(m // bm, n // bn),
        in_specs=[pl.BlockSpec((bm, k), lambda i, j: (i, 0)),
                  pl.BlockSpec((k, bn), lambda i, j: (0, j)),
                  pl.BlockSpec((k, bn), lambda i, j: (0, j))],
        out_specs=pl.BlockSpec((bm, bn), lambda i, j: (i, j)),
        out_shape=jax.ShapeDtypeStruct((m, n), BF16),
        compiler_params=_params("parallel", "parallel"),
        name="matmul_swiglu",
    )(x, wg, wu)


def _hgrn_tables(c, rows, pb):
    t = np.arange(rows)[:, None]
    s = np.arange(rows)[None, :]
    tri = ((t // c) == (s // c)) & (s <= t)
    t, s = t[:pb, :pb], s[:pb, :pb]
    masks = []
    l = 1
    while l < c:
        masks.append(((t // l) % 2 == 1) & ((s // l) == (t // l) - 1))
        l *= 2
    masks.append(t == s)
    return tri.astype(np.float32), np.stack(masks, 0).astype(np.float32)


def _hgrn_kernel(tri_ref, mask_ref, lbl_ref, q_ref, f_ref, v_ref, og_ref, gn_ref, s0_ref,
                 o_ref, sfin_ref, st_ref, oi_ref, qin_ref, u_ref, dec_ref,
                 *, c, n_chunks, rows, pb, layer, carry):
    t_idx = pl.program_id(2)
    d = HEAD_DIM_HGRN
    n_lv = c.bit_length() - 1
    g_chunks = rows // c
    n_pb = rows // pb
    nt = (((1,), (1,)), ((), ()))
    tn = (((0,), (0,)), ((), ()))

    if carry:
        @pl.when(t_idx == 0)
        def _():
            st_ref[...] = s0_ref[0, 0].T

    logits = lbl_ref[...].astype(F32)
    ex = jnp.exp(logits - jnp.max(logits, axis=0, keepdims=True))
    lb = jnp.sum(ex[:layer + 1], axis=0, keepdims=True) / jnp.sum(ex, axis=0, keepdims=True)
    gain = gn_ref[...].astype(F32)
    shape3 = (rows // SUBLANES, SUBLANES, d)
    sub = lax.broadcasted_iota(jnp.int32, shape3, 1)

    def block_rows(x, first, step, length):
        pieces = [jnp.broadcast_to(x[first + step * p:first + step * p + 1, :], (length, d))
                  for p in range(rows // length)]
        return pieces[0] if len(pieces) == 1 else jnp.concatenate(pieces, axis=0)

    def pair_ref(b, l):
        if l >= SUBLANES:
            return block_rows(b, l - 1, 2 * l, 2 * l)
        b3 = b.reshape(shape3)
        pick = lambda i: jnp.broadcast_to(b3[:, i:i + 1, :], shape3)
        r3 = pick(3) if l == 4 else jnp.where(sub < 4, pick(1), pick(5))
        return r3.reshape(rows, d)

    def odd_half(x_odd, x_even, l):
        if l >= SUBLANES:
            pieces = [(x_odd if p % 2 else x_even)[p * l:(p + 1) * l] for p in range(rows // l)]
            return jnp.concatenate(pieces, axis=0)
        return jnp.where((sub & l) != 0, x_odd.reshape(shape3), x_even.reshape(shape3)).reshape(rows, d)

    def phase_a(bi, carry):
        rsel = pl.ds(pl.multiple_of(bi * rows, rows), rows)
        f = lb + (1.0 - lb) * jax.nn.sigmoid(f_ref[0, rsel, :])
        g = jnp.log(f)
        k = 1.0 - f
        qb = q_ref[0, rsel, :]
        q = qb.astype(F32)
        v = v_ref[0, rsel, :]

        g_hi = g.astype(BF16)
        r1 = g - g_hi.astype(F32)
        g_mid = r1.astype(BF16)
        g_lo = (r1 - g_mid.astype(F32)).astype(BF16)
        b3 = jnp.dot(tri_ref[...], jnp.concatenate([g_hi, g_mid, g_lo], axis=1),
                     preferred_element_type=F32)
        b = b3[:, 0:d] + b3[:, d:2 * d] + b3[:, 2 * d:3 * d]

        def scores(zl, zr, li):
            p = lax.dot_general(zl, zr, nt, preferred_element_type=F32)
            return [mask_ref[li] * p[i * pb:(i + 1) * pb, i * pb:(i + 1) * pb] for i in range(n_pb)]

        a = scores(qb, k.astype(BF16), n_lv)
        z = odd_half(q * f, k, 1).astype(BF16)
        a = [x + y for x, y in zip(a, scores(z, z, 0))]
        for li in range(1, n_lv):
            l = 1 << li
            w = jnp.exp(-jnp.abs(b - pair_ref(b, l)))
            z = (odd_half(q, k, l) * w).astype(BF16)
            a = [x + y for x, y in zip(a, scores(z, z, li))]
        for i in range(n_pb):
            oi_ref[pl.ds(pl.multiple_of(bi * rows + i * pb, pb), pb), :] = jnp.dot(
                a[i].astype(BF16), v[i * pb:(i + 1) * pb], preferred_element_type=F32)

        dec = jnp.exp(b)
        qin_ref[rsel, :] = (q * dec).astype(BF16)
        k_out = (k * jnp.exp(block_rows(b, c - 1, c, c) - b)).astype(BF16)
        for ci in range(g_chunks):
            cr = slice(ci * c, (ci + 1) * c)
            u_ref[bi * g_chunks + ci] = lax.dot_general(v[cr], k_out[cr], tn, preferred_element_type=F32)
            dec_ref[pl.ds(pl.multiple_of((bi * g_chunks + ci) * SUBLANES, SUBLANES), SUBLANES), :] = (
                jnp.broadcast_to(dec[ci * c + c - 1:ci * c + c, :], (SUBLANES, d)))
        return carry

    def phase_bc(bi):
        st = st_ref[...] if carry else None
        outs = []
        for ci in range(g_chunks):
            chunk = bi * g_chunks + ci
            rsel = pl.ds(pl.multiple_of(chunk * c, c), c)
            if not carry:
                st = s0_ref[chunk, 0].T
            outs.append(oi_ref[rsel, :] + lax.dot_general(
                qin_ref[rsel, :], st.astype(BF16), nt, preferred_element_type=F32))
            st = st * dec_ref[pl.ds(pl.multiple_of(chunk * SUBLANES, SUBLANES), 1), :] + u_ref[chunk]
            if not carry:
                sfin_ref[chunk, 0] = st.T
        if carry:
            st_ref[...] = st
        rsel = pl.ds(pl.multiple_of(bi * rows, rows), rows)
        o = outs[0] if g_chunks == 1 else jnp.concatenate(outs, axis=0)
        y = o * lax.rsqrt(jnp.mean(o * o, axis=-1, keepdims=True) + EPS) * gain
        og = og_ref[0, rsel, :].astype(F32)
        o_ref[0, rsel, :] = (y * (og * jax.nn.sigmoid(og))).astype(o_ref.dtype)

    n_batches = n_chunks // g_chunks
    for bi in range(n_batches):
        phase_a(bi, 0)
        if bi > 0:
            phase_bc(bi - 1)
    phase_bc(n_batches - 1)

    if carry:
        @pl.when(t_idx == pl.num_programs(2) - 1)
        def _():
            sfin_ref[0, 0] = st_ref[...].T


def _hgrn(proj3, fpre3, lb_logits, gain, s0, layer, q_col, v_col, og_col):
    bsz, t, _ = proj3.shape
    hw = fpre3.shape[-1]
    d = HEAD_DIM_HGRN
    nh = hw // d
    c = CHUNK if t % CHUNK == 0 else t
    assert c & (c - 1) == 0 and c >= 2 * SUBLANES and t % c == 0
    carry = t > c
    if not carry:
        out, s_fin = _hgrn_call(proj3.reshape(1, bsz * t, -1), fpre3.reshape(1, bsz * t, hw), lb_logits,
                                gain, s0, layer, q_col, v_col, og_col, c=c, carry=False)
        return out.reshape(bsz, t, hw), s_fin
    return _hgrn_call(proj3, fpre3, lb_logits, gain, s0, layer, q_col, v_col, og_col, c=c, carry=True)


def _hgrn_call(proj3, fpre3, lb_logits, gain, s0, layer, q_col, v_col, og_col, *, c, carry):
    bsz, t, _ = proj3.shape
    hw = fpre3.shape[-1]
    d = HEAD_DIM_HGRN
    nh = hw // d
    tt = _tile(t, 2048, c)
    n_chunks = tt // c
    rows = _tile(tt, 2 * LANES, c)
    pb = min(rows, LANES)
    tri, masks = _hgrn_tables(c, rows, pb)
    qb, vb, ob = q_col // d, v_col // d, og_col // d
    kern = functools.partial(_hgrn_kernel, c=c, n_chunks=n_chunks, rows=rows, pb=pb, layer=layer,
                             carry=carry)
    nl = lb_logits.shape[0]
    if carry:
        state_spec = pl.BlockSpec((1, 1, d, d), lambda b, h, i: (b, h, 0, 0))
    else:
        state_spec = pl.BlockSpec((n_chunks, 1, d, d), lambda b, h, i: (i, h, 0, 0))
    return pl.pallas_call(
        kern,
        grid=(bsz, nh, t // tt),
        in_specs=[pl.BlockSpec((rows, rows), lambda b, h, i: (0, 0)),
                  pl.BlockSpec(masks.shape, lambda b, h, i: (0, 0, 0)),
                  pl.BlockSpec((nl, d), lambda b, h, i: (0, h)),
                  pl.BlockSpec((1, tt, d), lambda b, h, i: (b, i, qb + h)),
                  pl.BlockSpec((1, tt, d), lambda b, h, i: (b, i, h)),
                  pl.BlockSpec((1, tt, d), lambda b, h, i: (b, i, vb + h)),
                  pl.BlockSpec((1, tt, d), lambda b, h, i: (b, i, ob + h)),
                  pl.BlockSpec((1, d), lambda b, h, i: (0, h)),
                  state_spec],
        out_specs=[pl.BlockSpec((1, tt, d), lambda b, h, i: (b, i, h)),
                   state_spec],
        out_shape=[jax.ShapeDtypeStruct((bsz, t, hw), BF16),
                   jax.ShapeDtypeStruct(s0.shape, F32)],
        scratch_shapes=[pltpu.VMEM((d, d), F32),
                        pltpu.VMEM((tt, d), F32),
                        pltpu.VMEM((tt, d), BF16),
                        pltpu.VMEM((n_chunks, d, d), F32),
                        pltpu.VMEM((n_chunks * SUBLANES, d), F32)],
        compiler_params=_params("parallel", "parallel", "arbitrary"),
        name="hgrn2",
    )(jnp.asarray(tri, BF16), jnp.asarray(masks, F32), lb_logits.astype(F32),
      proj3, fpre3, proj3, proj3, gain.reshape(1, hw).astype(F32), s0.astype(F32))


CONV_PAD = 8
CONV_BLOCK_ELEMS = 1024 * 1024


def _conv_kernel(cc_ref, ch_ref, cb_ref, w_ref, buf_ref, o_ref, nbuf_ref, u_ref, *, t, k, rb):
    u_ref[CONV_PAD - (k - 1):CONV_PAD, :] = buf_ref[0].astype(F32)
    for r in range(0, t, rb):
        u_ref[CONV_PAD + r:CONV_PAD + r + rb, :] = (
            cc_ref[0, r:r + rb, :].astype(F32) * ch_ref[0, r:r + rb, :].astype(F32))
    w = w_ref[...].astype(F32)
    for r in range(0, t, rb):
        z = w[0:1, :] * u_ref[CONV_PAD - (k - 1) + r:CONV_PAD - (k - 1) + r + rb, :]
        for j in range(1, k):
            s = CONV_PAD - (k - 1) + j + r
            z = z + w[j:j + 1, :] * u_ref[s:s + rb, :]
        o_ref[0, r:r + rb, :] = (cb_ref[0, r:r + rb, :].astype(F32) * z).astype(o_ref.dtype)
    nbuf_ref[0] = u_ref[CONV_PAD + t - (k - 1):CONV_PAD + t, :]


def _conv(proj3, conv_w, buf, ch_col, cb_col, cc_col):
    bsz, t, _ = proj3.shape
    k, cw = conv_w.shape
    assert k - 1 <= CONV_PAD
    d = _tile(cw, max(LANES, CONV_BLOCK_ELEMS // t), LANES)
    rb = _tile(t, max(8, 512 * LANES // d), 8)
    kern = functools.partial(_conv_kernel, t=t, k=k, rb=rb)
    return pl.pallas_call(
        kern,
        grid=(bsz, cw // d),
        in_specs=[pl.BlockSpec((1, t, d), lambda b, j: (b, 0, cc_col // d + j)),
                  pl.BlockSpec((1, t, d), lambda b, j: (b, 0, ch_col // d + j)),
                  pl.BlockSpec((1, t, d), lambda b, j: (b, 0, cb_col // d + j)),
                  pl.BlockSpec((k, d), lambda b, j: (0, j)),
                  pl.BlockSpec((1, k - 1, d), lambda b, j: (b, 0, j))],
        out_specs=[pl.BlockSpec((1, t, d), lambda b, j: (b, 0, j)),
                   pl.BlockSpec((1, k - 1, d), lambda b, j: (b, 0, j))],
        out_shape=[jax.ShapeDtypeStruct((bsz, t, cw), BF16),
                   jax.ShapeDtypeStruct((bsz, k - 1, cw), F32)],
        scratch_shapes=[pltpu.VMEM((CONV_PAD + t, d), F32)],
        compiler_params=_params("parallel", "parallel"),
        name="short_conv",
    )(proj3, proj3, proj3, conv_w.astype(F32), buf.astype(F32))


def _attn_kernel(q_ref, k_ref, v_ref, o_ref, *, scale):
    nt = (((1,), (1,)), ((), ()))
    s = lax.dot_general(q_ref[0], k_ref[0], nt, preferred_element_type=F32) * scale
    s = s - jnp.max(s, axis=-1, keepdims=True)
    p = jnp.exp(s)
    p = p / jnp.sum(p, axis=-1, keepdims=True)
    o_ref[0] = jnp.dot(p.astype(BF16), v_ref[0], preferred_element_type=F32).astype(o_ref.dtype)


def _attention(q3, mk3, mv3, n_heads):
    bsz, t, dm = q3.shape
    n_mem = mk3.shape[1]
    hd = dm // n_heads
    bt = _tile(t, 512, 8)
    kern = functools.partial(_attn_kernel, scale=float(hd) ** -0.5)
    return pl.pallas_call(
        kern,
        grid=(bsz, t // bt, n_heads),
        in_specs=[pl.BlockSpec((1, bt, hd), lambda b, i, h: (b, i, h)),
                  pl.BlockSpec((1, n_mem, hd), lambda b, i, h: (b, 0, h)),
                  pl.BlockSpec((1, n_mem, hd), lambda b, i, h: (b, 0, h))],
        out_specs=pl.BlockSpec((1, bt, hd), lambda b, i, h: (b, i, h)),
        out_shape=jax.ShapeDtypeStruct((bsz, t, dm), BF16),
        compiler_params=_params("parallel", "parallel", "parallel"),
        name="cross_attention",
    )(q3, mk3, mv3)


def _layer(x, bsz, t, wts, lb_logits, layer, s0, buf, mk3, mv3, n_xa_heads):
    m, dm = x.shape
    hw = wts["hg_norm"].shape[0]
    cw = wts["conv_w"].shape[1]
    q_col, v_col, og_col = 0, hw, 2 * hw
    ch_col, cb_col, cc_col = 3 * hw, 3 * hw + cw, 3 * hw + 2 * cw
    ga_col, gb_col = 3 * hw + 3 * cw, 3 * hw + 3 * cw + dm

    h = _rmsnorm(x, wts["norm_mix"], BF16)
    pw = wts["w_in"].shape[1]
    proj = _matmul(h, wts["w_in"], BF16, cols=[(0, hw), (2 * hw, pw - 2 * hw)])
    fpre = _matmul(h, wts["w_in"], F32, cols=[(hw, hw)])
    proj3 = proj.reshape(bsz, t, proj.shape[1])
    a_in, s_fin = _hgrn(proj3, fpre.reshape(bsz, t, hw), lb_logits, wts["hg_norm"], s0, layer,
                        q_col, v_col, og_col)
    bz, new_buf = _conv(proj3, wts["conv_w"], buf, ch_col, cb_col, cc_col)
    merged = _matmul_gated(a_in.reshape(m, hw), bz.reshape(m, cw), wts["w_a"], wts["w_b"],
                           proj, ga_col, gb_col)
    x = _matmul(merged, wts["w_o"], F32, res=x)

    hq = _rmsnorm(x, wts["norm_xattn"], BF16)
    qx = _matmul(hq, wts["w_xq"], BF16)
    att = _attention(qx.reshape(bsz, t, dm), mk3, mv3, n_xa_heads)
    x = _matmul(att.reshape(m, dm), wts["w_xo"], F32, res=x)

    hf = _rmsnorm(x, wts["norm_ffn"], BF16)
    act = _matmul_swiglu(hf, wts["w_gate"], wts["w_up"])
    x = _matmul(act, wts["w_down"], F32, bm_pref=512, bn_pref=512, res=x)
    return x, s_fin, new_buf


def kernel(x_prompt, x_sample, cache_mem_k, cache_mem_v, state_hgrn, state_conv, mem_prompt, norm_mix, w_in, lb_logits, hg_norm, conv_w, w_a, w_b, w_o, norm_xattn, norm_mem, w_xq, w_xk, w_xv, w_xo, norm_ffn, w_gate, w_up, w_down, norm_final):
    depth = norm_mix.shape[0]
    bp, tp, dm = x_prompt.shape
    bs, ts, _ = x_sample.shape
    hw = hg_norm.shape[1]
    n_mem = mem_prompt.shape[1]
    n_xa_heads = cache_mem_k.shape[3]
    nh, dk, dv = state_hgrn.shape[2:]
    assert dk == HEAD_DIM_HGRN and dv == HEAD_DIM_HGRN and nh * dk == hw

    xp = x_prompt.reshape(bp * tp, dm)
    xs = x_sample.reshape(bs * ts, dm)
    outs = {k: [] for k in ("mk", "mv", "sp", "cp", "ss", "cs")}
    for l in range(depth):
        w_in_l = w_in[l]
        wts = {
            "norm_mix": norm_mix[l], "hg_norm": hg_norm[l], "conv_w": conv_w[l],
            "norm_xattn": norm_xattn[l], "norm_ffn": norm_ffn[l],
            "w_in": w_in_l.astype(BF16),
            "w_a": w_a[l].astype(BF16), "w_b": w_b[l].astype(BF16), "w_o": w_o[l].astype(BF16),
            "w_xq": w_xq[l].astype(BF16), "w_xo": w_xo[l].astype(BF16),
            "w_gate": w_gate[l].astype(BF16), "w_up": w_up[l].astype(BF16),
            "w_down": w_down[l].astype(BF16),
        }
        mem_n = _rmsnorm(mem_prompt.reshape(bp * n_mem, dm), norm_mem[l], BF16)
        mk_p = _matmul(mem_n, w_xk[l].astype(BF16), F32)
        mv_p = _matmul(mem_n, w_xv[l].astype(BF16), F32)
        s0 = jnp.zeros((bp, nh, dk, dv), F32)
        buf0 = jnp.zeros((bp, conv_w.shape[1] - 1, conv_w.shape[2]), F32)
        xp, s_p, buf_p = _layer(xp, bp, tp, wts, lb_logits, l, s0, buf0,
                                mk_p.astype(BF16).reshape(bp, n_mem, dm),
                                mv_p.astype(BF16).reshape(bp, n_mem, dm), n_xa_heads)
        xs, s_s, buf_s = _layer(xs, bs, ts, wts, lb_logits, l, state_hgrn[l], state_conv[l],
                                cache_mem_k[l].astype(BF16).reshape(bs, n_mem, dm),
                                cache_mem_v[l].astype(BF16).reshape(bs, n_mem, dm), n_xa_heads)
        outs["mk"].append(mk_p.reshape(bp, n_mem, n_xa_heads, dm // n_xa_heads))
        outs["mv"].append(mv_p.reshape(bp, n_mem, n_xa_heads, dm // n_xa_heads))
        outs["sp"].append(s_p)
        outs["cp"].append(buf_p)
        outs["ss"].append(s_s)
        outs["cs"].append(buf_s)
    y_prompt = _rmsnorm(xp, norm_final, F32).reshape(bp, tp, dm)
    y_sample = _rmsnorm(xs, norm_final, F32).reshape(bs, ts, dm)
    return (y_prompt, y_sample, jnp.stack(outs["mk"]), jnp.stack(outs["mv"]), jnp.stack(outs["sp"]),
            jnp.stack(outs["cp"]), jnp.stack(outs["ss"]), jnp.stack(outs["cs"]))
```

```python
import functools

import numpy as np
import jax
import jax.numpy as jnp
from jax import lax
from jax.experimental import pallas as pl
from jax.experimental.pallas import tpu as pltpu

EPS = 1e-6
LANES = 128
SUBLANES = 8
HEAD_DIM_HGRN = 128
CHUNK = 64
VMEM_LIMIT_BYTES = 56 * 1024 * 1024
BF16 = jnp.bfloat16
F32 = jnp.float32


def _params(*semantics):
    return pltpu.CompilerParams(dimension_semantics=semantics, vmem_limit_bytes=VMEM_LIMIT_BYTES)


def _tile(n, pref, mult):
    if n <= pref:
        return n
    t = (pref // mult) * mult
    while t >= mult:
        if n % t == 0:
            return t
        t -= mult
    raise ValueError(f"no tile for {n} (pref {pref}, mult {mult})")


def _rmsnorm_kernel(x_ref, g_ref, o_ref):
    x = x_ref[...].astype(F32)
    y = x * lax.rsqrt(jnp.mean(x * x, axis=-1, keepdims=True) + EPS)
    o_ref[...] = (y * g_ref[...]).astype(o_ref.dtype)


def _rmsnorm(x, g, out_dtype):
    m, d = x.shape
    bm = _tile(m, 512, 8)
    return pl.pallas_call(
        _rmsnorm_kernel,
        grid=(m // bm,),
        in_specs=[pl.BlockSpec((bm, d), lambda i: (i, 0)),
                  pl.BlockSpec((1, d), lambda i: (0, 0))],
        out_specs=pl.BlockSpec((bm, d), lambda i: (i, 0)),
        out_shape=jax.ShapeDtypeStruct((m, d), out_dtype),
        compiler_params=_params("parallel"),
        name="rmsnorm",
    )(x, g.reshape(1, d).astype(F32))


def _mm_kernel(*refs, has_res, n_side):
    x_ref, w_ref = refs[0], refs[1]
    r_ref = refs[2] if has_res else None
    src = refs[2 + has_res:2 + has_res + n_side]
    o_ref = refs[2 + has_res + n_side]
    dst = refs[3 + has_res + n_side:]
    acc = jnp.dot(x_ref[...], w_ref[...].astype(BF16), preferred_element_type=F32)
    if has_res:
        acc = r_ref[...] + acc
    o_ref[...] = acc.astype(o_ref.dtype)
    for src_ref, dst_ref in zip(src, dst):
        dst_ref[...] = src_ref[...].astype(BF16)


def _mm_gated_kernel(a_ref, b_ref, wa_ref, wb_ref, ga_ref, gb_ref, o_ref):
    a = jnp.dot(a_ref[...], wa_ref[...], preferred_element_type=F32)
    b = jnp.dot(b_ref[...], wb_ref[...], preferred_element_type=F32)
    ga = jax.nn.sigmoid(ga_ref[...].astype(F32))
    gb = jax.nn.sigmoid(gb_ref[...].astype(F32))
    o_ref[...] = (ga * a + gb * b).astype(o_ref.dtype)


def _mm_swiglu_kernel(x_ref, wg_ref, wu_ref, o_ref):
    x = x_ref[...]
    g = jnp.dot(x, wg_ref[...], preferred_element_type=F32)
    u = jnp.dot(x, wu_ref[...], preferred_element_type=F32)
    o_ref[...] = (g * jax.nn.sigmoid(g) * u).astype(o_ref.dtype)


BF16_SUBLANES = 16


def _matmul(x, w, out_dtype, *, bm_pref=1024, bn_pref=1024, res=None, cols=None, side=()):
    m, k = x.shape
    cols = cols or [(0, w.shape[1])]
    n = sum(width for _, width in cols)
    bm = _tile(m, bm_pref, 8)
    bn = _tile(int(functools.reduce(np.gcd, [v for seg in cols for v in seg if v])), bn_pref, LANES)
    steps, out_blk = [], 0
    for start, width in cols:
        steps.append((out_blk, start // bn - out_blk))
        out_blk += width // bn

    def w_block(j):
        blk = j + steps[0][1]
        for (first, shift), (_, prev) in zip(steps[1:], steps[:-1]):
            blk = blk + jnp.where(j >= first, shift - prev, 0)
        return blk

    ni, nj = m // bm, n // bn
    in_specs = [pl.BlockSpec((bm, k), lambda i, j: (i, 0)),
                pl.BlockSpec((k, bn), lambda i, j: (0, w_block(j)))]
    args = [x, w]
    if res is not None:
        in_specs.append(pl.BlockSpec((bm, bn), lambda i, j: (i, j)))
        args.append(res)
    out_specs = [pl.BlockSpec((bm, bn), lambda i, j: (i, j))]
    out_shape = [jax.ShapeDtypeStruct((m, n), out_dtype)]
    for sw in side:
        rows, width = sw.shape
        n_blocks = max(nb for nb in range(1, ni * nj + 1)
                       if rows % nb == 0 and (rows // nb) % BF16_SUBLANES == 0)
        spec = pl.BlockSpec((rows // n_blocks, width),
                            lambda i, j, n_blocks=n_blocks: (jnp.minimum(i * nj + j, n_blocks - 1), 0))
        in_specs.append(spec)
        args.append(sw)
        out_specs.append(spec)
        out_shape.append(jax.ShapeDtypeStruct((rows, width), BF16))
    outs = pl.pallas_call(
        functools.partial(_mm_kernel, has_res=res is not None, n_side=len(side)),
        grid=(ni, nj),
        in_specs=in_specs,
        out_specs=out_specs,
        out_shape=out_shape,
        compiler_params=_params("arbitrary", "arbitrary") if side else _params("parallel", "parallel"),
        name="matmul_res" if res is not None else "matmul",
    )(*args)
    return (outs[0], list(outs[1:])) if side else outs[0]


def _matmul_gated(a, b, wa, wb, proj, ga_col, gb_col):
    m, ka = a.shape
    kb = b.shape[1]
    n = wa.shape[1]
    bm = _tile(m, 1024, 8)
    bn = _tile(n, 1024, LANES)
    assert ga_col % bn == 0 and gb_col % bn == 0
    ga_blk, gb_blk = ga_col // bn, gb_col // bn
    return pl.pallas_call(
        _mm_gated_kernel,
        grid=(m // bm, n // bn),
        in_specs=[pl.BlockSpec((bm, ka), lambda i, j: (i, 0)),
                  pl.BlockSpec((bm, kb), lambda i, j: (i, 0)),
                  pl.BlockSpec((ka, bn), lambda i, j: (0, j)),
                  pl.BlockSpec((kb, bn), lambda i, j: (0, j)),
                  pl.BlockSpec((bm, bn), lambda i, j: (i, ga_blk + j)),
                  pl.BlockSpec((bm, bn), lambda i, j: (i, gb_blk + j))],
        out_specs=pl.BlockSpec((bm, bn), lambda i, j: (i, j)),
        out_shape=jax.ShapeDtypeStruct((m, n), BF16),
        compiler_params=_params("parallel", "parallel"),
        name="matmul_gated",
    )(a, b, wa, wb, proj, proj)


def _matmul_swiglu(x, wg, wu):
    m, k = x.shape
    n = wg.shape[1]
    bm = _tile(m, 2048, 8)
    bn = _tile(n, 256, LANES)
    return pl.pallas_call(
        _mm_swiglu_kernel,
        grid=(m // bm, n // bn),
        in_specs=[pl.BlockSpec((bm, k), lambda i, j: (i, 0)),
                  pl.BlockSpec((k, bn), lambda i, j: (0, j)),
                  pl.BlockSpec((k, bn), lambda i, j: (0, j))],
        out_specs=pl.BlockSpec((bm, bn), lambda i, j: (i, j)),
        out_shape=jax.ShapeDtypeStruct((m, n), BF16),
        compiler_params=_params("parallel", "parallel"),
        name="matmul_swiglu",
    )(x, wg, wu)


def _hgrn_tables(c, rows, pb):
    t = np.arange(rows)[:, None]
    s = np.arange(rows)[None, :]
    tri = ((t // c) == (s // c)) & (s <= t)
    t, s = t[:pb, :pb], s[:pb, :pb]
    masks = []
    l = 1
    while l < c:
        masks.append(((t // l) % 2 == 1) & ((s // l) == (t // l) - 1))
        l *= 2
    masks.append(t == s)
    return tri.astype(np.float32), np.stack(masks, 0).astype(np.float32)


def _hgrn_kernel(tri_ref, mask_ref, lbl_ref, q_ref, f_ref, v_ref, og_ref, gn_ref, s0_ref,
                 o_ref, sfin_ref, st_ref, oi_ref, qin_ref, u_ref, dec_ref,
                 *, c, n_chunks, rows, pb, layer, carry):
    t_idx = pl.program_id(2)
    d = HEAD_DIM_HGRN
    n_lv = c.bit_length() - 1
    g_chunks = rows // c
    n_pb = rows // pb
    nt = (((1,), (1,)), ((), ()))
    tn = (((0,), (0,)), ((), ()))

    if carry:
        @pl.when(t_idx == 0)
        def _():
            st_ref[...] = s0_ref[0, 0].T

    logits = lbl_ref[...].astype(F32)
    ex = jnp.exp(logits - jnp.max(logits, axis=0, keepdims=True))
    lb = jnp.sum(ex[:layer + 1], axis=0, keepdims=True) / jnp.sum(ex, axis=0, keepdims=True)
    gain = gn_ref[...].astype(F32)
    shape3 = (rows // SUBLANES, SUBLANES, d)
    sub = lax.broadcasted_iota(jnp.int32, shape3, 1)

    def block_rows(x, first, step, length):
        pieces = [jnp.broadcast_to(x[first + step * p:first + step * p + 1, :], (length, d))
                  for p in range(rows // length)]
        return pieces[0] if len(pieces) == 1 else jnp.concatenate(pieces, axis=0)

    def pair_ref(b, l):
        if l >= SUBLANES:
            return block_rows(b, l - 1, 2 * l, 2 * l)
        b3 = b.reshape(shape3)
        pick = lambda i: jnp.broadcast_to(b3[:, i:i + 1, :], shape3)
        r3 = pick(3) if l == 4 else jnp.where(sub < 4, pick(1), pick(5))
        return r3.reshape(rows, d)

    def odd_half(x_odd, x_even, l):
        if l >= SUBLANES:
            pieces = [(x_odd if p % 2 else x_even)[p * l:(p + 1) * l] for p in range(rows // l)]
            return jnp.concatenate(pieces, axis=0)
        return jnp.where((sub & l) != 0, x_odd.reshape(shape3), x_even.reshape(shape3)).reshape(rows, d)

    def phase_a(bi, carry):
        rsel = pl.ds(pl.multiple_of(bi * rows, rows), rows)
        f = lb + (1.0 - lb) * jax.nn.sigmoid(f_ref[0, rsel, :])
        g = jnp.log(f)
        k = 1.0 - f
        qb = q_ref[0, rsel, :]
        q = qb.astype(F32)
        v = v_ref[0, rsel, :]

        g_hi = g.astype(BF16)
        r1 = g - g_hi.astype(F32)
        g_mid = r1.astype(BF16)
        g_lo = (r1 - g_mid.astype(F32)).astype(BF16)
        b3 = jnp.dot(tri_ref[...], jnp.concatenate([g_hi, g_mid, g_lo], axis=1),
                     preferred_element_type=F32)
        b = b3[:, 0:d] + b3[:, d:2 * d] + b3[:, 2 * d:3 * d]

        def scores(zl, zr, li):
            p = lax.dot_general(zl, zr, nt, preferred_element_type=F32)
            return [mask_ref[li] * p[i * pb:(i + 1) * pb, i * pb:(i + 1) * pb] for i in range(n_pb)]

        a = scores(qb, k.astype(BF16), n_lv)
        z = odd_half(q * f, k, 1).astype(BF16)
        a = [x + y for x, y in zip(a, scores(z, z, 0))]
        for li in range(1, n_lv):
            l = 1 << li
            w = jnp.exp(-jnp.abs(b - pair_ref(b, l)))
            z = (odd_half(q, k, l) * w).astype(BF16)
            a = [x + y for x, y in zip(a, scores(z, z, li))]
        for i in range(n_pb):
            oi_ref[pl.ds(pl.multiple_of(bi * rows + i * pb, pb), pb), :] = jnp.dot(
                a[i].astype(BF16), v[i * pb:(i + 1) * pb], preferred_element_type=F32)

        dec = jnp.exp(b)
        qin_ref[rsel, :] = (q * dec).astype(BF16)
        k_out = (k * jnp.exp(block_rows(b, c - 1, c, c) - b)).astype(BF16)
        for ci in range(g_chunks):
            cr = slice(ci * c, (ci + 1) * c)
            u_ref[bi * g_chunks + ci] = lax.dot_general(v[cr], k_out[cr], tn, preferred_element_type=F32)
            dec_ref[pl.ds(pl.multiple_of((bi * g_chunks + ci) * SUBLANES, SUBLANES), SUBLANES), :] = (
                jnp.broadcast_to(dec[ci * c + c - 1:ci * c + c, :], (SUBLANES, d)))
        return carry

    def phase_bc(bi):
        st = st_ref[...] if carry else None
        outs = []
        for ci in range(g_chunks):
            chunk = bi * g_chunks + ci
            rsel = pl.ds(pl.multiple_of(chunk * c, c), c)
            if not carry:
                st = s0_ref[chunk, 0].T
            outs.append(oi_ref[rsel, :] + lax.dot_general(
                qin_ref[rsel, :], st.astype(BF16), nt, preferred_element_type=F32))
            st = st * dec_ref[pl.ds(pl.multiple_of(chunk * SUBLANES, SUBLANES), 1), :] + u_ref[chunk]
            if not carry:
                sfin_ref[chunk, 0] = st.T
        if carry:
            st_ref[...] = st
        rsel = pl.ds(pl.multiple_of(bi * rows, rows), rows)
        o = outs[0] if g_chunks == 1 else jnp.concatenate(outs, axis=0)
        y = o * lax.rsqrt(jnp.mean(o * o, axis=-1, keepdims=True) + EPS) * gain
        og = og_ref[0, rsel, :].astype(F32)
        o_ref[0, rsel, :] = (y * (og * jax.nn.sigmoid(og))).astype(o_ref.dtype)

    n_batches = n_chunks // g_chunks
    for bi in range(n_batches):
        phase_a(bi, 0)
        if bi > 0:
            phase_bc(bi - 1)
    phase_bc(n_batches - 1)

    if carry:
        @pl.when(t_idx == pl.num_programs(2) - 1)
        def _():
            sfin_ref[0, 0] = st_ref[...].T


def _hgrn(proj3, fpre3, lb_logits, gain, s0, layer, q_col, v_col, og_col):
    bsz, t, _ = proj3.shape
    hw = fpre3.shape[-1]
    d = HEAD_DIM_HGRN
    nh = hw // d
    c = CHUNK if t % CHUNK == 0 else t
    assert c & (c - 1) == 0 and c >= 2 * SUBLANES and t % c == 0
    carry = t > c
    if not carry:
        out, s_fin = _hgrn_call(proj3.reshape(1, bsz * t, -1), fpre3.reshape(1, bsz * t, hw), lb_logits,
                                gain, s0, layer, q_col, v_col, og_col, c=c, carry=False)
        return out.reshape(bsz, t, hw), s_fin
    return _hgrn_call(proj3, fpre3, lb_logits, gain, s0, layer, q_col, v_col, og_col, c=c, carry=True)


def _hgrn_call(proj3, fpre3, lb_logits, gain, s0, layer, q_col, v_col, og_col, *, c, carry):
    bsz, t, _ = proj3.shape
    hw = fpre3.shape[-1]
    d = HEAD_DIM_HGRN
    nh = hw // d
    tt = _tile(t, 2048, c)
    n_chunks = tt // c
    rows = _tile(tt, 2 * LANES, c)
    pb = min(rows, LANES)
    tri, masks = _hgrn_tables(c, rows, pb)
    qb, vb, ob = q_col // d, v_col // d, og_col // d
    kern = functools.partial(_hgrn_kernel, c=c, n_chunks=n_chunks, rows=rows, pb=pb, layer=layer,
                             carry=carry)
    nl = lb_logits.shape[0]
    if carry:
        state_spec = pl.BlockSpec((1, 1, d, d), lambda b, h, i: (b, h, 0, 0))
    else:
        state_spec = pl.BlockSpec((n_chunks, 1, d, d), lambda b, h, i: (i, h, 0, 0))
    return pl.pallas_call(
        kern,
        grid=(bsz, nh, t // tt),
        in_specs=[pl.BlockSpec((rows, rows), lambda b, h, i: (0, 0)),
                  pl.BlockSpec(masks.shape, lambda b, h, i: (0, 0, 0)),
                  pl.BlockSpec((nl, d), lambda b, h, i: (0, h)),
                  pl.BlockSpec((1, tt, d), lambda b, h, i: (b, i, qb + h)),
                  pl.BlockSpec((1, tt, d), lambda b, h, i: (b, i, h)),
                  pl.BlockSpec((1, tt, d), lambda b, h, i: (b, i, vb + h)),
                  pl.BlockSpec((1, tt, d), lambda b, h, i: (b, i, ob + h)),
                  pl.BlockSpec((1, d), lambda b, h, i: (0, h)),
                  state_spec],
        out_specs=[pl.BlockSpec((1, tt, d), lambda b, h, i: (b, i, h)),
                   state_spec],
        out_shape=[jax.ShapeDtypeStruct((bsz, t, hw), BF16),
                   jax.ShapeDtypeStruct(s0.shape, F32)],
        scratch_shapes=[pltpu.VMEM((d, d), F32),
                        pltpu.VMEM((tt, d), F32),
                        pltpu.VMEM((tt, d), BF16),
                        pltpu.VMEM((n_chunks, d, d), F32),
                        pltpu.VMEM((n_chunks * SUBLANES, d), F32)],
        compiler_params=_params("parallel", "parallel", "arbitrary"),
        name="hgrn2",
    )(jnp.asarray(tri, BF16), jnp.asarray(masks, F32), lb_logits.astype(F32),
      proj3, fpre3, proj3, proj3, gain.reshape(1, hw).astype(F32), s0.astype(F32))


CONV_PAD = 8
CONV_BLOCK_ELEMS = 1024 * 1024


def _conv_kernel(cc_ref, ch_ref, cb_ref, w_ref, buf_ref, o_ref, nbuf_ref, u_ref, *, t, k, rb):
    u_ref[CONV_PAD - (k - 1):CONV_PAD, :] = buf_ref[0].astype(F32)
    for r in range(0, t, rb):
        u_ref[CONV_PAD + r:CONV_PAD + r + rb, :] = (
            cc_ref[0, r:r + rb, :].astype(F32) * ch_ref[0, r:r + rb, :].astype(F32))
    w = w_ref[...].astype(F32)
    for r in range(0, t, rb):
        z = w[0:1, :] * u_ref[CONV_PAD - (k - 1) + r:CONV_PAD - (k - 1) + r + rb, :]
        for j in range(1, k):
            s = CONV_PAD - (k - 1) + j + r
            z = z + w[j:j + 1, :] * u_ref[s:s + rb, :]
        o_ref[0, r:r + rb, :] = (cb_ref[0, r:r + rb, :].astype(F32) * z).astype(o_ref.dtype)
    nbuf_ref[0] = u_ref[CONV_PAD + t - (k - 1):CONV_PAD + t, :]


def _conv(proj3, conv_w, buf, ch_col, cb_col, cc_col):
    bsz, t, _ = proj3.shape
    k, cw = conv_w.shape
    assert k - 1 <= CONV_PAD
    d = _tile(cw, max(LANES, CONV_BLOCK_ELEMS // t), LANES)
    rb = _tile(t, max(8, 512 * LANES // d), 8)
    kern = functools.partial(_conv_kernel, t=t, k=k, rb=rb)
    return pl.pallas_call(
        kern,
        grid=(bsz, cw // d),
        in_specs=[pl.BlockSpec((1, t, d), lambda b, j: (b, 0, cc_col // d + j)),
                  pl.BlockSpec((1, t, d), lambda b, j: (b, 0, ch_col // d + j)),
                  pl.BlockSpec((1, t, d), lambda b, j: (b, 0, cb_col // d + j)),
                  pl.BlockSpec((k, d), lambda b, j: (0, j)),
                  pl.BlockSpec((1, k - 1, d), lambda b, j: (b, 0, j))],
        out_specs=[pl.BlockSpec((1, t, d), lambda b, j: (b, 0, j)),
                   pl.BlockSpec((1, k - 1, d), lambda b, j: (b, 0, j))],
        out_shape=[jax.ShapeDtypeStruct((bsz, t, cw), BF16),
                   jax.ShapeDtypeStruct((bsz, k - 1, cw), F32)],
        scratch_shapes=[pltpu.VMEM((CONV_PAD + t, d), F32)],
        compiler_params=_params("parallel", "parallel"),
        name="short_conv",
    )(proj3, proj3, proj3, conv_w.astype(F32), buf.astype(F32))


def _attn_kernel(q_ref, k_ref, v_ref, o_ref, *, scale):
    nt = (((1,), (1,)), ((), ()))
    s = lax.dot_general(q_ref[0], k_ref[0], nt, preferred_element_type=F32) * scale
    s = s - jnp.max(s, axis=-1, keepdims=True)
    p = jnp.exp(s)
    p = p / jnp.sum(p, axis=-1, keepdims=True)
    o_ref[0] = jnp.dot(p.astype(BF16), v_ref[0], preferred_element_type=F32).astype(o_ref.dtype)


def _attention(q3, mk3, mv3, n_heads):
    bsz, t, dm = q3.shape
    n_mem = mk3.shape[1]
    hd = dm // n_heads
    bt = _tile(t, 512, 8)
    kern = functools.partial(_attn_kernel, scale=float(hd) ** -0.5)
    return pl.pallas_call(
        kern,
        grid=(bsz, t // bt, n_heads),
        in_specs=[pl.BlockSpec((1, bt, hd), lambda b, i, h: (b, i, h)),
                  pl.BlockSpec((1, n_mem, hd), lambda b, i, h: (b, 0, h)),
                  pl.BlockSpec((1, n_mem, hd), lambda b, i, h: (b, 0, h))],
        out_specs=pl.BlockSpec((1, bt, hd), lambda b, i, h: (b, i, h)),
        out_shape=jax.ShapeDtypeStruct((bsz, t, dm), BF16),
        compiler_params=_params("parallel", "parallel", "parallel"),
        name="cross_attention",
    )(q3, mk3, mv3)


def _layer(x, bsz, t, wts, lb_logits, layer, s0, buf, mk3, mv3, n_xa_heads):
    m, dm = x.shape
    hw = wts["hg_norm"].shape[0]
    cw = wts["conv_w"].shape[1]
    q_col, v_col, og_col = 0, hw, 2 * hw
    ch_col, cb_col, cc_col = 3 * hw, 3 * hw + cw, 3 * hw + 2 * cw
    ga_col, gb_col = 3 * hw + 3 * cw, 3 * hw + 3 * cw + dm

    h = _rmsnorm(x, wts["norm_mix"], BF16)
    pw = wts["w_in"].shape[1]
    proj_cols = [(0, hw), (2 * hw, pw - 2 * hw)]
    if "f32" in wts:
        names = list(wts["f32"])
        proj, cast = _matmul(h, wts["w_in"], BF16, cols=proj_cols, side=[wts["f32"][n] for n in names])
        wts = {k: v for k, v in wts.items() if k != "f32"} | dict(zip(names, cast))
    else:
        proj = _matmul(h, wts["w_in"], BF16, cols=proj_cols)
    fpre = _matmul(h, wts["w_in"], F32, cols=[(hw, hw)])
    proj3 = proj.reshape(bsz, t, proj.shape[1])
    a_in, s_fin = _hgrn(proj3, fpre.reshape(bsz, t, hw), lb_logits, wts["hg_norm"], s0, layer,
                        q_col, v_col, og_col)
    bz, new_buf = _conv(proj3, wts["conv_w"], buf, ch_col, cb_col, cc_col)
    merged = _matmul_gated(a_in.reshape(m, hw), bz.reshape(m, cw), wts["w_a"], wts["w_b"],
                           proj, ga_col, gb_col)
    x = _matmul(merged, wts["w_o"], F32, res=x)

    hq = _rmsnorm(x, wts["norm_xattn"], BF16)
    qx = _matmul(hq, wts["w_xq"], BF16)
    att = _attention(qx.reshape(bsz, t, dm), mk3, mv3, n_xa_heads)
    x = _matmul(att.reshape(m, dm), wts["w_xo"], F32, res=x)

    hf = _rmsnorm(x, wts["norm_ffn"], BF16)
    act = _matmul_swiglu(hf, wts["w_gate"], wts["w_up"])
    x = _matmul(act, wts["w_down"], F32, bm_pref=512, bn_pref=512, res=x)
    return x, s_fin, new_buf, wts


def kernel(x_prompt, x_sample, cache_mem_k, cache_mem_v, state_hgrn, state_conv, mem_prompt, norm_mix, w_in, lb_logits, hg_norm, conv_w, w_a, w_b, w_o, norm_xattn, norm_mem, w_xq, w_xk, w_xv, w_xo, norm_ffn, w_gate, w_up, w_down, norm_final):
    depth = norm_mix.shape[0]
    bp, tp, dm = x_prompt.shape
    bs, ts, _ = x_sample.shape
    hw = hg_norm.shape[1]
    n_mem = mem_prompt.shape[1]
    n_xa_heads = cache_mem_k.shape[3]
    nh, dk, dv = state_hgrn.shape[2:]
    assert dk == HEAD_DIM_HGRN and dv == HEAD_DIM_HGRN and nh * dk == hw

    xp = x_prompt.reshape(bp * tp, dm)
    xs = x_sample.reshape(bs * ts, dm)
    outs = {k: [] for k in ("mk", "mv", "sp", "cp", "ss", "cs")}
    for l in range(depth):
        w_in_l = w_in[l]
        wts = {
            "norm_mix": norm_mix[l], "hg_norm": hg_norm[l], "conv_w": conv_w[l],
            "norm_xattn": norm_xattn[l], "norm_ffn": norm_ffn[l],
            "w_in": w_in_l.astype(BF16),
            "f32": {"w_a": w_a[l], "w_b": w_b[l], "w_o": w_o[l], "w_xq": w_xq[l], "w_xo": w_xo[l],
                    "w_gate": w_gate[l], "w_up": w_up[l], "w_down": w_down[l]},
        }
        mem_n = _rmsnorm(mem_prompt.reshape(bp * n_mem, dm), norm_mem[l], BF16)
        mk_p = _matmul(mem_n, w_xk[l], F32)
        mv_p = _matmul(mem_n, w_xv[l], F32)
        s0 = jnp.zeros((bp, nh, dk, dv), F32)
        buf0 = jnp.zeros((bp, conv_w.shape[1] - 1, conv_w.shape[2]), F32)
        xp, s_p, buf_p, wts = _layer(xp, bp, tp, wts, lb_logits, l, s0, buf0,
                                     mk_p.astype(BF16).reshape(bp, n_mem, dm),
                                     mv_p.astype(BF16).reshape(bp, n_mem, dm), n_xa_heads)
        xs, s_s, buf_s, _ = _layer(xs, bs, ts, wts, lb_logits, l, state_hgrn[l], state_conv[l],
                                   cache_mem_k[l].astype(BF16).reshape(bs, n_mem, dm),
                                   cache_mem_v[l].astype(BF16).reshape(bs, n_mem, dm), n_xa_heads)
        outs["mk"].append(mk_p.reshape(bp, n_mem, n_xa_heads, dm // n_xa_heads))
        outs["mv"].append(mv_p.reshape(bp, n_mem, n_xa_heads, dm // n_xa_heads))
        outs["sp"].append(s_p)
        outs["cp"].append(buf_p)
        outs["ss"].append(s_s)
        outs["cs"].append(buf_s)
    y_prompt = _rmsnorm(xp, norm_final, F32).reshape(bp, tp, dm)
    y_sample = _rmsnorm(xs, norm_final, F32).reshape(bs, ts, dm)
    return (y_prompt, y_sample, jnp.stack(outs["mk"]), jnp.stack(outs["mv"]), jnp.stack(outs["sp"]),
            jnp.stack(outs["cp"]), jnp.stack(outs["ss"]), jnp.stack(outs["cs"]))
```

```python
import functools

import numpy as np
import jax
import jax.numpy as jnp
from jax import lax
from jax.experimental import pallas as pl
from jax.experimental.pallas import tpu as pltpu

EPS = 1e-6
LANES = 128
SUBLANES = 8
HEAD_DIM_HGRN = 128
CHUNK = 64
VMEM_LIMIT_BYTES = 56 * 1024 * 1024
BF16 = jnp.bfloat16
F32 = jnp.float32


def _params(*semantics):
    return pltpu.CompilerParams(dimension_semantics=semantics, vmem_limit_bytes=VMEM_LIMIT_BYTES)


def _tile(n, pref, mult):
    if n <= pref:
        return n
    t = (pref // mult) * mult
    while t >= mult:
        if n % t == 0:
            return t
        t -= mult
    raise ValueError(f"no tile for {n} (pref {pref}, mult {mult})")


def _rmsnorm_kernel(x_ref, g_ref, o_ref):
    x = x_ref[...].astype(F32)
    y = x * lax.rsqrt(jnp.mean(x * x, axis=-1, keepdims=True) + EPS)
    o_ref[...] = (y * g_ref[...]).astype(o_ref.dtype)


def _rmsnorm(x, g, out_dtype):
    m, d = x.shape
    bm = _tile(m, 512, 8)
    return pl.pallas_call(
        _rmsnorm_kernel,
        grid=(m // bm,),
        in_specs=[pl.BlockSpec((bm, d), lambda i: (i, 0)),
                  pl.BlockSpec((1, d), lambda i: (0, 0))],
        out_specs=pl.BlockSpec((bm, d), lambda i: (i, 0)),
        out_shape=jax.ShapeDtypeStruct((m, d), out_dtype),
        compiler_params=_params("parallel"),
        name="rmsnorm",
    )(x, g.reshape(1, d).astype(F32))


def _mm_kernel(*refs, has_res, n_side):
    x_ref, w_ref = refs[0], refs[1]
    r_ref = refs[2] if has_res else None
    src = refs[2 + has_res:2 + has_res + n_side]
    o_ref = refs[2 + has_res + n_side]
    dst = refs[3 + has_res + n_side:]
    acc = jnp.dot(x_ref[...], w_ref[...].astype(BF16), preferred_element_type=F32)
    if has_res:
        acc = r_ref[...] + acc
    o_ref[...] = acc.astype(o_ref.dtype)
    for src_ref, dst_ref in zip(src, dst):
        dst_ref[...] = src_ref[...].astype(BF16)


def _mm_gated_kernel(a_ref, b_ref, wa_ref, wb_ref, ga_ref, gb_ref, o_ref):
    a = jnp.dot(a_ref[...], wa_ref[...], preferred_element_type=F32)
    b = jnp.dot(b_ref[...], wb_ref[...], preferred_element_type=F32)
    ga = jax.nn.sigmoid(ga_ref[...].astype(F32))
    gb = jax.nn.sigmoid(gb_ref[...].astype(F32))
    o_ref[...] = (ga * a + gb * b).astype(o_ref.dtype)


def _mm_swiglu_kernel(x_ref, wg_ref, wu_ref, o_ref):
    x = x_ref[...]
    g = jnp.dot(x, wg_ref[...], preferred_element_type=F32)
    u = jnp.dot(x, wu_ref[...], preferred_element_type=F32)
    o_ref[...] = (g * jax.nn.sigmoid(g) * u).astype(o_ref.dtype)


BF16_SUBLANES = 16


def _matmul(x, w, out_dtype, *, bm_pref=1024, bn_pref=1024, res=None, cols=None, side=()):
    m, k = x.shape
    cols = cols or [(0, w.shape[1])]
    n = sum(width for _, width in cols)
    bm = _tile(m, bm_pref, 8)
    bn = _tile(int(functools.reduce(np.gcd, [v for seg in cols for v in seg if v])), bn_pref, LANES)
    steps, out_blk = [], 0
    for start, width in cols:
        steps.append((out_blk, start // bn - out_blk))
        out_blk += width // bn

    def w_block(j):
        blk = j + steps[0][1]
        for (first, shift), (_, prev) in zip(steps[1:], steps[:-1]):
            blk = blk + jnp.where(j >= first, shift - prev, 0)
        return blk

    ni, nj = m // bm, n // bn
    in_specs = [pl.BlockSpec((bm, k), lambda i, j: (i, 0)),
                pl.BlockSpec((k, bn), lambda i, j: (0, w_block(j)))]
    args = [x, w]
    if res is not None:
        in_specs.append(pl.BlockSpec((bm, bn), lambda i, j: (i, j)))
        args.append(res)
    out_specs = [pl.BlockSpec((bm, bn), lambda i, j: (i, j))]
    out_shape = [jax.ShapeDtypeStruct((m, n), out_dtype)]
    for sw in side:
        rows, width = sw.shape
        n_blocks = max(nb for nb in range(1, ni * nj + 1)
                       if rows % nb == 0 and (rows // nb) % BF16_SUBLANES == 0)
        spec = pl.BlockSpec((rows // n_blocks, width),
                            lambda i, j, n_blocks=n_blocks: (jnp.minimum(i * nj + j, n_blocks - 1), 0))
        in_specs.append(spec)
        args.append(sw)
        out_specs.append(spec)
        out_shape.append(jax.ShapeDtypeStruct((rows, width), BF16))
    outs = pl.pallas_call(
        functools.partial(_mm_kernel, has_res=res is not None, n_side=len(side)),
        grid=(ni, nj),
        in_specs=in_specs,
        out_specs=out_specs,
        out_shape=out_shape,
        compiler_params=_params("arbitrary", "arbitrary") if side else _params("parallel", "parallel"),
        name="matmul_res" if res is not None else "matmul",
    )(*args)
    return (outs[0], list(outs[1:])) if side else outs[0]


def _matmul_gated(a, b, wa, wb, proj, ga_col, gb_col):
    m, ka = a.shape
    kb = b.shape[1]
    n = wa.shape[1]
    bm = _tile(m, 1024, 8)
    bn = _tile(n, 1024, LANES)
    assert ga_col % bn == 0 and gb_col % bn == 0
    ga_blk, gb_blk = ga_col // bn, gb_col // bn
    return pl.pallas_call(
        _mm_gated_kernel,
        grid=(m // bm, n // bn),
        in_specs=[pl.BlockSpec((bm, ka), lambda i, j: (i, 0)),
                  pl.BlockSpec((bm, kb), lambda i, j: (i, 0)),
                  pl.BlockSpec((ka, bn), lambda i, j: (0, j)),
                  pl.BlockSpec((kb, bn), lambda i, j: (0, j)),
                  pl.BlockSpec((bm, bn), lambda i, j: (i, ga_blk + j)),
                  pl.BlockSpec((bm, bn), lambda i, j: (i, gb_blk + j))],
        out_specs=pl.BlockSpec((bm, bn), lambda i, j: (i, j)),
        out_shape=jax.ShapeDtypeStruct((m, n), BF16),
        compiler_params=_params("parallel", "parallel"),
        name="matmul_gated",
    )(a, b, wa, wb, proj, proj)


def _matmul_swiglu(x, wg, wu):
    m, k = x.shape
    n = wg.shape[1]
    bm = _tile(m, 2048, 8)
    bn = _tile(n, 256, LANES)
    return pl.pallas_call(
        _mm_swiglu_kernel,
        grid=(m // bm, n // bn),
        in_specs=[pl.BlockSpec((bm, k), lambda i, j: (i, 0)),
                  pl.BlockSpec((k, bn), lambda i, j: (0, j)),
                  pl.BlockSpec((k, bn), lambda i, j: (0, j))],
        out_specs=pl.BlockSpec((bm, bn), lambda i, j: (i, j)),
        out_shape=jax.ShapeDtypeStruct((m, n), BF16),
        compiler_params=_params("parallel", "parallel"),
        name="matmul_swiglu",
    )(x, wg, wu)


def _hgrn_tables(c, rows, pb):
    t = np.arange(rows)[:, None]
    s = np.arange(rows)[None, :]
    tri = ((t // c) == (s // c)) & (s <= t)
    t, s = t[:pb, :pb], s[:pb, :pb]
    masks = []
    l = 1
    while l < c:
        masks.append(((t // l) % 2 == 1) & ((s // l) == (t // l) - 1))
        l *= 2
    masks.append(t == s)
    return tri.astype(np.float32), np.stack(masks, 0).astype(np.float32)


def _hgrn_kernel(tri_ref, mask_ref, lbl_ref, q_ref, f_ref, v_ref, og_ref, gn_ref, s0_ref,
                 o_ref, sfin_ref, st_ref, oi_ref, qin_ref, u_ref, dec_ref,
                 *, c, n_chunks, rows, pb, layer, carry):
    t_idx = pl.program_id(2)
    d = HEAD_DIM_HGRN
    n_lv = c.bit_length() - 1
    g_chunks = rows // c
    n_pb = rows // pb
    nt = (((1,), (1,)), ((), ()))
    tn = (((0,), (0,)), ((), ()))

    if carry:
        @pl.when(t_idx == 0)
        def _():
            st_ref[...] = s0_ref[0, 0].T

    logits = lbl_ref[...].astype(F32)
    ex = jnp.exp(logits - jnp.max(logits, axis=0, keepdims=True))
    lb = jnp.sum(ex[:layer + 1], axis=0, keepdims=True) / jnp.sum(ex, axis=0, keepdims=True)
    gain = gn_ref[...].astype(F32)
    shape3 = (rows // SUBLANES, SUBLANES, d)
    sub = lax.broadcasted_iota(jnp.int32, shape3, 1)

    def block_rows(x, first, step, length):
        pieces = [jnp.broadcast_to(x[first + step * p:first + step * p + 1, :], (length, d))
                  for p in range(rows // length)]
        return pieces[0] if len(pieces) == 1 else jnp.concatenate(pieces, axis=0)

    def pair_ref(b, l):
        if l >= SUBLANES:
            return block_rows(b, l - 1, 2 * l, 2 * l)
        b3 = b.reshape(shape3)
        pick = lambda i: jnp.broadcast_to(b3[:, i:i + 1, :], shape3)
        r3 = pick(3) if l == 4 else jnp.where(sub < 4, pick(1), pick(5))
        return r3.reshape(rows, d)

    def odd_half(x_odd, x_even, l):
        if l >= SUBLANES:
            pieces = [(x_odd if p % 2 else x_even)[p * l:(p + 1) * l] for p in range(rows // l)]
            return jnp.concatenate(pieces, axis=0)
        return jnp.where((sub & l) != 0, x_odd.reshape(shape3), x_even.reshape(shape3)).reshape(rows, d)

    def phase_a(bi, carry):
        rsel = pl.ds(pl.multiple_of(bi * rows, rows), rows)
        f = lb + (1.0 - lb) * jax.nn.sigmoid(f_ref[0, rsel, :])
        g = jnp.log(f)
        k = 1.0 - f
        qb = q_ref[0, rsel, :]
        q = qb.astype(F32)
        v = v_ref[0, rsel, :]

        g_hi = g.astype(BF16)
        r1 = g - g_hi.astype(F32)
        g_mid = r1.astype(BF16)
        g_lo = (r1 - g_mid.astype(F32)).astype(BF16)
        b3 = jnp.dot(tri_ref[...], jnp.concatenate([g_hi, g_mid, g_lo], axis=1),
                     preferred_element_type=F32)
        b = b3[:, 0:d] + b3[:, d:2 * d] + b3[:, 2 * d:3 * d]

        def scores(zl, zr, li):
            p = lax.dot_general(zl, zr, nt, preferred_element_type=F32)
            return [mask_ref[li] * p[i * pb:(i + 1) * pb, i * pb:(i + 1) * pb] for i in range(n_pb)]

        a = scores(qb, k.astype(BF16), n_lv)
        z = odd_half(q * f, k, 1).astype(BF16)
        a = [x + y for x, y in zip(a, scores(z, z, 0))]
        for li in range(1, n_lv):
            l = 1 << li
            w = jnp.exp(-jnp.abs(b - pair_ref(b, l)))
            z = (odd_half(q, k, l) * w).astype(BF16)
            a = [x + y for x, y in zip(a, scores(z, z, li))]
        for i in range(n_pb):
            oi_ref[pl.ds(pl.multiple_of(bi * rows + i * pb, pb), pb), :] = jnp.dot(
                a[i].astype(BF16), v[i * pb:(i + 1) * pb], preferred_element_type=F32)

        dec = jnp.exp(b)
        qin_ref[rsel, :] = (q * dec).astype(BF16)
        k_out = (k * jnp.exp(block_rows(b, c - 1, c, c) - b)).astype(BF16)
        for ci in range(g_chunks):
            cr = slice(ci * c, (ci + 1) * c)
            u_ref[bi * g_chunks + ci] = lax.dot_general(v[cr], k_out[cr], tn, preferred_element_type=F32)
            dec_ref[pl.ds(pl.multiple_of((bi * g_chunks + ci) * SUBLANES, SUBLANES), SUBLANES), :] = (
                jnp.broadcast_to(dec[ci * c + c - 1:ci * c + c, :], (SUBLANES, d)))
        return carry

    def phase_bc(bi):
        st = st_ref[...] if carry else None
        outs = []
        for ci in range(g_chunks):
            chunk = bi * g_chunks + ci
            rsel = pl.ds(pl.multiple_of(chunk * c, c), c)
            if not carry:
                st = s0_ref[chunk, 0].T
            outs.append(oi_ref[rsel, :] + lax.dot_general(
                qin_ref[rsel, :], st.astype(BF16), nt, preferred_element_type=F32))
            st = st * dec_ref[pl.ds(pl.multiple_of(chunk * SUBLANES, SUBLANES), 1), :] + u_ref[chunk]
            if not carry:
                sfin_ref[chunk, 0] = st.T
        if carry:
            st_ref[...] = st
        rsel = pl.ds(pl.multiple_of(bi * rows, rows), rows)
        o = outs[0] if g_chunks == 1 else jnp.concatenate(outs, axis=0)
        y = o * lax.rsqrt(jnp.mean(o * o, axis=-1, keepdims=True) + EPS) * gain
        og = og_ref[0, rsel, :].astype(F32)
        o_ref[0, rsel, :] = (y * (og * jax.nn.sigmoid(og))).astype(o_ref.dtype)

    n_batches = n_chunks // g_chunks
    for bi in range(n_batches):
        phase_a(bi, 0)
        if bi > 0:
            phase_bc(bi - 1)
    phase_bc(n_batches - 1)

    if carry:
        @pl.when(t_idx == pl.num_programs(2) - 1)
        def _():
            sfin_ref[0, 0] = st_ref[...].T


def _hgrn(proj3, fpre3, lb_logits, gain, s0, layer, q_col, v_col, og_col):
    bsz, t, _ = proj3.shape
    hw = fpre3.shape[-1]
    d = HEAD_DIM_HGRN
    nh = hw // d
    c = CHUNK if t % CHUNK == 0 else t
    assert c & (c - 1) == 0 and c >= 2 * SUBLANES and t % c == 0
    carry = t > c
    if not carry:
        out, s_fin = _hgrn_call(proj3.reshape(1, bsz * t, -1), fpre3.reshape(1, bsz * t, hw), lb_logits,
                                gain, s0, layer, q_col, v_col, og_col, c=c, carry=False)
        return out.reshape(bsz, t, hw), s_fin
    return _hgrn_call(proj3, fpre3, lb_logits, gain, s0, layer, q_col, v_col, og_col, c=c, carry=True)


def _hgrn_call(proj3, fpre3, lb_logits, gain, s0, layer, q_col, v_col, og_col, *, c, carry):
    bsz, t, _ = proj3.shape
    hw = fpre3.shape[-1]
    d = HEAD_DIM_HGRN
    nh = hw // d
    tt = _tile(t, 2048, c)
    n_chunks = tt // c
    rows = _tile(tt, 2 * LANES, c)
    pb = min(rows, LANES)
    tri, masks = _hgrn_tables(c, rows, pb)
    qb, vb, ob = q_col // d, v_col // d, og_col // d
    kern = functools.partial(_hgrn_kernel, c=c, n_chunks=n_chunks, rows=rows, pb=pb, layer=layer,
                             carry=carry)
    nl = lb_logits.shape[0]
    if carry:
        state_spec = pl.BlockSpec((1, 1, d, d), lambda b, h, i: (b, h, 0, 0))
    else:
        state_spec = pl.BlockSpec((n_chunks, 1, d, d), lambda b, h, i: (i, h, 0, 0))
    return pl.pallas_call(
        kern,
        grid=(bsz, nh, t // tt),
        in_specs=[pl.BlockSpec((rows, rows), lambda b, h, i: (0, 0)),
                  pl.BlockSpec(masks.shape, lambda b, h, i: (0, 0, 0)),
                  pl.BlockSpec((nl, d), lambda b, h, i: (0, h)),
                  pl.BlockSpec((1, tt, d), lambda b, h, i: (b, i, qb + h)),
                  pl.BlockSpec((1, tt, d), lambda b, h, i: (b, i, h)),
                  pl.BlockSpec((1, tt, d), lambda b, h, i: (b, i, vb + h)),
                  pl.BlockSpec((1, tt, d), lambda b, h, i: (b, i, ob + h)),
                  pl.BlockSpec((1, d), lambda b, h, i: (0, h)),
                  state_spec],
        out_specs=[pl.BlockSpec((1, tt, d), lambda b, h, i: (b, i, h)),
                   state_spec],
        out_shape=[jax.ShapeDtypeStruct((bsz, t, hw), BF16),
                   jax.ShapeDtypeStruct(s0.shape, F32)],
        scratch_shapes=[pltpu.VMEM((d, d), F32),
                        pltpu.VMEM((tt, d), F32),
                        pltpu.VMEM((tt, d), BF16),
                        pltpu.VMEM((n_chunks, d, d), F32),
                        pltpu.VMEM((n_chunks * SUBLANES, d), F32)],
        compiler_params=_params("parallel", "parallel", "arbitrary"),
        name="hgrn2",
    )(jnp.asarray(tri, BF16), jnp.asarray(masks, F32), lb_logits.astype(F32),
      proj3, fpre3, proj3, proj3, gain.reshape(1, hw).astype(F32), s0.astype(F32))


CONV_PAD = 8
CONV_BLOCK_ELEMS = 1024 * 1024


def _conv_kernel(cc_ref, ch_ref, cb_ref, w_ref, buf_ref, o_ref, nbuf_ref, u_ref, *, t, k, rb):
    u_ref[CONV_PAD - (k - 1):CONV_PAD, :] = buf_ref[0].astype(F32)
    for r in range(0, t, rb):
        u_ref[CONV_PAD + r:CONV_PAD + r + rb, :] = (
            cc_ref[0, r:r + rb, :].astype(F32) * ch_ref[0, r:r + rb, :].astype(F32))
    w = w_ref[...].astype(F32)
    for r in range(0, t, rb):
        z = w[0:1, :] * u_ref[CONV_PAD - (k - 1) + r:CONV_PAD - (k - 1) + r + rb, :]
        for j in range(1, k):
            s = CONV_PAD - (k - 1) + j + r
            z = z + w[j:j + 1, :] * u_ref[s:s + rb, :]
        o_ref[0, r:r + rb, :] = (cb_ref[0, r:r + rb, :].astype(F32) * z).astype(o_ref.dtype)
    nbuf_ref[0] = u_ref[CONV_PAD + t - (k - 1):CONV_PAD + t, :]


def _conv(proj3, conv_w, buf, ch_col, cb_col, cc_col):
    bsz, t, _ = proj3.shape
    k, cw = conv_w.shape
    assert k - 1 <= CONV_PAD
    d = _tile(cw, max(LANES, CONV_BLOCK_ELEMS // t), LANES)
    rb = _tile(t, max(8, 512 * LANES // d), 8)
    kern = functools.partial(_conv_kernel, t=t, k=k, rb=rb)
    return pl.pallas_call(
        kern,
        grid=(bsz, cw // d),
        in_specs=[pl.BlockSpec((1, t, d), lambda b, j: (b, 0, cc_col // d + j)),
                  pl.BlockSpec((1, t, d), lambda b, j: (b, 0, ch_col // d + j)),
                  pl.BlockSpec((1, t, d), lambda b, j: (b, 0, cb_col // d + j)),
                  pl.BlockSpec((k, d), lambda b, j: (0, j)),
                  pl.BlockSpec((1, k - 1, d), lambda b, j: (b, 0, j))],
        out_specs=[pl.BlockSpec((1, t, d), lambda b, j: (b, 0, j)),
                   pl.BlockSpec((1, k - 1, d), lambda b, j: (b, 0, j))],
        out_shape=[jax.ShapeDtypeStruct((bsz, t, cw), BF16),
                   jax.ShapeDtypeStruct((bsz, k - 1, cw), F32)],
        scratch_shapes=[pltpu.VMEM((CONV_PAD + t, d), F32)],
        compiler_params=_params("parallel", "parallel"),
        name="short_conv",
    )(proj3, proj3, proj3, conv_w.astype(F32), buf.astype(F32))


def _attn_kernel(q_ref, k_ref, v_ref, o_ref, kb_ref, vb_ref, *, scale, sub_rows):
    nt = (((1,), (1,)), ((), ()))

    @pl.when(pl.program_id(2) == 0)
    def _():
        kb_ref[...] = k_ref[0].astype(BF16)
        vb_ref[...] = v_ref[0].astype(BF16)

    subs = [pl.ds(r, sub_rows) for r in range(0, q_ref.shape[1], sub_rows)]
    scores = [lax.dot_general(q_ref[0, r, :], kb_ref[...], nt, preferred_element_type=F32) * scale
              for r in subs]
    probs = []
    for s in scores:
        p = jnp.exp(s - jnp.max(s, axis=-1, keepdims=True))
        probs.append((p / jnp.sum(p, axis=-1, keepdims=True)).astype(BF16))
    for r, p in zip(subs, probs):
        o_ref[0, r, :] = jnp.dot(p, vb_ref[...], preferred_element_type=F32).astype(o_ref.dtype)


def _attention(q3, mk3, mv3, n_heads):
    bsz, t, dm = q3.shape
    n_mem = mk3.shape[1]
    hd = dm // n_heads
    bt = _tile(t, 2048, 8)
    kern = functools.partial(_attn_kernel, scale=float(hd) ** -0.5, sub_rows=_tile(bt, 512, 8))
    return pl.pallas_call(
        kern,
        grid=(bsz, n_heads, t // bt),
        in_specs=[pl.BlockSpec((1, bt, hd), lambda b, h, i: (b, i, h)),
                  pl.BlockSpec((1, n_mem, hd), lambda b, h, i: (b, 0, h)),
                  pl.BlockSpec((1, n_mem, hd), lambda b, h, i: (b, 0, h))],
        out_specs=pl.BlockSpec((1, bt, hd), lambda b, h, i: (b, i, h)),
        out_shape=jax.ShapeDtypeStruct((bsz, t, dm), BF16),
        scratch_shapes=[pltpu.VMEM((n_mem, hd), BF16), pltpu.VMEM((n_mem, hd), BF16)],
        compiler_params=_params("parallel", "parallel", "arbitrary"),
        name="cross_attention",
    )(q3, mk3, mv3)


def _layer(x, bsz, t, wts, lb_logits, layer, s0, buf, mk3, mv3, n_xa_heads):
    m, dm = x.shape
    hw = wts["hg_norm"].shape[0]
    cw = wts["conv_w"].shape[1]
    q_col, v_col, og_col = 0, hw, 2 * hw
    ch_col, cb_col, cc_col = 3 * hw, 3 * hw + cw, 3 * hw + 2 * cw
    ga_col, gb_col = 3 * hw + 3 * cw, 3 * hw + 3 * cw + dm

    h = _rmsnorm(x, wts["norm_mix"], BF16)
    pw = wts["w_in"].shape[1]
    proj_cols = [(0, hw), (2 * hw, pw - 2 * hw)]
    if "f32" in wts:
        names = list(wts["f32"])
        proj, cast = _matmul(h, wts["w_in"], BF16, cols=proj_cols, side=[wts["f32"][n] for n in names])
        wts = {k: v for k, v in wts.items() if k != "f32"} | dict(zip(names, cast))
    else:
        proj = _matmul(h, wts["w_in"], BF16, cols=proj_cols)
    fpre = _matmul(h, wts["w_in"], F32, cols=[(hw, hw)])
    proj3 = proj.reshape(bsz, t, proj.shape[1])
    a_in, s_fin = _hgrn(proj3, fpre.reshape(bsz, t, hw), lb_logits, wts["hg_norm"], s0, layer,
                        q_col, v_col, og_col)
    bz, new_buf = _conv(proj3, wts["conv_w"], buf, ch_col, cb_col, cc_col)
    merged = _matmul_gated(a_in.reshape(m, hw), bz.reshape(m, cw), wts["w_a"], wts["w_b"],
                           proj, ga_col, gb_col)
    x = _matmul(merged, wts["w_o"], F32, res=x)

    hq = _rmsnorm(x, wts["norm_xattn"], BF16)
    qx = _matmul(hq, wts["w_xq"], BF16)
    att = _attention(qx.reshape(bsz, t, dm), mk3, mv3, n_xa_heads)
    x = _matmul(att.reshape(m, dm), wts["w_xo"], F32, res=x)

    hf = _rmsnorm(x, wts["norm_ffn"], BF16)
    act = _matmul_swiglu(hf, wts["w_gate"], wts["w_up"])
    x = _matmul(act, wts["w_down"], F32, bm_pref=512, bn_pref=512, res=x)
    return x, s_fin, new_buf, wts


def kernel(x_prompt, x_sample, cache_mem_k, cache_mem_v, state_hgrn, state_conv, mem_prompt, norm_mix, w_in, lb_logits, hg_norm, conv_w, w_a, w_b, w_o, norm_xattn, norm_mem, w_xq, w_xk, w_xv, w_xo, norm_ffn, w_gate, w_up, w_down, norm_final):
    depth = norm_mix.shape[0]
    bp, tp, dm = x_prompt.shape
    bs, ts, _ = x_sample.shape
    hw = hg_norm.shape[1]
    n_mem = mem_prompt.shape[1]
    n_xa_heads = cache_mem_k.shape[3]
    nh, dk, dv = state_hgrn.shape[2:]
    assert dk == HEAD_DIM_HGRN and dv == HEAD_DIM_HGRN and nh * dk == hw

    xp = x_prompt.reshape(bp * tp, dm)
    xs = x_sample.reshape(bs * ts, dm)
    outs = {k: [] for k in ("mk", "mv", "sp", "cp", "ss", "cs")}
    for l in range(depth):
        w_in_l = w_in[l]
        wts = {
            "norm_mix": norm_mix[l], "hg_norm": hg_norm[l], "conv_w": conv_w[l],
            "norm_xattn": norm_xattn[l], "norm_ffn": norm_ffn[l],
            "w_in": w_in_l.astype(BF16),
            "f32": {"w_a": w_a[l], "w_b": w_b[l], "w_o": w_o[l], "w_xq": w_xq[l], "w_xo": w_xo[l],
                    "w_gate": w_gate[l], "w_up": w_up[l], "w_down": w_down[l]},
        }
        mem_n = _rmsnorm(mem_prompt.reshape(bp * n_mem, dm), norm_mem[l], BF16)
        mk_p = _matmul(mem_n, w_xk[l], F32)
        mv_p = _matmul(mem_n, w_xv[l], F32)
        s0 = jnp.zeros((bp, nh, dk, dv), F32)
        buf0 = jnp.zeros((bp, conv_w.shape[1] - 1, conv_w.shape[2]), F32)
        xp, s_p, buf_p, wts = _layer(xp, bp, tp, wts, lb_logits, l, s0, buf0,
                                     mk_p.reshape(bp, n_mem, dm), mv_p.reshape(bp, n_mem, dm), n_xa_heads)
        xs, s_s, buf_s, _ = _layer(xs, bs, ts, wts, lb_logits, l, state_hgrn[l], state_conv[l],
                                   cache_mem_k[l].reshape(bs, n_mem, dm),
                                   cache_mem_v[l].reshape(bs, n_mem, dm), n_xa_heads)
        outs["mk"].append(mk_p.reshape(bp, n_mem, n_xa_heads, dm // n_xa_heads))
        outs["mv"].append(mv_p.reshape(bp, n_mem, n_xa_heads, dm // n_xa_heads))
        outs["sp"].append(s_p)
        outs["cp"].append(buf_p)
        outs["ss"].append(s_s)
        outs["cs"].append(buf_s)
    y_prompt = _rmsnorm(xp, norm_final, F32).reshape(bp, tp, dm)
    y_sample = _rmsnorm(xs, norm_final, F32).reshape(bs, ts, dm)
    return (y_prompt, y_sample, jnp.stack(outs["mk"]), jnp.stack(outs["mv"]), jnp.stack(outs["sp"]),
            jnp.stack(outs["cp"]), jnp.stack(outs["ss"]), jnp.stack(outs["cs"]))
```

```python
import functools

import numpy as np
import jax
import jax.numpy as jnp
from jax import lax
from jax.experimental import pallas as pl
from jax.experimental.pallas import tpu as pltpu

EPS = 1e-6
LANES = 128
SUBLANES = 8
HEAD_DIM_HGRN = 128
CHUNK = 64
VMEM_LIMIT_BYTES = 60 * 1024 * 1024
BF16_SUBLANES = 16
BF16 = jnp.bfloat16
F32 = jnp.float32


def _params(*semantics):
    return pltpu.CompilerParams(dimension_semantics=semantics, vmem_limit_bytes=VMEM_LIMIT_BYTES)


def _tile(n, pref, mult):
    if n <= pref:
        return n
    t = (pref // mult) * mult
    while t >= mult:
        if n % t == 0:
            return t
        t -= mult
    raise ValueError(f"no tile for {n} (pref {pref}, mult {mult})")


def _rmsnorm_kernel(x_ref, g_ref, o_ref):
    x = x_ref[...].astype(F32)
    y = x * lax.rsqrt(jnp.mean(x * x, axis=-1, keepdims=True) + EPS)
    o_ref[...] = (y * g_ref[...]).astype(o_ref.dtype)


def _rmsnorm(x, g, out_dtype):
    m, d = x.shape
    bm = _tile(m, 512, 8)
    return pl.pallas_call(
        _rmsnorm_kernel,
        grid=(m // bm,),
        in_specs=[pl.BlockSpec((bm, d), lambda i: (i, 0)),
                  pl.BlockSpec((1, d), lambda i: (0, 0))],
        out_specs=pl.BlockSpec((bm, d), lambda i: (i, 0)),
        out_shape=jax.ShapeDtypeStruct((m, d), out_dtype),
        compiler_params=_params("parallel"),
        name="rmsnorm",
    )(x, g.reshape(1, d).astype(F32))


def _row_scale(rs_ref, n_lanes):
    return pltpu.repeat(rs_ref[...], n_lanes // LANES, axis=1)


MM_SUB_ROWS = 512


def _row_blocks(bm):
    sub = MM_SUB_ROWS if bm % MM_SUB_ROWS == 0 else bm
    return [pl.ds(r, sub) for r in range(0, bm, sub)]


def _mm_kernel(*refs, has_res, has_scale, has_gain, n_side, out_width):
    refs = list(refs)
    x_ref, w_ref = refs.pop(0), refs.pop(0)
    r_ref = refs.pop(0) if has_res else None
    ss_in_ref = refs.pop(0) if has_scale else None
    g_ref = refs.pop(0) if has_gain else None
    src = [refs.pop(0) for _ in range(n_side)]
    o_ref = refs.pop(0)
    og_ref, ss_ref = (refs.pop(0), refs.pop(0)) if has_gain else (None, None)
    dst = refs
    if has_gain:
        @pl.when(pl.program_id(1) == 0)
        def _():
            ss_ref[...] = jnp.zeros_like(ss_ref)

    w = w_ref[...].astype(BF16)
    for rows in _row_blocks(x_ref.shape[0]):
        acc = jnp.dot(x_ref[rows, :], w, preferred_element_type=F32)
        if has_scale:
            acc = acc * _row_scale(ss_in_ref.at[rows], acc.shape[1])
        if has_res:
            acc = r_ref[rows, :] + acc
        o_ref[rows, :] = acc.astype(o_ref.dtype)
        if has_gain:
            og_ref[rows, :] = (acc * g_ref[...]).astype(og_ref.dtype)
            sq = acc * acc
            ss_ref[rows, :] += functools.reduce(
                jnp.add, [sq[:, c:c + LANES] for c in range(0, sq.shape[1], LANES)])

    if has_gain:
        @pl.when(pl.program_id(1) == pl.num_programs(1) - 1)
        def _():
            total = jnp.sum(ss_ref[...], axis=-1, keepdims=True)
            ss_ref[...] = jnp.broadcast_to(lax.rsqrt(total * (1.0 / out_width) + EPS), ss_ref.shape)
    for src_ref, dst_ref in zip(src, dst):
        dst_ref[...] = src_ref[...].astype(BF16)


def _mm_gated_kernel(a_ref, b_ref, wa_ref, wb_ref, ga_ref, gb_ref, o_ref):
    for rows in _row_blocks(a_ref.shape[0]):
        a = jnp.dot(a_ref[rows, :], wa_ref[...], preferred_element_type=F32)
        b = jnp.dot(b_ref[rows, :], wb_ref[...], preferred_element_type=F32)
        ga = jax.nn.sigmoid(ga_ref[rows, :].astype(F32))
        gb = jax.nn.sigmoid(gb_ref[rows, :].astype(F32))
        o_ref[rows, :] = (ga * a + gb * b).astype(o_ref.dtype)


def _mm_swiglu_kernel(x_ref, wg_ref, wu_ref, ss_ref, o_ref):
    for rows in _row_blocks(x_ref.shape[0]):
        x = x_ref[rows, :]
        scale = _row_scale(ss_ref.at[rows], wg_ref.shape[1])
        g = jnp.dot(x, wg_ref[...], preferred_element_type=F32) * scale
        u = jnp.dot(x, wu_ref[...], preferred_element_type=F32) * scale
        o_ref[rows, :] = (g * jax.nn.sigmoid(g) * u).astype(o_ref.dtype)


def _matmul(x, w, out_dtype, *, bm_pref=1024, bn_pref=1024, res=None, cols=None, side=(),
            row_ss=None, norm_gain=None):
    m, k = x.shape
    cols = cols or [(0, w.shape[1])]
    n = sum(width for _, width in cols)
    bm = _tile(m, bm_pref, 8)
    bn = _tile(int(functools.reduce(np.gcd, [v for seg in cols for v in seg if v])), bn_pref, LANES)
    steps, out_blk = [], 0
    for start, width in cols:
        steps.append((out_blk, start // bn - out_blk))
        out_blk += width // bn

    def w_block(j):
        blk = j + steps[0][1]
        for (first, shift), (_, prev) in zip(steps[1:], steps[:-1]):
            blk = blk + jnp.where(j >= first, shift - prev, 0)
        return blk

    ni, nj = m // bm, n // bn
    in_specs = [pl.BlockSpec((bm, k), lambda i, j: (i, 0)),
                pl.BlockSpec((k, bn), lambda i, j: (0, w_block(j)))]
    args = [x, w]
    if res is not None:
        in_specs.append(pl.BlockSpec((bm, bn), lambda i, j: (i, j)))
        args.append(res)
    ss_spec = pl.BlockSpec((bm, LANES), lambda i, j: (i, 0))
    if row_ss is not None:
        in_specs.append(ss_spec)
        args.append(row_ss)
    out_specs = [pl.BlockSpec((bm, bn), lambda i, j: (i, j))]
    out_shape = [jax.ShapeDtypeStruct((m, n), out_dtype)]
    if norm_gain is not None:
        assert not side and cols == [(0, w.shape[1])]
        in_specs.append(pl.BlockSpec((1, bn), lambda i, j: (0, j)))
        args.append(norm_gain.reshape(1, n).astype(F32))
        out_specs += [pl.BlockSpec((bm, bn), lambda i, j: (i, j)), ss_spec]
        out_shape += [jax.ShapeDtypeStruct((m, n), BF16), jax.ShapeDtypeStruct((m, LANES), F32)]
    for sw in side:
        rows, width = sw.shape
        n_blocks = max(nb for nb in range(1, ni * nj + 1)
                       if rows % nb == 0 and (rows // nb) % BF16_SUBLANES == 0)
        spec = pl.BlockSpec((rows // n_blocks, width),
                            lambda i, j, n_blocks=n_blocks: (jnp.minimum(i * nj + j, n_blocks - 1), 0))
        in_specs.append(spec)
        args.append(sw)
        out_specs.append(spec)
        out_shape.append(jax.ShapeDtypeStruct((rows, width), BF16))
    outs = pl.pallas_call(
        functools.partial(_mm_kernel, has_res=res is not None, has_scale=row_ss is not None,
                          has_gain=norm_gain is not None, n_side=len(side), out_width=n),
        grid=(ni, nj),
        in_specs=in_specs,
        out_specs=out_specs,
        out_shape=out_shape,
        compiler_params=(_params("arbitrary", "arbitrary") if side else
                         _params("parallel", "arbitrary") if norm_gain is not None else
                         _params("parallel", "parallel")),
        name="matmul_res" if res is not None else "matmul",
    )(*args)
    if side:
        return outs[0], list(outs[1:])
    return tuple(outs) if norm_gain is not None else outs[0]


def _matmul_gated(a, b, wa, wb, proj, ga_col, gb_col):
    m, ka = a.shape
    kb = b.shape[1]
    n = wa.shape[1]
    bm = _tile(m, 1024, 8)
    bn = _tile(n, 1024, LANES)
    assert ga_col % bn == 0 and gb_col % bn == 0
    ga_blk, gb_blk = ga_col // bn, gb_col // bn
    return pl.pallas_call(
        _mm_gated_kernel,
        grid=(m // bm, n // bn),
        in_specs=[pl.BlockSpec((bm, ka), lambda i, j: (i, 0)),
                  pl.BlockSpec((bm, kb), lambda i, j: (i, 0)),
                  pl.BlockSpec((ka, bn), lambda i, j: (0, j)),
                  pl.BlockSpec((kb, bn), lambda i, j: (0, j)),
                  pl.BlockSpec((bm, bn), lambda i, j: (i, ga_blk + j)),
                  pl.BlockSpec((bm, bn), lambda i, j: (i, gb_blk + j))],
        out_specs=pl.BlockSpec((bm, bn), lambda i, j: (i, j)),
        out_shape=jax.ShapeDtypeStruct((m, n), BF16),
        compiler_params=_params("parallel", "parallel"),
        name="matmul_gated",
    )(a, b, wa, wb, proj, proj)


def _matmul_swiglu(x, row_ss, wg, wu):
    m, k = x.shape
    n = wg.shape[1]
    bm = _tile(m, 2048, 8)
    bn = _tile(n, 256, LANES)
    return pl.pallas_call(
        _mm_swiglu_kernel,
        grid=(m // bm, n // bn),
        in_specs=[pl.BlockSpec((bm, k), lambda i, j: (i, 0)),
                  pl.BlockSpec((k, bn), lambda i, j: (0, j)),
                  pl.BlockSpec((k, bn), lambda i, j: (0, j)),
                  pl.BlockSpec((bm, LANES), lambda i, j: (i, 0))],
        out_specs=pl.BlockSpec((bm, bn), lambda i, j: (i, j)),
        out_shape=jax.ShapeDtypeStruct((m, n), BF16),
        compiler_params=_params("parallel", "parallel"),
        name="matmul_swiglu",
    )(x, wg, wu, row_ss)


def _hgrn_tables(c, rows, pb):
    t = np.arange(rows)[:, None]
    s = np.arange(rows)[None, :]
    tri = ((t // c) == (s // c)) & (s <= t)
    t, s = t[:pb, :pb], s[:pb, :pb]
    masks = []
    l = 1
    while l < c:
        masks.append(((t // l) % 2 == 1) & ((s // l) == (t // l) - 1))
        l *= 2
    masks.append(t == s)
    return tri.astype(np.float32), np.stack(masks, 0).astype(np.float32)


def _hgrn_kernel(tri_ref, mask_ref, lbl_ref, q_ref, f_ref, v_ref, og_ref, gn_ref, s0_ref,
                 o_ref, sfin_ref, st_ref, oi_ref, qin_ref, u_ref, dec_ref,
                 *, c, n_chunks, rows, pb, layer, carry):
    t_idx = pl.program_id(2)
    d = HEAD_DIM_HGRN
    n_lv = c.bit_length() - 1
    g_chunks = rows // c
    n_pb = rows // pb
    nt = (((1,), (1,)), ((), ()))
    tn = (((0,), (0,)), ((), ()))

    if carry:
        @pl.when(t_idx == 0)
        def _():
            st_ref[...] = s0_ref[0, 0].T

    logits = lbl_ref[...].astype(F32)
    ex = jnp.exp(logits - jnp.max(logits, axis=0, keepdims=True))
    lb = jnp.sum(ex[:layer + 1], axis=0, keepdims=True) / jnp.sum(ex, axis=0, keepdims=True)
    gain = gn_ref[...].astype(F32)
    shape3 = (rows // SUBLANES, SUBLANES, d)
    sub = lax.broadcasted_iota(jnp.int32, shape3, 1)

    def block_rows(x, first, step, length):
        pieces = [jnp.broadcast_to(x[first + step * p:first + step * p + 1, :], (length, d))
                  for p in range(rows // length)]
        return pieces[0] if len(pieces) == 1 else jnp.concatenate(pieces, axis=0)

    def pair_ref(b, l):
        if l >= SUBLANES:
            return block_rows(b, l - 1, 2 * l, 2 * l)
        b3 = b.reshape(shape3)
        pick = lambda i: jnp.broadcast_to(b3[:, i:i + 1, :], shape3)
        r3 = pick(3) if l == 4 else jnp.where(sub < 4, pick(1), pick(5))
        return r3.reshape(rows, d)

    def odd_half(x_odd, x_even, l):
        if l >= SUBLANES:
            pieces = [(x_odd if p % 2 else x_even)[p * l:(p + 1) * l] for p in range(rows // l)]
            return jnp.concatenate(pieces, axis=0)
        return jnp.where((sub & l) != 0, x_odd.reshape(shape3), x_even.reshape(shape3)).reshape(rows, d)

    def phase_a(bi, carry):
        rsel = pl.ds(pl.multiple_of(bi * rows, rows), rows)
        f = lb + (1.0 - lb) * jax.nn.sigmoid(f_ref[0, rsel, :])
        g = jnp.log(f)
        k = 1.0 - f
        qb = q_ref[0, rsel, :]
        q = qb.astype(F32)
        v = v_ref[0, rsel, :]

        g_hi = g.astype(BF16)
        r1 = g - g_hi.astype(F32)
        g_mid = r1.astype(BF16)
        g_lo = (r1 - g_mid.astype(F32)).astype(BF16)
        b3 = jnp.dot(tri_ref[...], jnp.concatenate([g_hi, g_mid, g_lo], axis=1),
                     preferred_element_type=F32)
        b = b3[:, 0:d] + b3[:, d:2 * d] + b3[:, 2 * d:3 * d]

        def scores(zl, zr, li):
            p = lax.dot_general(zl, zr, nt, preferred_element_type=F32)
            return [mask_ref[li] * p[i * pb:(i + 1) * pb, i * pb:(i + 1) * pb] for i in range(n_pb)]

        a = scores(qb, k.astype(BF16), n_lv)
        z = odd_half(q * f, k, 1).astype(BF16)
        a = [x + y for x, y in zip(a, scores(z, z, 0))]
        for li in range(1, n_lv):
            l = 1 << li
            w = jnp.exp(-jnp.abs(b - pair_ref(b, l)))
            z = (odd_half(q, k, l) * w).astype(BF16)
            a = [x + y for x, y in zip(a, scores(z, z, li))]
        for i in range(n_pb):
            oi_ref[pl.ds(pl.multiple_of(bi * rows + i * pb, pb), pb), :] = jnp.dot(
                a[i].astype(BF16), v[i * pb:(i + 1) * pb], preferred_element_type=F32)

        dec = jnp.exp(b)
        qin_ref[rsel, :] = (q * dec).astype(BF16)
        k_out = (k * jnp.exp(block_rows(b, c - 1, c, c) - b)).astype(BF16)
        for ci in range(g_chunks):
            cr = slice(ci * c, (ci + 1) * c)
            u_ref[bi * g_chunks + ci] = lax.dot_general(v[cr], k_out[cr], tn, preferred_element_type=F32)
            dec_ref[pl.ds(pl.multiple_of((bi * g_chunks + ci) * SUBLANES, SUBLANES), SUBLANES), :] = (
                jnp.broadcast_to(dec[ci * c + c - 1:ci * c + c, :], (SUBLANES, d)))
        return carry

    def phase_bc(bi):
        st = st_ref[...] if carry else None
        outs = []
        for ci in range(g_chunks):
            chunk = bi * g_chunks + ci
            rsel = pl.ds(pl.multiple_of(chunk * c, c), c)
            if not carry:
                st = s0_ref[chunk, 0].T
            outs.append(oi_ref[rsel, :] + lax.dot_general(
                qin_ref[rsel, :], st.astype(BF16), nt, preferred_element_type=F32))
            st = st * dec_ref[pl.ds(pl.multiple_of(chunk * SUBLANES, SUBLANES), 1), :] + u_ref[chunk]
            if not carry:
                sfin_ref[chunk, 0] = st.T
        if carry:
            st_ref[...] = st
        rsel = pl.ds(pl.multiple_of(bi * rows, rows), rows)
        o = outs[0] if g_chunks == 1 else jnp.concatenate(outs, axis=0)
        y = o * lax.rsqrt(jnp.mean(o * o, axis=-1, keepdims=True) + EPS) * gain
        og = og_ref[0, rsel, :].astype(F32)
        o_ref[0, rsel, :] = (y * (og * jax.nn.sigmoid(og))).astype(o_ref.dtype)

    n_batches = n_chunks // g_chunks
    for bi in range(n_batches):
        phase_a(bi, 0)
        if bi > 0:
            phase_bc(bi - 1)
    phase_bc(n_batches - 1)

    if carry:
        @pl.when(t_idx == pl.num_programs(2) - 1)
        def _():
            sfin_ref[0, 0] = st_ref[...].T


def _hgrn(proj3, fpre3, lb_logits, gain, s0, layer, q_col, v_col, og_col):
    bsz, t, _ = proj3.shape
    hw = fpre3.shape[-1]
    d = HEAD_DIM_HGRN
    nh = hw // d
    c = CHUNK if t % CHUNK == 0 else t
    assert c & (c - 1) == 0 and c >= 2 * SUBLANES and t % c == 0
    carry = t > c
    if not carry:
        out, s_fin = _hgrn_call(proj3.reshape(1, bsz * t, -1), fpre3.reshape(1, bsz * t, hw), lb_logits,
                                gain, s0, layer, q_col, v_col, og_col, c=c, carry=False)
        return out.reshape(bsz, t, hw), s_fin
    return _hgrn_call(proj3, fpre3, lb_logits, gain, s0, layer, q_col, v_col, og_col, c=c, carry=True)


def _hgrn_call(proj3, fpre3, lb_logits, gain, s0, layer, q_col, v_col, og_col, *, c, carry):
    bsz, t, _ = proj3.shape
    hw = fpre3.shape[-1]
    d = HEAD_DIM_HGRN
    nh = hw // d
    tt = _tile(t, 4096, c)
    n_chunks = tt // c
    rows = _tile(tt, 2 * LANES, c)
    pb = min(rows, LANES)
    tri, masks = _hgrn_tables(c, rows, pb)
    qb, vb, ob = q_col // d, v_col // d, og_col // d
    kern = functools.partial(_hgrn_kernel, c=c, n_chunks=n_chunks, rows=rows, pb=pb, layer=layer,
                             carry=carry)
    nl = lb_logits.shape[0]
    if carry:
        state_spec = pl.BlockSpec((1, 1, d, d), lambda b, h, i: (b, h, 0, 0))
    else:
        state_spec = pl.BlockSpec((n_chunks, 1, d, d), lambda b, h, i: (i, h, 0, 0))
    return pl.pallas_call(
        kern,
        grid=(bsz, nh, t // tt),
        in_specs=[pl.BlockSpec((rows, rows), lambda b, h, i: (0, 0)),
                  pl.BlockSpec(masks.shape, lambda b, h, i: (0, 0, 0)),
                  pl.BlockSpec((nl, d), lambda b, h, i: (0, h)),
                  pl.BlockSpec((1, tt, d), lambda b, h, i: (b, i, qb + h)),
                  pl.BlockSpec((1, tt, d), lambda b, h, i: (b, i, h)),
                  pl.BlockSpec((1, tt, d), lambda b, h, i: (b, i, vb + h)),
                  pl.BlockSpec((1, tt, d), lambda b, h, i: (b, i, ob + h)),
                  pl.BlockSpec((1, d), lambda b, h, i: (0, h)),
                  state_spec],
        out_specs=[pl.BlockSpec((1, tt, d), lambda b, h, i: (b, i, h)),
                   state_spec],
        out_shape=[jax.ShapeDtypeStruct((bsz, t, hw), BF16),
                   jax.ShapeDtypeStruct(s0.shape, F32)],
        scratch_shapes=[pltpu.VMEM((d, d), F32),
                        pltpu.VMEM((tt, d), F32),
                        pltpu.VMEM((tt, d), BF16),
                        pltpu.VMEM((n_chunks, d, d), F32),
                        pltpu.VMEM((n_chunks * SUBLANES, d), F32)],
        compiler_params=_params("parallel", "parallel", "arbitrary"),
        name="hgrn2",
    )(jnp.asarray(tri, BF16), jnp.asarray(masks, F32), lb_logits.astype(F32),
      proj3, fpre3, proj3, proj3, gain.reshape(1, hw).astype(F32), s0.astype(F32))


CONV_PAD = 8
CONV_BLOCK_ELEMS = 1024 * 1024


def _conv_kernel(cc_ref, ch_ref, cb_ref, w_ref, buf_ref, o_ref, nbuf_ref, u_ref, *, t, k, rb):
    u_ref[CONV_PAD - (k - 1):CONV_PAD, :] = buf_ref[0].astype(F32)
    for r in range(0, t, rb):
        u_ref[CONV_PAD + r:CONV_PAD + r + rb, :] = (
            cc_ref[0, r:r + rb, :].astype(F32) * ch_ref[0, r:r + rb, :].astype(F32))
    w = w_ref[...].astype(F32)
    for r in range(0, t, rb):
        z = w[0:1, :] * u_ref[CONV_PAD - (k - 1) + r:CONV_PAD - (k - 1) + r + rb, :]
        for j in range(1, k):
            s = CONV_PAD - (k - 1) + j + r
            z = z + w[j:j + 1, :] * u_ref[s:s + rb, :]
        o_ref[0, r:r + rb, :] = (cb_ref[0, r:r + rb, :].astype(F32) * z).astype(o_ref.dtype)
    nbuf_ref[0] = u_ref[CONV_PAD + t - (k - 1):CONV_PAD + t, :]


def _conv(proj3, conv_w, buf, ch_col, cb_col, cc_col):
    bsz, t, _ = proj3.shape
    k, cw = conv_w.shape
    assert k - 1 <= CONV_PAD
    d = _tile(cw, max(LANES, CONV_BLOCK_ELEMS // t), LANES)
    rb = _tile(t, max(8, 512 * LANES // d), 8)
    kern = functools.partial(_conv_kernel, t=t, k=k, rb=rb)
    return pl.pallas_call(
        kern,
        grid=(bsz, cw // d),
        in_specs=[pl.BlockSpec((1, t, d), lambda b, j: (b, 0, cc_col // d + j)),
                  pl.BlockSpec((1, t, d), lambda b, j: (b, 0, ch_col // d + j)),
                  pl.BlockSpec((1, t, d), lambda b, j: (b, 0, cb_col // d + j)),
                  pl.BlockSpec((k, d), lambda b, j: (0, j)),
                  pl.BlockSpec((1, k - 1, d), lambda b, j: (b, 0, j))],
        out_specs=[pl.BlockSpec((1, t, d), lambda b, j: (b, 0, j)),
                   pl.BlockSpec((1, k - 1, d), lambda b, j: (b, 0, j))],
        out_shape=[jax.ShapeDtypeStruct((bsz, t, cw), BF16),
                   jax.ShapeDtypeStruct((bsz, k - 1, cw), F32)],
        scratch_shapes=[pltpu.VMEM((CONV_PAD + t, d), F32)],
        compiler_params=_params("parallel", "parallel"),
        name="short_conv",
    )(proj3, proj3, proj3, conv_w.astype(F32), buf.astype(F32))


def _attn_kernel(q_ref, k_ref, v_ref, o_ref, kb_ref, vb_ref, *, scale, sub_rows):
    nt = (((1,), (1,)), ((), ()))

    @pl.when(pl.program_id(2) == 0)
    def _():
        kb_ref[...] = k_ref[0].astype(BF16)
        vb_ref[...] = v_ref[0].astype(BF16)

    subs = [pl.ds(r, sub_rows) for r in range(0, q_ref.shape[1], sub_rows)]
    scores = [lax.dot_general(q_ref[0, r, :], kb_ref[...], nt, preferred_element_type=F32) * scale
              for r in subs]
    probs = []
    for s in scores:
        p = jnp.exp(s - jnp.max(s, axis=-1, keepdims=True))
        probs.append((p / jnp.sum(p, axis=-1, keepdims=True)).astype(BF16))
    for r, p in zip(subs, probs):
        o_ref[0, r, :] = jnp.dot(p, vb_ref[...], preferred_element_type=F32).astype(o_ref.dtype)


def _attention(q3, mk4, mv4, mem_layer, n_heads):
    bsz, t, dm = q3.shape
    n_mem = mk4.shape[2]
    hd = dm // n_heads
    mem_spec = pl.BlockSpec((None, 1, n_mem, hd), lambda b, h, i: (mem_layer, b, 0, h))
    bt = _tile(t, 2048, 8)
    kern = functools.partial(_attn_kernel, scale=float(hd) ** -0.5, sub_rows=_tile(bt, 512, 8))
    return pl.pallas_call(
        kern,
        grid=(bsz, n_heads, t // bt),
        in_specs=[pl.BlockSpec((1, bt, hd), lambda b, h, i: (b, i, h)), mem_spec, mem_spec],
        out_specs=pl.BlockSpec((1, bt, hd), lambda b, h, i: (b, i, h)),
        out_shape=jax.ShapeDtypeStruct((bsz, t, dm), BF16),
        scratch_shapes=[pltpu.VMEM((n_mem, hd), BF16), pltpu.VMEM((n_mem, hd), BF16)],
        compiler_params=_params("parallel", "parallel", "arbitrary"),
        name="cross_attention",
    )(q3, mk4, mv4)


def _layer(x, bsz, t, wts, lb_logits, layer, s0, buf, mem, n_xa_heads):
    m, dm = x.shape
    hw = wts["hg_norm"].shape[0]
    cw = wts["conv_w"].shape[1]
    q_col, v_col, og_col = 0, hw, 2 * hw
    ch_col, cb_col, cc_col = 3 * hw, 3 * hw + cw, 3 * hw + 2 * cw
    ga_col, gb_col = 3 * hw + 3 * cw, 3 * hw + 3 * cw + dm

    h = _rmsnorm(x, wts["norm_mix"], BF16)
    pw = wts["w_in"].shape[1]
    proj_cols = [(0, hw), (2 * hw, pw - 2 * hw)]
    if "f32" in wts:
        names = list(wts["f32"])
        proj, cast = _matmul(h, wts["w_in"], BF16, cols=proj_cols, side=[wts["f32"][n] for n in names])
        wts = {k: v for k, v in wts.items() if k != "f32"} | dict(zip(names, cast))
    else:
        proj = _matmul(h, wts["w_in"], BF16, cols=proj_cols)
    fpre = _matmul(h, wts["w_in"], F32, cols=[(hw, hw)])
    proj3 = proj.reshape(bsz, t, proj.shape[1])
    a_in, s_fin = _hgrn(proj3, fpre.reshape(bsz, t, hw), lb_logits, wts["hg_norm"], s0, layer,
                        q_col, v_col, og_col)
    bz, new_buf = _conv(proj3, wts["conv_w"], buf, ch_col, cb_col, cc_col)
    merged = _matmul_gated(a_in.reshape(m, hw), bz.reshape(m, cw), wts["w_a"], wts["w_b"],
                           proj, ga_col, gb_col)
    x, xg, ss = _matmul(merged, wts["w_o"], F32, bn_pref=512, res=x, norm_gain=wts["norm_xattn"])
    qx = _matmul(xg, wts["w_xq"], BF16, row_ss=ss)
    att = _attention(qx.reshape(bsz, t, dm), *mem, n_xa_heads)
    x, xg, ss = _matmul(att.reshape(m, dm), wts["w_xo"], F32, bn_pref=512, res=x,
                        norm_gain=wts["norm_ffn"])
    act = _matmul_swiglu(xg, ss, wts["w_gate"], wts["w_up"])
    x = _matmul(act, wts["w_down"], F32, bm_pref=512, bn_pref=512, res=x)
    return x, s_fin, new_buf, wts


def kernel(x_prompt, x_sample, cache_mem_k, cache_mem_v, state_hgrn, state_conv, mem_prompt, norm_mix, w_in, lb_logits, hg_norm, conv_w, w_a, w_b, w_o, norm_xattn, norm_mem, w_xq, w_xk, w_xv, w_xo, norm_ffn, w_gate, w_up, w_down, norm_final):
    depth = norm_mix.shape[0]
    bp, tp, dm = x_prompt.shape
    bs, ts, _ = x_sample.shape
    hw = hg_norm.shape[1]
    n_mem = mem_prompt.shape[1]
    n_xa_heads = cache_mem_k.shape[3]
    nh, dk, dv = state_hgrn.shape[2:]
    assert dk == HEAD_DIM_HGRN and dv == HEAD_DIM_HGRN and nh * dk == hw

    xp = x_prompt.reshape(bp * tp, dm)
    xs = x_sample.reshape(bs * ts, dm)
    outs = {k: [] for k in ("mk", "mv", "sp", "cp", "ss", "cs")}
    for l in range(depth):
        w_in_l = w_in[l]
        wts = {
            "norm_mix": norm_mix[l], "hg_norm": hg_norm[l], "conv_w": conv_w[l],
            "norm_xattn": norm_xattn[l], "norm_ffn": norm_ffn[l],
            "w_in": w_in_l.astype(BF16),
            "f32": {"w_a": w_a[l], "w_b": w_b[l], "w_o": w_o[l], "w_xq": w_xq[l], "w_xo": w_xo[l],
                    "w_gate": w_gate[l], "w_up": w_up[l], "w_down": w_down[l]},
        }
        mem_n = _rmsnorm(mem_prompt.reshape(bp * n_mem, dm), norm_mem[l], BF16)
        mk_p = _matmul(mem_n, w_xk[l], F32)
        mv_p = _matmul(mem_n, w_xv[l], F32)
        s0 = jnp.zeros((bp, nh, dk, dv), F32)
        buf0 = jnp.zeros((bp, conv_w.shape[1] - 1, conv_w.shape[2]), F32)
        mem_p = (mk_p.reshape(1, bp, n_mem, dm), mv_p.reshape(1, bp, n_mem, dm), 0)
        mem_s = (cache_mem_k.reshape(depth, bs, n_mem, dm), cache_mem_v.reshape(depth, bs, n_mem, dm), l)
        xp, s_p, buf_p, wts = _layer(xp, bp, tp, wts, lb_logits, l, s0, buf0, mem_p, n_xa_heads)
        xs, s_s, buf_s, _ = _layer(xs, bs, ts, wts, lb_logits, l, state_hgrn[l], state_conv[l],
                                   mem_s, n_xa_heads)
        outs["mk"].append(mk_p.reshape(bp, n_mem, n_xa_heads, dm // n_xa_heads))
        outs["mv"].append(mv_p.reshape(bp, n_mem, n_xa_heads, dm // n_xa_heads))
        outs["sp"].append(s_p)
        outs["cp"].append(buf_p)
        outs["ss"].append(s_s)
        outs["cs"].append(buf_s)
    y_prompt = _rmsnorm(xp, norm_final, F32).reshape(bp, tp, dm)
    y_sample = _rmsnorm(xs, norm_final, F32).reshape(bs, ts, dm)
    return (y_prompt, y_sample, jnp.stack(outs["mk"]), jnp.stack(outs["mv"]), jnp.stack(outs["sp"]),
            jnp.stack(outs["cp"]), jnp.stack(outs["ss"]), jnp.stack(outs["cs"]))
```

```python
import functools

import numpy as np
import jax
import jax.numpy as jnp
from jax import lax
from jax.experimental import pallas as pl
from jax.experimental.pallas import tpu as pltpu

EPS = 1e-6
LANES = 128
SUBLANES = 8
HEAD_DIM_HGRN = 128
CHUNK = 64
VMEM_LIMIT_BYTES = 60 * 1024 * 1024
BF16_SUBLANES = 16
BF16 = jnp.bfloat16
F32 = jnp.float32


def _params(*semantics):
    return pltpu.CompilerParams(dimension_semantics=semantics, vmem_limit_bytes=VMEM_LIMIT_BYTES)


def _tile(n, pref, mult):
    if n <= pref:
        return n
    t = (pref // mult) * mult
    while t >= mult:
        if n % t == 0:
            return t
        t -= mult
    raise ValueError(f"no tile for {n} (pref {pref}, mult {mult})")


def _rmsnorm_kernel(x_ref, g_ref, o_ref):
    x = x_ref[...].astype(F32)
    y = x * lax.rsqrt(jnp.mean(x * x, axis=-1, keepdims=True) + EPS)
    o_ref[...] = (y * g_ref[...]).astype(o_ref.dtype)


def _rmsnorm(x, g, out_dtype):
    m, d = x.shape
    bm = _tile(m, 512, 8)
    return pl.pallas_call(
        _rmsnorm_kernel,
        grid=(m // bm,),
        in_specs=[pl.BlockSpec((bm, d), lambda i: (i, 0)),
                  pl.BlockSpec((1, d), lambda i: (0, 0))],
        out_specs=pl.BlockSpec((bm, d), lambda i: (i, 0)),
        out_shape=jax.ShapeDtypeStruct((m, d), out_dtype),
        compiler_params=_params("parallel"),
        name="rmsnorm",
    )(x, g.reshape(1, d).astype(F32))


def _row_scale(rs_ref, n_lanes):
    return pltpu.repeat(rs_ref[...], n_lanes // LANES, axis=1)


MM_SUB_ROWS = 512


def _row_blocks(bm):
    sub = MM_SUB_ROWS if bm % MM_SUB_ROWS == 0 else bm
    return [pl.ds(r, sub) for r in range(0, bm, sub)]


def _mm_kernel(*refs, has_res, has_scale, has_gain, n_side, out_width):
    refs = list(refs)
    x_ref, w_ref = refs.pop(0), refs.pop(0)
    r_ref = refs.pop(0) if has_res else None
    ss_in_ref = refs.pop(0) if has_scale else None
    g_ref = refs.pop(0) if has_gain else None
    src = [refs.pop(0) for _ in range(n_side)]
    o_ref = refs.pop(0)
    og_ref, ss_ref = (refs.pop(0), refs.pop(0)) if has_gain else (None, None)
    dst = refs
    if has_gain:
        @pl.when(pl.program_id(1) == 0)
        def _():
            ss_ref[...] = jnp.zeros_like(ss_ref)

    w = w_ref[...].astype(BF16)
    for rows in _row_blocks(x_ref.shape[0]):
        acc = jnp.dot(x_ref[rows, :], w, preferred_element_type=F32)
        if has_scale:
            acc = acc * _row_scale(ss_in_ref.at[rows], acc.shape[1])
        if has_res:
            acc = r_ref[rows, :] + acc
        o_ref[rows, :] = acc.astype(o_ref.dtype)
        if has_gain:
            og_ref[rows, :] = (acc * g_ref[...]).astype(og_ref.dtype)
            sq = acc * acc
            ss_ref[rows, :] += functools.reduce(
                jnp.add, [sq[:, c:c + LANES] for c in range(0, sq.shape[1], LANES)])

    if has_gain:
        @pl.when(pl.program_id(1) == pl.num_programs(1) - 1)
        def _():
            total = jnp.sum(ss_ref[...], axis=-1, keepdims=True)
            ss_ref[...] = jnp.broadcast_to(lax.rsqrt(total * (1.0 / out_width) + EPS), ss_ref.shape)
    for src_ref, dst_ref in zip(src, dst):
        dst_ref[...] = src_ref[...].astype(BF16)


def _mm_gated_kernel(a_ref, b_ref, wa_ref, wb_ref, ga_ref, gb_ref, o_ref):
    for rows in _row_blocks(a_ref.shape[0]):
        a = jnp.dot(a_ref[rows, :], wa_ref[...], preferred_element_type=F32)
        b = jnp.dot(b_ref[rows, :], wb_ref[...], preferred_element_type=F32)
        ga = jax.nn.sigmoid(ga_ref[rows, :].astype(F32))
        gb = jax.nn.sigmoid(gb_ref[rows, :].astype(F32))
        o_ref[rows, :] = (ga * a + gb * b).astype(o_ref.dtype)


def _mm_swiglu_kernel(x_ref, wg_ref, wu_ref, ss_ref, o_ref):
    for rows in _row_blocks(x_ref.shape[0]):
        x = x_ref[rows, :]
        scale = _row_scale(ss_ref.at[rows], wg_ref.shape[1])
        g = jnp.dot(x, wg_ref[...], preferred_element_type=F32) * scale
        u = jnp.dot(x, wu_ref[...], preferred_element_type=F32) * scale
        o_ref[rows, :] = (g * jax.nn.sigmoid(g) * u).astype(o_ref.dtype)


def _matmul(x, w, out_dtype, *, bm_pref=1024, bn_pref=1024, res=None, cols=None, side=(),
            row_ss=None, norm_gain=None, cols_outer=False):
    m, k = x.shape
    cols = cols or [(0, w.shape[1])]
    n = sum(width for _, width in cols)
    bm = _tile(m, bm_pref, 8)
    bn = _tile(int(functools.reduce(np.gcd, [v for seg in cols for v in seg if v])), bn_pref, LANES)
    steps, out_blk = [], 0
    for start, width in cols:
        steps.append((out_blk, start // bn - out_blk))
        out_blk += width // bn

    def w_block(j):
        blk = j + steps[0][1]
        for (first, shift), (_, prev) in zip(steps[1:], steps[:-1]):
            blk = blk + jnp.where(j >= first, shift - prev, 0)
        return blk

    ni, nj = m // bm, n // bn
    if cols_outer:
        assert not side and norm_gain is None and row_ss is None
        in_specs = [pl.BlockSpec((bm, k), lambda j, i: (i, 0)),
                    pl.BlockSpec((k, bn), lambda j, i: (0, w_block(j)))]
        args = [x, w]
        if res is not None:
            in_specs.append(pl.BlockSpec((bm, bn), lambda j, i: (i, j)))
            args.append(res)
        return pl.pallas_call(
            functools.partial(_mm_kernel, has_res=res is not None, has_scale=False, has_gain=False,
                              n_side=0, out_width=n),
            grid=(nj, ni),
            in_specs=in_specs,
            out_specs=[pl.BlockSpec((bm, bn), lambda j, i: (i, j))],
            out_shape=[jax.ShapeDtypeStruct((m, n), out_dtype)],
            compiler_params=_params("parallel", "parallel"),
            name="matmul_res" if res is not None else "matmul",
        )(*args)[0]
    in_specs = [pl.BlockSpec((bm, k), lambda i, j: (i, 0)),
                pl.BlockSpec((k, bn), lambda i, j: (0, w_block(j)))]
    args = [x, w]
    if res is not None:
        in_specs.append(pl.BlockSpec((bm, bn), lambda i, j: (i, j)))
        args.append(res)
    ss_spec = pl.BlockSpec((bm, LANES), lambda i, j: (i, 0))
    if row_ss is not None:
        in_specs.append(ss_spec)
        args.append(row_ss)
    out_specs = [pl.BlockSpec((bm, bn), lambda i, j: (i, j))]
    out_shape = [jax.ShapeDtypeStruct((m, n), out_dtype)]
    if norm_gain is not None:
        assert not side and cols == [(0, w.shape[1])]
        in_specs.append(pl.BlockSpec((1, bn), lambda i, j: (0, j)))
        args.append(norm_gain.reshape(1, n).astype(F32))
        out_specs += [pl.BlockSpec((bm, bn), lambda i, j: (i, j)), ss_spec]
        out_shape += [jax.ShapeDtypeStruct((m, n), BF16), jax.ShapeDtypeStruct((m, LANES), F32)]
    for sw in side:
        rows, width = sw.shape
        n_blocks = max(nb for nb in range(1, ni * nj + 1)
                       if rows % nb == 0 and (rows // nb) % BF16_SUBLANES == 0)
        spec = pl.BlockSpec((rows // n_blocks, width),
                            lambda i, j, n_blocks=n_blocks: (jnp.minimum(i * nj + j, n_blocks - 1), 0))
        in_specs.append(spec)
        args.append(sw)
        out_specs.append(spec)
        out_shape.append(jax.ShapeDtypeStruct((rows, width), BF16))
    outs = pl.pallas_call(
        functools.partial(_mm_kernel, has_res=res is not None, has_scale=row_ss is not None,
                          has_gain=norm_gain is not None, n_side=len(side), out_width=n),
        grid=(ni, nj),
        in_specs=in_specs,
        out_specs=out_specs,
        out_shape=out_shape,
        compiler_params=(_params("arbitrary", "arbitrary") if side else
                         _params("parallel", "arbitrary") if norm_gain is not None else
                         _params("parallel", "parallel")),
        name="matmul_res" if res is not None else "matmul",
    )(*args)
    if side:
        return outs[0], list(outs[1:])
    return tuple(outs) if norm_gain is not None else outs[0]


def _matmul_gated(a, b, wa, wb, proj, ga_col, gb_col):
    m, ka = a.shape
    kb = b.shape[1]
    n = wa.shape[1]
    bm = _tile(m, 1024, 8)
    bn = _tile(n, 1024, LANES)
    assert ga_col % bn == 0 and gb_col % bn == 0
    ga_blk, gb_blk = ga_col // bn, gb_col // bn
    return pl.pallas_call(
        _mm_gated_kernel,
        grid=(m // bm, n // bn),
        in_specs=[pl.BlockSpec((bm, ka), lambda i, j: (i, 0)),
                  pl.BlockSpec((bm, kb), lambda i, j: (i, 0)),
                  pl.BlockSpec((ka, bn), lambda i, j: (0, j)),
                  pl.BlockSpec((kb, bn), lambda i, j: (0, j)),
                  pl.BlockSpec((bm, bn), lambda i, j: (i, ga_blk + j)),
                  pl.BlockSpec((bm, bn), lambda i, j: (i, gb_blk + j))],
        out_specs=pl.BlockSpec((bm, bn), lambda i, j: (i, j)),
        out_shape=jax.ShapeDtypeStruct((m, n), BF16),
        compiler_params=_params("parallel", "parallel"),
        name="matmul_gated",
    )(a, b, wa, wb, proj, proj)


def _matmul_swiglu(x, row_ss, wg, wu):
    m, k = x.shape
    n = wg.shape[1]
    bm = _tile(m, 2048, 8)
    bn = _tile(n, 256, LANES)
    return pl.pallas_call(
        _mm_swiglu_kernel,
        grid=(m // bm, n // bn),
        in_specs=[pl.BlockSpec((bm, k), lambda i, j: (i, 0)),
                  pl.BlockSpec((k, bn), lambda i, j: (0, j)),
                  pl.BlockSpec((k, bn), lambda i, j: (0, j)),
                  pl.BlockSpec((bm, LANES), lambda i, j: (i, 0))],
        out_specs=pl.BlockSpec((bm, bn), lambda i, j: (i, j)),
        out_shape=jax.ShapeDtypeStruct((m, n), BF16),
        compiler_params=_params("parallel", "parallel"),
        name="matmul_swiglu",
    )(x, wg, wu, row_ss)


def _hgrn_tables(c, rows, pb):
    t = np.arange(rows)[:, None]
    s = np.arange(rows)[None, :]
    tri = ((t // c) == (s // c)) & (s <= t)
    t, s = t[:pb, :pb], s[:pb, :pb]
    masks = []
    l = 1
    while l < c:
        masks.append(((t // l) % 2 == 1) & ((s // l) == (t // l) - 1))
        l *= 2
    masks.append(t == s)
    return tri.astype(np.float32), np.stack(masks, 0).astype(np.float32)


def _hgrn_kernel(tri_ref, mask_ref, lbl_ref, q_ref, f_ref, v_ref, og_ref, gn_ref, s0_ref,
                 o_ref, sfin_ref, st_ref, oi_ref, qin_ref, u_ref, dec_ref,
                 *, c, n_chunks, rows, pb, layer, carry):
    t_idx = pl.program_id(2)
    d = HEAD_DIM_HGRN
    n_lv = c.bit_length() - 1
    g_chunks = rows // c
    n_pb = rows // pb
    nt = (((1,), (1,)), ((), ()))
    tn = (((0,), (0,)), ((), ()))

    if carry:
        @pl.when(t_idx == 0)
        def _():
            st_ref[...] = s0_ref[0, 0].T

    logits = lbl_ref[...].astype(F32)
    ex = jnp.exp(logits - jnp.max(logits, axis=0, keepdims=True))
    lb = jnp.sum(ex[:layer + 1], axis=0, keepdims=True) / jnp.sum(ex, axis=0, keepdims=True)
    gain = gn_ref[...].astype(F32)
    shape3 = (rows // SUBLANES, SUBLANES, d)
    sub = lax.broadcasted_iota(jnp.int32, shape3, 1)

    def block_rows(x, first, step, length):
        pieces = [jnp.broadcast_to(x[first + step * p:first + step * p + 1, :], (length, d))
                  for p in range(rows // length)]
        return pieces[0] if len(pieces) == 1 else jnp.concatenate(pieces, axis=0)

    def pair_ref(b, l):
        if l >= SUBLANES:
            return block_rows(b, l - 1, 2 * l, 2 * l)
        b3 = b.reshape(shape3)
        pick = lambda i: jnp.broadcast_to(b3[:, i:i + 1, :], shape3)
        r3 = pick(3) if l == 4 else jnp.where(sub < 4, pick(1), pick(5))
        return r3.reshape(rows, d)

    def odd_half(x_odd, x_even, l):
        if l >= SUBLANES:
            pieces = [(x_odd if p % 2 else x_even)[p * l:(p + 1) * l] for p in range(rows // l)]
            return jnp.concatenate(pieces, axis=0)
        return jnp.where((sub & l) != 0, x_odd.reshape(shape3), x_even.reshape(shape3)).reshape(rows, d)

    def phase_a(bi, carry):
        rsel = pl.ds(pl.multiple_of(bi * rows, rows), rows)
        f = lb + (1.0 - lb) * jax.nn.sigmoid(f_ref[0, rsel, :])
        g = jnp.log(f)
        k = 1.0 - f
        qb = q_ref[0, rsel, :]
        q = qb.astype(F32)
        v = v_ref[0, rsel, :]

        g_hi = g.astype(BF16)
        r1 = g - g_hi.astype(F32)
        g_mid = r1.astype(BF16)
        g_lo = (r1 - g_mid.astype(F32)).astype(BF16)
        b3 = jnp.dot(tri_ref[...], jnp.concatenate([g_hi, g_mid, g_lo], axis=1),
                     preferred_element_type=F32)
        b = b3[:, 0:d] + b3[:, d:2 * d] + b3[:, 2 * d:3 * d]

        def scores(zl, zr, li):
            p = lax.dot_general(zl, zr, nt, preferred_element_type=F32)
            return [mask_ref[li] * p[i * pb:(i + 1) * pb, i * pb:(i + 1) * pb] for i in range(n_pb)]

        a = scores(qb, k.astype(BF16), n_lv)
        z = odd_half(q * f, k, 1).astype(BF16)
        a = [x + y for x, y in zip(a, scores(z, z, 0))]
        for li in range(1, n_lv):
            l = 1 << li
            w = jnp.exp(-jnp.abs(b - pair_ref(b, l)))
            z = (odd_half(q, k, l) * w).astype(BF16)
            a = [x + y for x, y in zip(a, scores(z, z, li))]
        for i in range(n_pb):
            oi_ref[pl.ds(pl.multiple_of(bi * rows + i * pb, pb), pb), :] = jnp.dot(
                a[i].astype(BF16), v[i * pb:(i + 1) * pb], preferred_element_type=F32)

        dec = jnp.exp(b)
        qin_ref[rsel, :] = (q * dec).astype(BF16)
        k_out = (k * jnp.exp(block_rows(b, c - 1, c, c) - b)).astype(BF16)
        for ci in range(g_chunks):
            cr = slice(ci * c, (ci + 1) * c)
            u_ref[bi * g_chunks + ci] = lax.dot_general(v[cr], k_out[cr], tn, preferred_element_type=F32)
            dec_ref[pl.ds(pl.multiple_of((bi * g_chunks + ci) * SUBLANES, SUBLANES), SUBLANES), :] = (
                jnp.broadcast_to(dec[ci * c + c - 1:ci * c + c, :], (SUBLANES, d)))
        return carry

    def phase_bc(bi):
        st = st_ref[...] if carry else None
        outs = []
        for ci in range(g_chunks):
            chunk = bi * g_chunks + ci
            rsel = pl.ds(pl.multiple_of(chunk * c, c), c)
            if not carry:
                st = s0_ref[chunk, 0].T
            outs.append(oi_ref[rsel, :] + lax.dot_general(
                qin_ref[rsel, :], st.astype(BF16), nt, preferred_element_type=F32))
            st = st * dec_ref[pl.ds(pl.multiple_of(chunk * SUBLANES, SUBLANES), 1), :] + u_ref[chunk]
            if not carry:
                sfin_ref[chunk, 0] = st.T
        if carry:
            st_ref[...] = st
        rsel = pl.ds(pl.multiple_of(bi * rows, rows), rows)
        o = outs[0] if g_chunks == 1 else jnp.concatenate(outs, axis=0)
        y = o * lax.rsqrt(jnp.mean(o * o, axis=-1, keepdims=True) + EPS) * gain
        og = og_ref[0, rsel, :].astype(F32)
        o_ref[0, rsel, :] = (y * (og * jax.nn.sigmoid(og))).astype(o_ref.dtype)

    n_batches = n_chunks // g_chunks
    for bi in range(n_batches):
        phase_a(bi, 0)
        if bi > 0:
            phase_bc(bi - 1)
    phase_bc(n_batches - 1)

    if carry:
        @pl.when(t_idx == pl.num_programs(2) - 1)
        def _():
            sfin_ref[0, 0] = st_ref[...].T


def _hgrn(proj3, fpre3, lb_logits, gain, s0, layer, q_col, v_col, og_col):
    bsz, t, _ = proj3.shape
    hw = fpre3.shape[-1]
    d = HEAD_DIM_HGRN
    nh = hw // d
    c = CHUNK if t % CHUNK == 0 else t
    assert c & (c - 1) == 0 and c >= 2 * SUBLANES and t % c == 0
    carry = t > c
    if not carry:
        out, s_fin = _hgrn_call(proj3.reshape(1, bsz * t, -1), fpre3.reshape(1, bsz * t, hw), lb_logits,
                                gain, s0, layer, q_col, v_col, og_col, c=c, carry=False)
        return out.reshape(bsz, t, hw), s_fin
    return _hgrn_call(proj3, fpre3, lb_logits, gain, s0, layer, q_col, v_col, og_col, c=c, carry=True)


def _hgrn_call(proj3, fpre3, lb_logits, gain, s0, layer, q_col, v_col, og_col, *, c, carry):
    bsz, t, _ = proj3.shape
    hw = fpre3.shape[-1]
    d = HEAD_DIM_HGRN
    nh = hw // d
    tt = _tile(t, 4096, c)
    n_chunks = tt // c
    rows = _tile(tt, 2 * LANES, c)
    pb = min(rows, LANES)
    tri, masks = _hgrn_tables(c, rows, pb)
    qb, vb, ob = q_col // d, v_col // d, og_col // d
    kern = functools.partial(_hgrn_kernel, c=c, n_chunks=n_chunks, rows=rows, pb=pb, layer=layer,
                             carry=carry)
    nl = lb_logits.shape[0]
    if carry:
        state_spec = pl.BlockSpec((1, 1, d, d), lambda b, h, i: (b, h, 0, 0))
    else:
        state_spec = pl.BlockSpec((n_chunks, 1, d, d), lambda b, h, i: (i, h, 0, 0))
    return pl.pallas_call(
        kern,
        grid=(bsz, nh, t // tt),
        in_specs=[pl.BlockSpec((rows, rows), lambda b, h, i: (0, 0)),
                  pl.BlockSpec(masks.shape, lambda b, h, i: (0, 0, 0)),
                  pl.BlockSpec((nl, d), lambda b, h, i: (0, h)),
                  pl.BlockSpec((1, tt, d), lambda b, h, i: (b, i, qb + h)),
                  pl.BlockSpec((1, tt, d), lambda b, h, i: (b, i, h)),
                  pl.BlockSpec((1, tt, d), lambda b, h, i: (b, i, vb + h)),
                  pl.BlockSpec((1, tt, d), lambda b, h, i: (b, i, ob + h)),
                  pl.BlockSpec((1, d), lambda b, h, i: (0, h)),
                  state_spec],
        out_specs=[pl.BlockSpec((1, tt, d), lambda b, h, i: (b, i, h)),
                   state_spec],
        out_shape=[jax.ShapeDtypeStruct((bsz, t, hw), BF16),
                   jax.ShapeDtypeStruct(s0.shape, F32)],
        scratch_shapes=[pltpu.VMEM((d, d), F32),
                        pltpu.VMEM((tt, d), F32),
                        pltpu.VMEM((tt, d), BF16),
                        pltpu.VMEM((n_chunks, d, d), F32),
                        pltpu.VMEM((n_chunks * SUBLANES, d), F32)],
        compiler_params=_params("parallel", "parallel", "arbitrary"),
        name="hgrn2",
    )(jnp.asarray(tri, BF16), jnp.asarray(masks, F32), lb_logits.astype(F32),
      proj3, fpre3, proj3, proj3, gain.reshape(1, hw).astype(F32), s0.astype(F32))


CONV_PAD = 8
CONV_BLOCK_ELEMS = 1024 * 1024


def _conv_kernel(cc_ref, ch_ref, cb_ref, w_ref, buf_ref, o_ref, nbuf_ref, u_ref, *, t, k, rb):
    u_ref[CONV_PAD - (k - 1):CONV_PAD, :] = buf_ref[0].astype(F32)
    for r in range(0, t, rb):
        u_ref[CONV_PAD + r:CONV_PAD + r + rb, :] = (
            cc_ref[0, r:r + rb, :].astype(F32) * ch_ref[0, r:r + rb, :].astype(F32))
    w = w_ref[...].astype(F32)
    for r in range(0, t, rb):
        z = w[0:1, :] * u_ref[CONV_PAD - (k - 1) + r:CONV_PAD - (k - 1) + r + rb, :]
        for j in range(1, k):
            s = CONV_PAD - (k - 1) + j + r
            z = z + w[j:j + 1, :] * u_ref[s:s + rb, :]
        o_ref[0, r:r + rb, :] = (cb_ref[0, r:r + rb, :].astype(F32) * z).astype(o_ref.dtype)
    nbuf_ref[0] = u_ref[CONV_PAD + t - (k - 1):CONV_PAD + t, :]


def _conv(proj3, conv_w, buf, ch_col, cb_col, cc_col):
    bsz, t, _ = proj3.shape
    k, cw = conv_w.shape
    assert k - 1 <= CONV_PAD
    d = _tile(cw, max(LANES, CONV_BLOCK_ELEMS // t), LANES)
    rb = _tile(t, max(8, 512 * LANES // d), 8)
    kern = functools.partial(_conv_kernel, t=t, k=k, rb=rb)
    return pl.pallas_call(
        kern,
        grid=(bsz, cw // d),
        in_specs=[pl.BlockSpec((1, t, d), lambda b, j: (b, 0, cc_col // d + j)),
                  pl.BlockSpec((1, t, d), lambda b, j: (b, 0, ch_col // d + j)),
                  pl.BlockSpec((1, t, d), lambda b, j: (b, 0, cb_col // d + j)),
                  pl.BlockSpec((k, d), lambda b, j: (0, j)),
                  pl.BlockSpec((1, k - 1, d), lambda b, j: (b, 0, j))],
        out_specs=[pl.BlockSpec((1, t, d), lambda b, j: (b, 0, j)),
                   pl.BlockSpec((1, k - 1, d), lambda b, j: (b, 0, j))],
        out_shape=[jax.ShapeDtypeStruct((bsz, t, cw), BF16),
                   jax.ShapeDtypeStruct((bsz, k - 1, cw), F32)],
        scratch_shapes=[pltpu.VMEM((CONV_PAD + t, d), F32)],
        compiler_params=_params("parallel", "parallel"),
        name="short_conv",
    )(proj3, proj3, proj3, conv_w.astype(F32), buf.astype(F32))


def _attn_kernel(q_ref, k_ref, v_ref, o_ref, kb_ref, vb_ref, *, scale, sub_rows):
    nt = (((1,), (1,)), ((), ()))

    @pl.when(pl.program_id(2) == 0)
    def _():
        kb_ref[...] = k_ref[0].astype(BF16)
        vb_ref[...] = v_ref[0].astype(BF16)

    subs = [pl.ds(r, sub_rows) for r in range(0, q_ref.shape[1], sub_rows)]
    scores = [lax.dot_general(q_ref[0, r, :], kb_ref[...], nt, preferred_element_type=F32) * scale
              for r in subs]
    probs = []
    for s in scores:
        p = jnp.exp(s - jnp.max(s, axis=-1, keepdims=True))
        probs.append((p / jnp.sum(p, axis=-1, keepdims=True)).astype(BF16))
    for r, p in zip(subs, probs):
        o_ref[0, r, :] = jnp.dot(p, vb_ref[...], preferred_element_type=F32).astype(o_ref.dtype)


def _attention(q3, mk4, mv4, mem_layer, n_heads):
    bsz, t, dm = q3.shape
    n_mem = mk4.shape[2]
    hd = dm // n_heads
    mem_spec = pl.BlockSpec((None, 1, n_mem, hd), lambda b, h, i: (mem_layer, b, 0, h))
    bt = _tile(t, 2048, 8)
    kern = functools.partial(_attn_kernel, scale=float(hd) ** -0.5, sub_rows=_tile(bt, 512, 8))
    return pl.pallas_call(
        kern,
        grid=(bsz, n_heads, t // bt),
        in_specs=[pl.BlockSpec((1, bt, hd), lambda b, h, i: (b, i, h)), mem_spec, mem_spec],
        out_specs=pl.BlockSpec((1, bt, hd), lambda b, h, i: (b, i, h)),
        out_shape=jax.ShapeDtypeStruct((bsz, t, dm), BF16),
        scratch_shapes=[pltpu.VMEM((n_mem, hd), BF16), pltpu.VMEM((n_mem, hd), BF16)],
        compiler_params=_params("parallel", "parallel", "arbitrary"),
        name="cross_attention",
    )(q3, mk4, mv4)


def _layer(x, bsz, t, wts, lb_logits, layer, s0, buf, mem, n_xa_heads):
    m, dm = x.shape
    hw = wts["hg_norm"].shape[0]
    cw = wts["conv_w"].shape[1]
    q_col, v_col, og_col = 0, hw, 2 * hw
    ch_col, cb_col, cc_col = 3 * hw, 3 * hw + cw, 3 * hw + 2 * cw
    ga_col, gb_col = 3 * hw + 3 * cw, 3 * hw + 3 * cw + dm

    h = _rmsnorm(x, wts["norm_mix"], BF16)
    pw = wts["w_in"].shape[1]
    proj_cols = [(0, hw), (2 * hw, pw - 2 * hw)]
    if "f32" in wts:
        names = list(wts["f32"])
        proj, cast = _matmul(h, wts["w_in"], BF16, cols=proj_cols, side=[wts["f32"][n] for n in names])
        wts = {k: v for k, v in wts.items() if k != "f32"} | dict(zip(names, cast))
    else:
        proj = _matmul(h, wts["w_in"], BF16, cols=proj_cols)
    fpre = _matmul(h, wts["w_in"], F32, cols=[(hw, hw)])
    proj3 = proj.reshape(bsz, t, proj.shape[1])
    a_in, s_fin = _hgrn(proj3, fpre.reshape(bsz, t, hw), lb_logits, wts["hg_norm"], s0, layer,
                        q_col, v_col, og_col)
    bz, new_buf = _conv(proj3, wts["conv_w"], buf, ch_col, cb_col, cc_col)
    merged = _matmul_gated(a_in.reshape(m, hw), bz.reshape(m, cw), wts["w_a"], wts["w_b"],
                           proj, ga_col, gb_col)
    x, xg, ss = _matmul(merged, wts["w_o"], F32, bn_pref=512, res=x, norm_gain=wts["norm_xattn"])
    qx = _matmul(xg, wts["w_xq"], BF16, row_ss=ss)
    att = _attention(qx.reshape(bsz, t, dm), *mem, n_xa_heads)
    x, xg, ss = _matmul(att.reshape(m, dm), wts["w_xo"], F32, bn_pref=512, res=x,
                        norm_gain=wts["norm_ffn"])
    act = _matmul_swiglu(xg, ss, wts["w_gate"], wts["w_up"])
    x = _matmul(act, wts["w_down"], F32, bm_pref=512, bn_pref=512, res=x, cols_outer=True)
    return x, s_fin, new_buf, wts


def kernel(x_prompt, x_sample, cache_mem_k, cache_mem_v, state_hgrn, state_conv, mem_prompt, norm_mix, w_in, lb_logits, hg_norm, conv_w, w_a, w_b, w_o, norm_xattn, norm_mem, w_xq, w_xk, w_xv, w_xo, norm_ffn, w_gate, w_up, w_down, norm_final):
    depth = norm_mix.shape[0]
    bp, tp, dm = x_prompt.shape
    bs, ts, _ = x_sample.shape
    hw = hg_norm.shape[1]
    n_mem = mem_prompt.shape[1]
    n_xa_heads = cache_mem_k.shape[3]
    nh, dk, dv = state_hgrn.shape[2:]
    assert dk == HEAD_DIM_HGRN and dv == HEAD_DIM_HGRN and nh * dk == hw

    xp = x_prompt.reshape(bp * tp, dm)
    xs = x_sample.reshape(bs * ts, dm)
    outs = {k: [] for k in ("mk", "mv", "sp", "cp", "ss", "cs")}
    for l in range(depth):
        w_in_l = w_in[l]
        wts = {
            "norm_mix": norm_mix[l], "hg_norm": hg_norm[l], "conv_w": conv_w[l],
            "norm_xattn": norm_xattn[l], "norm_ffn": norm_ffn[l],
            "w_in": w_in_l.astype(BF16),
            "f32": {"w_a": w_a[l], "w_b": w_b[l], "w_o": w_o[l], "w_xq": w_xq[l], "w_xo": w_xo[l],
                    "w_gate": w_gate[l], "w_up": w_up[l], "w_down": w_down[l]},
        }
        mem_n = _rmsnorm(mem_prompt.reshape(bp * n_mem, dm), norm_mem[l], BF16)
        mk_p = _matmul(mem_n, w_xk[l], F32)
        mv_p = _matmul(mem_n, w_xv[l], F32)
        s0 = jnp.zeros((bp, nh, dk, dv), F32)
        buf0 = jnp.zeros((bp, conv_w.shape[1] - 1, conv_w.shape[2]), F32)
        mem_p = (mk_p.reshape(1, bp, n_mem, dm), mv_p.reshape(1, bp, n_mem, dm), 0)
        mem_s = (cache_mem_k.reshape(depth, bs, n_mem, dm), cache_mem_v.reshape(depth, bs, n_mem, dm), l)
        xp, s_p, buf_p, wts = _layer(xp, bp, tp, wts, lb_logits, l, s0, buf0, mem_p, n_xa_heads)
        xs, s_s, buf_s, _ = _layer(xs, bs, ts, wts, lb_logits, l, state_hgrn[l], state_conv[l],
                                   mem_s, n_xa_heads)
        outs["mk"].append(mk_p.reshape(bp, n_mem, n_xa_heads, dm // n_xa_heads))
        outs["mv"].append(mv_p.reshape(bp, n_mem, n_xa_heads, dm // n_xa_heads))
        outs["sp"].append(s_p)
        outs["cp"].append(buf_p)
        outs["ss"].append(s_s)
        outs["cs"].append(buf_s)
    y_prompt = _rmsnorm(xp, norm_final, F32).reshape(bp, tp, dm)
    y_sample = _rmsnorm(xs, norm_final, F32).reshape(bs, ts, dm)
    return (y_prompt, y_sample, jnp.stack(outs["mk"]), jnp.stack(outs["mv"]), jnp.stack(outs["sp"]),
            jnp.stack(outs["cp"]), jnp.stack(outs["ss"]), jnp.stack(outs["cs"]))
```

```python
import functools

import numpy as np
import jax
import jax.numpy as jnp
from jax import lax
from jax.experimental import pallas as pl
from jax.experimental.pallas import tpu as pltpu

EPS = 1e-6
LANES = 128
SUBLANES = 8
HEAD_DIM_HGRN = 128
CHUNK = 64
VMEM_LIMIT_BYTES = 60 * 1024 * 1024
BF16_SUBLANES = 16
BF16 = jnp.bfloat16
F32 = jnp.float32


def _params(*semantics):
    return pltpu.CompilerParams(dimension_semantics=semantics, vmem_limit_bytes=VMEM_LIMIT_BYTES)


def _tile(n, pref, mult):
    if n <= pref:
        return n
    t = (pref // mult) * mult
    while t >= mult:
        if n % t == 0:
            return t
        t -= mult
    raise ValueError(f"no tile for {n} (pref {pref}, mult {mult})")


def _rmsnorm_kernel(x_ref, g_ref, o_ref):
    x = x_ref[...].astype(F32)
    y = x * lax.rsqrt(jnp.mean(x * x, axis=-1, keepdims=True) + EPS)
    o_ref[...] = (y * g_ref[...]).astype(o_ref.dtype)


def _rmsnorm(x, g, out_dtype):
    m, d = x.shape
    bm = _tile(m, 512, 8)
    return pl.pallas_call(
        _rmsnorm_kernel,
        grid=(m // bm,),
        in_specs=[pl.BlockSpec((bm, d), lambda i: (i, 0)),
                  pl.BlockSpec((1, d), lambda i: (0, 0))],
        out_specs=pl.BlockSpec((bm, d), lambda i: (i, 0)),
        out_shape=jax.ShapeDtypeStruct((m, d), out_dtype),
        compiler_params=_params("parallel"),
        name="rmsnorm",
    )(x, g.reshape(1, d).astype(F32))


def _row_scale(rs_ref, n_lanes):
    return pltpu.repeat(rs_ref[...], n_lanes // LANES, axis=1)


MM_SUB_ROWS = 512
SMALL_M_W_BLOCK_BYTES = 16 * 1024 * 1024


def _row_blocks(bm):
    sub = MM_SUB_ROWS if bm % MM_SUB_ROWS == 0 else bm
    return [pl.ds(r, sub) for r in range(0, bm, sub)]


def _mm_kernel(*refs, has_res, has_scale, has_gain, n_side, out_width, emit_w=False):
    refs = list(refs)
    x_ref, w_ref = refs.pop(0), refs.pop(0)
    r_ref = refs.pop(0) if has_res else None
    ss_in_ref = refs.pop(0) if has_scale else None
    g_ref = refs.pop(0) if has_gain else None
    src = [refs.pop(0) for _ in range(n_side)]
    o_ref = refs.pop(0)
    og_ref, ss_ref = (refs.pop(0), refs.pop(0)) if has_gain else (None, None)
    wcopy_ref = refs.pop(0) if emit_w else None
    dst = refs
    if has_gain:
        @pl.when(pl.program_id(1) == 0)
        def _():
            ss_ref[...] = jnp.zeros_like(ss_ref)

    w = w_ref[...].astype(BF16)
    if emit_w:
        wcopy_ref[...] = w
    for rows in _row_blocks(x_ref.shape[0]):
        acc = jnp.dot(x_ref[rows, :], w, preferred_element_type=F32)
        if has_scale:
            acc = acc * _row_scale(ss_in_ref.at[rows], acc.shape[1])
        if has_res:
            acc = r_ref[rows, :] + acc
        o_ref[rows, :] = acc.astype(o_ref.dtype)
        if has_gain:
            og_ref[rows, :] = (acc * g_ref[...]).astype(og_ref.dtype)
            sq = acc * acc
            ss_ref[rows, :] += functools.reduce(
                jnp.add, [sq[:, c:c + LANES] for c in range(0, sq.shape[1], LANES)])

    if has_gain:
        @pl.when(pl.program_id(1) == pl.num_programs(1) - 1)
        def _():
            total = jnp.sum(ss_ref[...], axis=-1, keepdims=True)
            ss_ref[...] = jnp.broadcast_to(lax.rsqrt(total * (1.0 / out_width) + EPS), ss_ref.shape)
    for src_ref, dst_ref in zip(src, dst):
        dst_ref[...] = src_ref[...].astype(BF16)


def _mm_gated_kernel(a_ref, b_ref, wa_ref, wb_ref, ga_ref, gb_ref, o_ref):
    for rows in _row_blocks(a_ref.shape[0]):
        a = jnp.dot(a_ref[rows, :], wa_ref[...], preferred_element_type=F32)
        b = jnp.dot(b_ref[rows, :], wb_ref[...], preferred_element_type=F32)
        ga = jax.nn.sigmoid(ga_ref[rows, :].astype(F32))
        gb = jax.nn.sigmoid(gb_ref[rows, :].astype(F32))
        o_ref[rows, :] = (ga * a + gb * b).astype(o_ref.dtype)


def _mm_swiglu_kernel(x_ref, wg_ref, wu_ref, ss_ref, o_ref):
    for rows in _row_blocks(x_ref.shape[0]):
        x = x_ref[rows, :]
        scale = _row_scale(ss_ref.at[rows], wg_ref.shape[1])
        g = jnp.dot(x, wg_ref[...], preferred_element_type=F32) * scale
        u = jnp.dot(x, wu_ref[...], preferred_element_type=F32) * scale
        o_ref[rows, :] = (g * jax.nn.sigmoid(g) * u).astype(o_ref.dtype)


def _matmul(x, w, out_dtype, *, bm_pref=1024, bn_pref=1024, res=None, cols=None, side=(),
            row_ss=None, norm_gain=None, cols_outer=False, emit_w=False):
    m, k = x.shape
    cols = cols or [(0, w.shape[1])]
    n = sum(width for _, width in cols)
    bm = _tile(m, bm_pref, 8)
    if m < bm_pref and not emit_w:
        wide = min(SMALL_M_W_BLOCK_BYTES // (w.dtype.itemsize * k), bn_pref * bm_pref // m)
        bn_pref = max(bn_pref, wide // LANES * LANES)
    bn = _tile(int(functools.reduce(np.gcd, [v for seg in cols for v in seg if v])), bn_pref, LANES)
    steps, out_blk = [], 0
    for start, width in cols:
        steps.append((out_blk, start // bn - out_blk))
        out_blk += width // bn

    def w_block(j):
        blk = j + steps[0][1]
        for (first, shift), (_, prev) in zip(steps[1:], steps[:-1]):
            blk = blk + jnp.where(j >= first, shift - prev, 0)
        return blk

    ni, nj = m // bm, n // bn
    if cols_outer:
        assert not side and norm_gain is None and row_ss is None
        in_specs = [pl.BlockSpec((bm, k), lambda j, i: (i, 0)),
                    pl.BlockSpec((k, bn), lambda j, i: (0, w_block(j)))]
        args = [x, w]
        if res is not None:
            in_specs.append(pl.BlockSpec((bm, bn), lambda j, i: (i, j)))
            args.append(res)
        return pl.pallas_call(
            functools.partial(_mm_kernel, has_res=res is not None, has_scale=False, has_gain=False,
                              n_side=0, out_width=n),
            grid=(nj, ni),
            in_specs=in_specs,
            out_specs=[pl.BlockSpec((bm, bn), lambda j, i: (i, j))],
            out_shape=[jax.ShapeDtypeStruct((m, n), out_dtype)],
            compiler_params=_params("parallel", "parallel"),
            name="matmul_res" if res is not None else "matmul",
        )(*args)[0]
    in_specs = [pl.BlockSpec((bm, k), lambda i, j: (i, 0)),
                pl.BlockSpec((k, bn), lambda i, j: (0, w_block(j)))]
    args = [x, w]
    if res is not None:
        in_specs.append(pl.BlockSpec((bm, bn), lambda i, j: (i, j)))
        args.append(res)
    ss_spec = pl.BlockSpec((bm, LANES), lambda i, j: (i, 0))
    if row_ss is not None:
        in_specs.append(ss_spec)
        args.append(row_ss)
    out_specs = [pl.BlockSpec((bm, bn), lambda i, j: (i, j))]
    out_shape = [jax.ShapeDtypeStruct((m, n), out_dtype)]
    if norm_gain is not None:
        assert not side and cols == [(0, w.shape[1])]
        in_specs.append(pl.BlockSpec((1, bn), lambda i, j: (0, j)))
        args.append(norm_gain.reshape(1, n).astype(F32))
        out_specs += [pl.BlockSpec((bm, bn), lambda i, j: (i, j)), ss_spec]
        out_shape += [jax.ShapeDtypeStruct((m, n), BF16), jax.ShapeDtypeStruct((m, LANES), F32)]
    if emit_w:
        assert ni == 1 and not side and norm_gain is None
        out_specs.append(pl.BlockSpec((k, bn), lambda i, j: (0, j)))
        out_shape.append(jax.ShapeDtypeStruct((k, n), BF16))
    for sw in side:
        rows, width = sw.shape
        n_blocks = max(nb for nb in range(1, ni * nj + 1)
                       if rows % nb == 0 and (rows // nb) % BF16_SUBLANES == 0)
        spec = pl.BlockSpec((rows // n_blocks, width),
                            lambda i, j, n_blocks=n_blocks: (jnp.minimum(i * nj + j, n_blocks - 1), 0))
        in_specs.append(spec)
        args.append(sw)
        out_specs.append(spec)
        out_shape.append(jax.ShapeDtypeStruct((rows, width), BF16))
    outs = pl.pallas_call(
        functools.partial(_mm_kernel, has_res=res is not None, has_scale=row_ss is not None,
                          has_gain=norm_gain is not None, n_side=len(side), out_width=n,
                          emit_w=emit_w),
        grid=(ni, nj),
        in_specs=in_specs,
        out_specs=out_specs,
        out_shape=out_shape,
        compiler_params=(_params("arbitrary", "arbitrary") if side else
                         _params("parallel", "arbitrary") if norm_gain is not None else
                         _params("parallel", "parallel")),
        name="matmul_res" if res is not None else "matmul",
    )(*args)
    if side:
        return outs[0], list(outs[1:])
    return tuple(outs) if norm_gain is not None or emit_w else outs[0]


def _matmul_gated(a, b, wa, wb, proj, ga_col, gb_col):
    m, ka = a.shape
    kb = b.shape[1]
    n = wa.shape[1]
    bm = _tile(m, 1024, 8)
    bn = _tile(n, 1024, LANES)
    assert ga_col % bn == 0 and gb_col % bn == 0
    ga_blk, gb_blk = ga_col // bn, gb_col // bn
    return pl.pallas_call(
        _mm_gated_kernel,
        grid=(m // bm, n // bn),
        in_specs=[pl.BlockSpec((bm, ka), lambda i, j: (i, 0)),
                  pl.BlockSpec((bm, kb), lambda i, j: (i, 0)),
                  pl.BlockSpec((ka, bn), lambda i, j: (0, j)),
                  pl.BlockSpec((kb, bn), lambda i, j: (0, j)),
                  pl.BlockSpec((bm, bn), lambda i, j: (i, ga_blk + j)),
                  pl.BlockSpec((bm, bn), lambda i, j: (i, gb_blk + j))],
        out_specs=pl.BlockSpec((bm, bn), lambda i, j: (i, j)),
        out_shape=jax.ShapeDtypeStruct((m, n), BF16),
        compiler_params=_params("parallel", "parallel"),
        name="matmul_gated",
    )(a, b, wa, wb, proj, proj)


def _matmul_swiglu(x, row_ss, wg, wu):
    m, k = x.shape
    n = wg.shape[1]
    bm = _tile(m, 2048, 8)
    bn = _tile(n, 256, LANES)
    return pl.pallas_call(
        _mm_swiglu_kernel,
        grid=(m // bm, n // bn),
        in_specs=[pl.BlockSpec((bm, k), lambda i, j: (i, 0)),
                  pl.BlockSpec((k, bn), lambda i, j: (0, j)),
                  pl.BlockSpec((k, bn), lambda i, j: (0, j)),
                  pl.BlockSpec((bm, LANES), lambda i, j: (i, 0))],
        out_specs=pl.BlockSpec((bm, bn), lambda i, j: (i, j)),
        out_shape=jax.ShapeDtypeStruct((m, n), BF16),
        compiler_params=_params("parallel", "parallel"),
        name="matmul_swiglu",
    )(x, wg, wu, row_ss)


def _hgrn_tables(c, rows, pb):
    t = np.arange(rows)[:, None]
    s = np.arange(rows)[None, :]
    tri = ((t // c) == (s // c)) & (s <= t)
    t, s = t[:pb, :pb], s[:pb, :pb]
    masks = []
    l = 1
    while l < c:
        masks.append(((t // l) % 2 == 1) & ((s // l) == (t // l) - 1))
        l *= 2
    masks.append(t == s)
    return tri.astype(np.float32), np.stack(masks, 0).astype(np.float32)


def _hgrn_kernel(tri_ref, mask_ref, lbl_ref, q_ref, f_ref, v_ref, og_ref, gn_ref, s0_ref,
                 o_ref, sfin_ref, st_ref, oi_ref, qin_ref, u_ref, dec_ref,
                 *, c, n_chunks, rows, pb, layer, carry):
    t_idx = pl.program_id(2)
    d = HEAD_DIM_HGRN
    n_lv = c.bit_length() - 1
    g_chunks = rows // c
    n_pb = rows // pb
    nt = (((1,), (1,)), ((), ()))
    tn = (((0,), (0,)), ((), ()))

    if carry:
        @pl.when(t_idx == 0)
        def _():
            st_ref[...] = s0_ref[0, 0].T

    logits = lbl_ref[...].astype(F32)
    ex = jnp.exp(logits - jnp.max(logits, axis=0, keepdims=True))
    lb = jnp.sum(ex[:layer + 1], axis=0, keepdims=True) / jnp.sum(ex, axis=0, keepdims=True)
    gain = gn_ref[...].astype(F32)
    shape3 = (rows // SUBLANES, SUBLANES, d)
    sub = lax.broadcasted_iota(jnp.int32, (1, SUBLANES, d), 1)

    def block_rows(x, first, step, length):
        pieces = [jnp.broadcast_to(x[first + step * p:first + step * p + 1, :], (length, d))
                  for p in range(rows // length)]
        return pieces[0] if len(pieces) == 1 else jnp.concatenate(pieces, axis=0)

    def pair_ref(b, l):
        if l >= SUBLANES:
            return block_rows(b, l - 1, 2 * l, 2 * l)
        b3 = b.reshape(shape3)
        pick = lambda i: jnp.broadcast_to(b3[:, i:i + 1, :], shape3)
        r3 = pick(3) if l == 4 else jnp.where(sub < 4, pick(1), pick(5))
        return r3.reshape(rows, d)

    def odd_half(x_odd, x_even, l):
        if l >= SUBLANES:
            pieces = [(x_odd if p % 2 else x_even)[p * l:(p + 1) * l] for p in range(rows // l)]
            return jnp.concatenate(pieces, axis=0)
        return jnp.where((sub & l) != 0, x_odd.reshape(shape3), x_even.reshape(shape3)).reshape(rows, d)

    def phase_a(bi, carry):
        rsel = pl.ds(pl.multiple_of(bi * rows, rows), rows)
        f = lb + (1.0 - lb) * jax.nn.sigmoid(f_ref[0, rsel, :])
        g = jnp.log(f)
        k = 1.0 - f
        qb = q_ref[0, rsel, :]
        q = qb.astype(F32)
        v = v_ref[0, rsel, :]

        g_hi = g.astype(BF16)
        r1 = g - g_hi.astype(F32)
        g_mid = r1.astype(BF16)
        g_lo = (r1 - g_mid.astype(F32)).astype(BF16)
        b3 = jnp.dot(tri_ref[...], jnp.concatenate([g_hi, g_mid, g_lo], axis=1),
                     preferred_element_type=F32)
        b = b3[:, 0:d] + b3[:, d:2 * d] + b3[:, 2 * d:3 * d]

        def scores(zl, zr, li):
            p = lax.dot_general(zl, zr, nt, preferred_element_type=F32)
            return [mask_ref[li] * p[i * pb:(i + 1) * pb, i * pb:(i + 1) * pb] for i in range(n_pb)]

        a = scores(qb, k.astype(BF16), n_lv)
        z = odd_half(q * f, k, 1).astype(BF16)
        a = [x + y for x, y in zip(a, scores(z, z, 0))]
        for li in range(1, n_lv):
            l = 1 << li
            w = jnp.exp(-jnp.abs(b - pair_ref(b, l)))
            z = (odd_half(q, k, l) * w).astype(BF16)
            a = [x + y for x, y in zip(a, scores(z, z, li))]
        for i in range(n_pb):
            oi_ref[pl.ds(pl.multiple_of(bi * rows + i * pb, pb), pb), :] = jnp.dot(
                a[i].astype(BF16), v[i * pb:(i + 1) * pb], preferred_element_type=F32)

        dec = jnp.exp(b)
        qin_ref[rsel, :] = (q * dec).astype(BF16)
        k_out = (k * jnp.exp(block_rows(b, c - 1, c, c) - b)).astype(BF16)
        for ci in range(g_chunks):
            cr = slice(ci * c, (ci + 1) * c)
            u_ref[bi * g_chunks + ci] = lax.dot_general(v[cr], k_out[cr], tn, preferred_element_type=F32)
            dec_ref[pl.ds(pl.multiple_of((bi * g_chunks + ci) * SUBLANES, SUBLANES), SUBLANES), :] = (
                jnp.broadcast_to(dec[ci * c + c - 1:ci * c + c, :], (SUBLANES, d)))
        return carry

    def phase_bc(bi):
        st = st_ref[...] if carry else None
        outs = []
        for ci in range(g_chunks):
            chunk = bi * g_chunks + ci
            rsel = pl.ds(pl.multiple_of(chunk * c, c), c)
            if not carry:
                st = s0_ref[chunk, 0].T
            outs.append(oi_ref[rsel, :] + lax.dot_general(
                qin_ref[rsel, :], st.astype(BF16), nt, preferred_element_type=F32))
            st = st * dec_ref[pl.ds(pl.multiple_of(chunk * SUBLANES, SUBLANES), 1), :] + u_ref[chunk]
            if not carry:
                sfin_ref[chunk, 0] = st.T
        if carry:
            st_ref[...] = st
        rsel = pl.ds(pl.multiple_of(bi * rows, rows), rows)
        o = outs[0] if g_chunks == 1 else jnp.concatenate(outs, axis=0)
        y = o * lax.rsqrt(jnp.mean(o * o, axis=-1, keepdims=True) + EPS) * gain
        og = og_ref[0, rsel, :].astype(F32)
        o_ref[0, rsel, :] = (y * (og * jax.nn.sigmoid(og))).astype(o_ref.dtype)

    n_batches = n_chunks // g_chunks
    for bi in range(n_batches):
        phase_a(bi, 0)
        if bi > 0:
            phase_bc(bi - 1)
    phase_bc(n_batches - 1)

    if carry:
        @pl.when(t_idx == pl.num_programs(2) - 1)
        def _():
            sfin_ref[0, 0] = st_ref[...].T


def _hgrn(proj3, fpre3, lb_logits, gain, s0, layer, q_col, v_col, og_col):
    bsz, t, _ = proj3.shape
    hw = fpre3.shape[-1]
    d = HEAD_DIM_HGRN
    nh = hw // d
    c = CHUNK if t % CHUNK == 0 else t
    assert c & (c - 1) == 0 and c >= 2 * SUBLANES and t % c == 0
    carry = t > c
    if not carry:
        out, s_fin = _hgrn_call(proj3.reshape(1, bsz * t, -1), fpre3.reshape(1, bsz * t, hw), lb_logits,
                                gain, s0, layer, q_col, v_col, og_col, c=c, carry=False)
        return out.reshape(bsz, t, hw), s_fin
    return _hgrn_call(proj3, fpre3, lb_logits, gain, s0, layer, q_col, v_col, og_col, c=c, carry=True)


def _hgrn_call(proj3, fpre3, lb_logits, gain, s0, layer, q_col, v_col, og_col, *, c, carry):
    bsz, t, _ = proj3.shape
    hw = fpre3.shape[-1]
    d = HEAD_DIM_HGRN
    nh = hw // d
    tt = _tile(t, 4096, c)
    n_chunks = tt // c
    rows = _tile(tt, 2 * LANES, c)
    pb = min(rows, LANES)
    tri, masks = _hgrn_tables(c, rows, pb)
    qb, vb, ob = q_col // d, v_col // d, og_col // d
    kern = functools.partial(_hgrn_kernel, c=c, n_chunks=n_chunks, rows=rows, pb=pb, layer=layer,
                             carry=carry)
    nl = lb_logits.shape[0]
    if carry:
        state_spec = pl.BlockSpec((1, 1, d, d), lambda b, h, i: (b, h, 0, 0))
    else:
        state_spec = pl.BlockSpec((n_chunks, 1, d, d), lambda b, h, i: (i, h, 0, 0))
    return pl.pallas_call(
        kern,
        grid=(bsz, nh, t // tt),
        in_specs=[pl.BlockSpec((rows, rows), lambda b, h, i: (0, 0)),
                  pl.BlockSpec(masks.shape, lambda b, h, i: (0, 0, 0)),
                  pl.BlockSpec((nl, d), lambda b, h, i: (0, h)),
                  pl.BlockSpec((1, tt, d), lambda b, h, i: (b, i, qb + h)),
                  pl.BlockSpec((1, tt, d), lambda b, h, i: (b, i, h)),
                  pl.BlockSpec((1, tt, d), lambda b, h, i: (b, i, vb + h)),
                  pl.BlockSpec((1, tt, d), lambda b, h, i: (b, i, ob + h)),
                  pl.BlockSpec((1, d), lambda b, h, i: (0, h)),
                  state_spec],
        out_specs=[pl.BlockSpec((1, tt, d), lambda b, h, i: (b, i, h)),
                   state_spec],
        out_shape=[jax.ShapeDtypeStruct((bsz, t, hw), BF16),
                   jax.ShapeDtypeStruct(s0.shape, F32)],
        scratch_shapes=[pltpu.VMEM((d, d), F32),
                        pltpu.VMEM((tt, d), F32),
                        pltpu.VMEM((tt, d), BF16),
                        pltpu.VMEM((n_chunks, d, d), F32),
                        pltpu.VMEM((n_chunks * SUBLANES, d), F32)],
        compiler_params=_params("parallel", "parallel", "arbitrary"),
        name="hgrn2",
    )(jnp.asarray(tri, BF16), jnp.asarray(masks, F32), lb_logits.astype(F32),
      proj3, fpre3, proj3, proj3, gain.reshape(1, hw).astype(F32), s0.astype(F32))


CONV_PAD = 8
CONV_BLOCK_ELEMS = 1024 * 1024


def _conv_kernel(cc_ref, ch_ref, cb_ref, w_ref, buf_ref, o_ref, nbuf_ref, u_ref, *, t, k, rb):
    u_ref[CONV_PAD - (k - 1):CONV_PAD, :] = buf_ref[0].astype(F32)
    for r in range(0, t, rb):
        u_ref[CONV_PAD + r:CONV_PAD + r + rb, :] = (
            cc_ref[0, r:r + rb, :].astype(F32) * ch_ref[0, r:r + rb, :].astype(F32))
    w = w_ref[...].astype(F32)
    for r in range(0, t, rb):
        z = w[0:1, :] * u_ref[CONV_PAD - (k - 1) + r:CONV_PAD - (k - 1) + r + rb, :]
        for j in range(1, k):
            s = CONV_PAD - (k - 1) + j + r
            z = z + w[j:j + 1, :] * u_ref[s:s + rb, :]
        o_ref[0, r:r + rb, :] = (cb_ref[0, r:r + rb, :].astype(F32) * z).astype(o_ref.dtype)
    nbuf_ref[0] = u_ref[CONV_PAD + t - (k - 1):CONV_PAD + t, :]


def _conv(proj3, conv_w, buf, ch_col, cb_col, cc_col):
    bsz, t, _ = proj3.shape
    k, cw = conv_w.shape
    assert k - 1 <= CONV_PAD
    d = _tile(cw, max(LANES, CONV_BLOCK_ELEMS // t), LANES)
    rb = _tile(t, max(8, 512 * LANES // d), 8)
    kern = functools.partial(_conv_kernel, t=t, k=k, rb=rb)
    return pl.pallas_call(
        kern,
        grid=(bsz, cw // d),
        in_specs=[pl.BlockSpec((1, t, d), lambda b, j: (b, 0, cc_col // d + j)),
                  pl.BlockSpec((1, t, d), lambda b, j: (b, 0, ch_col // d + j)),
                  pl.BlockSpec((1, t, d), lambda b, j: (b, 0, cb_col // d + j)),
                  pl.BlockSpec((k, d), lambda b, j: (0, j)),
                  pl.BlockSpec((1, k - 1, d), lambda b, j: (b, 0, j))],
        out_specs=[pl.BlockSpec((1, t, d), lambda b, j: (b, 0, j)),
                   pl.BlockSpec((1, k - 1, d), lambda b, j: (b, 0, j))],
        out_shape=[jax.ShapeDtypeStruct((bsz, t, cw), BF16),
                   jax.ShapeDtypeStruct((bsz, k - 1, cw), F32)],
        scratch_shapes=[pltpu.VMEM((CONV_PAD + t, d), F32)],
        compiler_params=_params("parallel", "parallel"),
        name="short_conv",
    )(proj3, proj3, proj3, conv_w.astype(F32), buf.astype(F32))


def _attn_kernel(q_ref, k_ref, v_ref, o_ref, kb_ref, vb_ref, *, scale, sub_rows):
    nt = (((1,), (1,)), ((), ()))

    @pl.when(pl.program_id(2) == 0)
    def _():
        kb_ref[...] = k_ref[0].astype(BF16)
        vb_ref[...] = v_ref[0].astype(BF16)

    subs = [pl.ds(r, sub_rows) for r in range(0, q_ref.shape[1], sub_rows)]
    scores = [lax.dot_general(q_ref[0, r, :], kb_ref[...], nt, preferred_element_type=F32) * scale
              for r in subs]
    probs = []
    for s in scores:
        p = jnp.exp(s - jnp.max(s, axis=-1, keepdims=True))
        probs.append((p / jnp.sum(p, axis=-1, keepdims=True)).astype(BF16))
    for r, p in zip(subs, probs):
        o_ref[0, r, :] = jnp.dot(p, vb_ref[...], preferred_element_type=F32).astype(o_ref.dtype)


def _attention(q3, mk4, mv4, mem_layer, n_heads):
    bsz, t, dm = q3.shape
    n_mem = mk4.shape[2]
    hd = dm // n_heads
    mem_spec = pl.BlockSpec((None, 1, n_mem, hd), lambda b, h, i: (mem_layer, b, 0, h))
    bt = _tile(t, 2048, 8)
    kern = functools.partial(_attn_kernel, scale=float(hd) ** -0.5, sub_rows=_tile(bt, 512, 8))
    return pl.pallas_call(
        kern,
        grid=(bsz, n_heads, t // bt),
        in_specs=[pl.BlockSpec((1, bt, hd), lambda b, h, i: (b, i, h)), mem_spec, mem_spec],
        out_specs=pl.BlockSpec((1, bt, hd), lambda b, h, i: (b, i, h)),
        out_shape=jax.ShapeDtypeStruct((bsz, t, dm), BF16),
        scratch_shapes=[pltpu.VMEM((n_mem, hd), BF16), pltpu.VMEM((n_mem, hd), BF16)],
        compiler_params=_params("parallel", "parallel", "arbitrary"),
        name="cross_attention",
    )(q3, mk4, mv4)


def _project(x, wts):
    hw = wts["hg_norm"].shape[0]
    h = _rmsnorm(x, wts["norm_mix"], BF16)
    if "w_in" in wts:
        pw = wts["w_in"].shape[1]
        proj, w_rest = _matmul(h, wts["w_in"], BF16, bn_pref=512, cols=[(0, hw), (2 * hw, pw - 2 * hw)],
                               emit_w=True)
        fpre, w_f = _matmul(h, wts["w_in"], F32, bn_pref=512, cols=[(hw, hw)], emit_w=True)
        wts = {k: v for k, v in wts.items() if k != "w_in"} | {"w_in_rest": w_rest, "w_in_f": w_f}
    elif "f32" in wts:
        names = list(wts["f32"])
        proj, cast = _matmul(h, wts["w_in_rest"], BF16, side=[wts["f32"][n] for n in names])
        fpre = _matmul(h, wts["w_in_f"], F32)
        wts = {k: v for k, v in wts.items() if k != "f32"} | dict(zip(names, cast))
    else:
        proj = _matmul(h, wts["w_in_rest"], BF16)
        fpre = _matmul(h, wts["w_in_f"], F32)
    return proj, fpre, wts


def _layer(x, proj, fpre, bsz, t, wts, lb_logits, layer, s0, buf, mem, n_xa_heads):
    m, dm = x.shape
    hw = wts["hg_norm"].shape[0]
    cw = wts["conv_w"].shape[1]
    q_col, v_col, og_col = 0, hw, 2 * hw
    ch_col, cb_col, cc_col = 3 * hw, 3 * hw + cw, 3 * hw + 2 * cw
    ga_col, gb_col = 3 * hw + 3 * cw, 3 * hw + 3 * cw + dm
    proj3 = proj.reshape(bsz, t, proj.shape[1])
    a_in, s_fin = _hgrn(proj3, fpre.reshape(bsz, t, hw), lb_logits, wts["hg_norm"], s0, layer,
                        q_col, v_col, og_col)
    bz, new_buf = _conv(proj3, wts["conv_w"], buf, ch_col, cb_col, cc_col)
    merged = _matmul_gated(a_in.reshape(m, hw), bz.reshape(m, cw), wts["w_a"], wts["w_b"],
                           proj, ga_col, gb_col)
    x, xg, ss = _matmul(merged, wts["w_o"], F32, bn_pref=512, res=x, norm_gain=wts["norm_xattn"])
    qx = _matmul(xg, wts["w_xq"], BF16, row_ss=ss)
    att = _attention(qx.reshape(bsz, t, dm), *mem, n_xa_heads)
    x, xg, ss = _matmul(att.reshape(m, dm), wts["w_xo"], F32, bn_pref=512, res=x,
                        norm_gain=wts["norm_ffn"])
    act = _matmul_swiglu(xg, ss, wts["w_gate"], wts["w_up"])
    x = _matmul(act, wts["w_down"], F32, bm_pref=512, bn_pref=512, res=x, cols_outer=True)
    return x, s_fin, new_buf


def kernel(x_prompt, x_sample, cache_mem_k, cache_mem_v, state_hgrn, state_conv, mem_prompt, norm_mix, w_in, lb_logits, hg_norm, conv_w, w_a, w_b, w_o, norm_xattn, norm_mem, w_xq, w_xk, w_xv, w_xo, norm_ffn, w_gate, w_up, w_down, norm_final):
    depth = norm_mix.shape[0]
    bp, tp, dm = x_prompt.shape
    bs, ts, _ = x_sample.shape
    hw = hg_norm.shape[1]
    n_mem = mem_prompt.shape[1]
    n_xa_heads = cache_mem_k.shape[3]
    nh, dk, dv = state_hgrn.shape[2:]
    assert dk == HEAD_DIM_HGRN and dv == HEAD_DIM_HGRN and nh * dk == hw

    xp = x_prompt.reshape(bp * tp, dm)
    xs = x_sample.reshape(bs * ts, dm)
    outs = {k: [] for k in ("mk", "mv", "sp", "cp", "ss", "cs")}
    for l in range(depth):
        w_in_l = w_in[l]
        wts = {
            "norm_mix": norm_mix[l], "hg_norm": hg_norm[l], "conv_w": conv_w[l],
            "norm_xattn": norm_xattn[l], "norm_ffn": norm_ffn[l],
            "w_in": w_in_l,
            "f32": {"w_a": w_a[l], "w_b": w_b[l], "w_o": w_o[l], "w_xq": w_xq[l], "w_xo": w_xo[l],
                    "w_gate": w_gate[l], "w_up": w_up[l], "w_down": w_down[l]},
        }
        mem_n = _rmsnorm(mem_prompt.reshape(bp * n_mem, dm), norm_mem[l], BF16)
        mk_p = _matmul(mem_n, w_xk[l], F32)
        mv_p = _matmul(mem_n, w_xv[l], F32)
        s0 = jnp.zeros((bp, nh, dk, dv), F32)
        buf0 = jnp.zeros((bp, conv_w.shape[1] - 1, conv_w.shape[2]), F32)
        mem_p = (mk_p.reshape(1, bp, n_mem, dm), mv_p.reshape(1, bp, n_mem, dm), 0)
        mem_s = (cache_mem_k.reshape(depth, bs, n_mem, dm), cache_mem_v.reshape(depth, bs, n_mem, dm), l)
        proj_s, fpre_s, wts = _project(xs, wts)
        proj_p, fpre_p, wts = _project(xp, wts)
        xp, s_p, buf_p = _layer(xp, proj_p, fpre_p, bp, tp, wts, lb_logits, l, s0, buf0, mem_p, n_xa_heads)
        xs, s_s, buf_s = _layer(xs, proj_s, fpre_s, bs, ts, wts, lb_logits, l, state_hgrn[l],
                                state_conv[l], mem_s, n_xa_heads)
        outs["mk"].append(mk_p.reshape(bp, n_mem, n_xa_heads, dm // n_xa_heads))
        outs["mv"].append(mv_p.reshape(bp, n_mem, n_xa_heads, dm // n_xa_heads))
        outs["sp"].append(s_p)
        outs["cp"].append(buf_p)
        outs["ss"].append(s_s)
        outs["cs"].append(buf_s)
    y_prompt = _rmsnorm(xp, norm_final, F32).reshape(bp, tp, dm)
    y_sample = _rmsnorm(xs, norm_final, F32).reshape(bs, ts, dm)
    return (y_prompt, y_sample, jnp.stack(outs["mk"]), jnp.stack(outs["mv"]), jnp.stack(outs["sp"]),
            jnp.stack(outs["cp"]), jnp.stack(outs["ss"]), jnp.stack(outs["cs"]))
```

```python
import functools

import numpy as np
import jax
import jax.numpy as jnp
from jax import lax
from jax.experimental import pallas as pl
from jax.experimental.pallas import tpu as pltpu

EPS = 1e-6
LANES = 128
SUBLANES = 8
HEAD_DIM_HGRN = 128
CHUNK = 64
VMEM_LIMIT_BYTES = 60 * 1024 * 1024
BF16_SUBLANES = 16
BF16 = jnp.bfloat16
F32 = jnp.float32


def _params(*semantics):
    return pltpu.CompilerParams(dimension_semantics=semantics, vmem_limit_bytes=VMEM_LIMIT_BYTES)


def _tile(n, pref, mult):
    if n <= pref:
        return n
    t = (pref // mult) * mult
    while t >= mult:
        if n % t == 0:
            return t
        t -= mult
    raise ValueError(f"no tile for {n} (pref {pref}, mult {mult})")


def _rmsnorm_kernel(x_ref, g_ref, o_ref):
    x = x_ref[...].astype(F32)
    y = x * lax.rsqrt(jnp.mean(x * x, axis=-1, keepdims=True) + EPS)
    o_ref[...] = (y * g_ref[...]).astype(o_ref.dtype)


def _rmsnorm(x, g, out_dtype):
    m, d = x.shape
    bm = _tile(m, 512, 8)
    return pl.pallas_call(
        _rmsnorm_kernel,
        grid=(m // bm,),
        in_specs=[pl.BlockSpec((bm, d), lambda i: (i, 0)),
                  pl.BlockSpec((1, d), lambda i: (0, 0))],
        out_specs=pl.BlockSpec((bm, d), lambda i: (i, 0)),
        out_shape=jax.ShapeDtypeStruct((m, d), out_dtype),
        compiler_params=_params("parallel"),
        name="rmsnorm",
    )(x, g.reshape(1, d).astype(F32))


def _row_scale(rs_ref, n_lanes):
    return pltpu.repeat(rs_ref[...], n_lanes // LANES, axis=1)


MM_SUB_ROWS = 512


def _row_blocks(bm):
    sub = MM_SUB_ROWS if bm % MM_SUB_ROWS == 0 else bm
    return [pl.ds(r, sub) for r in range(0, bm, sub)]


def _mm_kernel(*refs, has_res, has_scale, has_gain, n_side, out_width, emit_w=False):
    refs = list(refs)
    x_ref, w_ref = refs.pop(0), refs.pop(0)
    r_ref = refs.pop(0) if has_res else None
    ss_in_ref = refs.pop(0) if has_scale else None
    g_ref = refs.pop(0) if has_gain else None
    src = [refs.pop(0) for _ in range(n_side)]
    o_ref = refs.pop(0)
    og_ref, ss_ref = (refs.pop(0), refs.pop(0)) if has_gain else (None, None)
    wcopy_ref = refs.pop(0) if emit_w else None
    dst = refs
    if has_gain:
        @pl.when(pl.program_id(1) == 0)
        def _():
            ss_ref[...] = jnp.zeros_like(ss_ref)

    w = w_ref[...].astype(BF16)
    if emit_w:
        wcopy_ref[...] = w
    for rows in _row_blocks(x_ref.shape[0]):
        acc = jnp.dot(x_ref[rows, :], w, preferred_element_type=F32)
        if has_scale:
            acc = acc * _row_scale(ss_in_ref.at[rows], acc.shape[1])
        if has_res:
            acc = r_ref[rows, :] + acc
        o_ref[rows, :] = acc.astype(o_ref.dtype)
        if has_gain:
            og_ref[rows, :] = (acc * g_ref[...]).astype(og_ref.dtype)
            sq = acc * acc
            ss_ref[rows, :] += functools.reduce(
                jnp.add, [sq[:, c:c + LANES] for c in range(0, sq.shape[1], LANES)])

    if has_gain:
        @pl.when(pl.program_id(1) == pl.num_programs(1) - 1)
        def _():
            total = jnp.sum(ss_ref[...], axis=-1, keepdims=True)
            ss_ref[...] = jnp.broadcast_to(lax.rsqrt(total * (1.0 / out_width) + EPS), ss_ref.shape)
    for src_ref, dst_ref in zip(src, dst):
        dst_ref[...] = src_ref[...].astype(BF16)


def _mm_gated_kernel(a_ref, b_ref, wa_ref, wb_ref, ga_ref, gb_ref, o_ref):
    for rows in _row_blocks(a_ref.shape[0]):
        a = jnp.dot(a_ref[rows, :], wa_ref[...], preferred_element_type=F32)
        b = jnp.dot(b_ref[rows, :], wb_ref[...], preferred_element_type=F32)
        ga = jax.nn.sigmoid(ga_ref[rows, :].astype(F32))
        gb = jax.nn.sigmoid(gb_ref[rows, :].astype(F32))
        o_ref[rows, :] = (ga * a + gb * b).astype(o_ref.dtype)


def _mm_swiglu_kernel(x_ref, wg_ref, wu_ref, ss_ref, o_ref):
    for rows in _row_blocks(x_ref.shape[0]):
        x = x_ref[rows, :]
        scale = _row_scale(ss_ref.at[rows], wg_ref.shape[1])
        g = jnp.dot(x, wg_ref[...], preferred_element_type=F32) * scale
        u = jnp.dot(x, wu_ref[...], preferred_element_type=F32) * scale
        o_ref[rows, :] = (g * jax.nn.sigmoid(g) * u).astype(o_ref.dtype)


def _matmul(x, w, out_dtype, *, bm_pref=1024, bn_pref=1024, res=None, cols=None, side=(),
            row_ss=None, norm_gain=None, cols_outer=False, emit_w=False):
    m, k = x.shape
    cols = cols or [(0, w.shape[1])]
    n = sum(width for _, width in cols)
    bm = _tile(m, bm_pref, 8)
    bn = _tile(int(functools.reduce(np.gcd, [v for seg in cols for v in seg if v])), bn_pref, LANES)
    steps, out_blk = [], 0
    for start, width in cols:
        steps.append((out_blk, start // bn - out_blk))
        out_blk += width // bn

    def w_block(j):
        blk = j + steps[0][1]
        for (first, shift), (_, prev) in zip(steps[1:], steps[:-1]):
            blk = blk + jnp.where(j >= first, shift - prev, 0)
        return blk

    ni, nj = m // bm, n // bn
    if cols_outer:
        assert not side and norm_gain is None and row_ss is None
        in_specs = [pl.BlockSpec((bm, k), lambda j, i: (i, 0)),
                    pl.BlockSpec((k, bn), lambda j, i: (0, w_block(j)))]
        args = [x, w]
        if res is not None:
            in_specs.append(pl.BlockSpec((bm, bn), lambda j, i: (i, j)))
            args.append(res)
        return pl.pallas_call(
            functools.partial(_mm_kernel, has_res=res is not None, has_scale=False, has_gain=False,
                              n_side=0, out_width=n),
            grid=(nj, ni),
            in_specs=in_specs,
            out_specs=[pl.BlockSpec((bm, bn), lambda j, i: (i, j))],
            out_shape=[jax.ShapeDtypeStruct((m, n), out_dtype)],
            compiler_params=_params("parallel", "parallel"),
            name="matmul_res" if res is not None else "matmul",
        )(*args)[0]
    in_specs = [pl.BlockSpec((bm, k), lambda i, j: (i, 0)),
                pl.BlockSpec((k, bn), lambda i, j: (0, w_block(j)))]
    args = [x, w]
    if res is not None:
        in_specs.append(pl.BlockSpec((bm, bn), lambda i, j: (i, j)))
        args.append(res)
    ss_spec = pl.BlockSpec((bm, LANES), lambda i, j: (i, 0))
    if row_ss is not None:
        in_specs.append(ss_spec)
        args.append(row_ss)
    out_specs = [pl.BlockSpec((bm, bn), lambda i, j: (i, j))]
    out_shape = [jax.ShapeDtypeStruct((m, n), out_dtype)]
    if norm_gain is not None:
        assert not side and cols == [(0, w.shape[1])]
        in_specs.append(pl.BlockSpec((1, bn), lambda i, j: (0, j)))
        args.append(norm_gain.reshape(1, n).astype(F32))
        out_specs += [pl.BlockSpec((bm, bn), lambda i, j: (i, j)), ss_spec]
        out_shape += [jax.ShapeDtypeStruct((m, n), BF16), jax.ShapeDtypeStruct((m, LANES), F32)]
    if emit_w:
        assert ni == 1 and not side and norm_gain is None
        out_specs.append(pl.BlockSpec((k, bn), lambda i, j: (0, j)))
        out_shape.append(jax.ShapeDtypeStruct((k, n), BF16))
    for sw in side:
        rows, width = sw.shape
        n_blocks = max(nb for nb in range(1, ni * nj + 1)
                       if rows % nb == 0 and (rows // nb) % BF16_SUBLANES == 0)
        spec = pl.BlockSpec((rows // n_blocks, width),
                            lambda i, j, n_blocks=n_blocks: (jnp.minimum(i * nj + j, n_blocks - 1), 0))
        in_specs.append(spec)
        args.append(sw)
        out_specs.append(spec)
        out_shape.append(jax.ShapeDtypeStruct((rows, width), BF16))
    outs = pl.pallas_call(
        functools.partial(_mm_kernel, has_res=res is not None, has_scale=row_ss is not None,
                          has_gain=norm_gain is not None, n_side=len(side), out_width=n,
                          emit_w=emit_w),
        grid=(ni, nj),
        in_specs=in_specs,
        out_specs=out_specs,
        out_shape=out_shape,
        compiler_params=(_params("arbitrary", "arbitrary") if side else
                         _params("parallel", "arbitrary") if norm_gain is not None else
                         _params("parallel", "parallel")),
        name="matmul_res" if res is not None else "matmul",
    )(*args)
    if side:
        return outs[0], list(outs[1:])
    return tuple(outs) if norm_gain is not None or emit_w else outs[0]


def _matmul_gated(a, b, wa, wb, proj, ga_col, gb_col):
    m, ka = a.shape
    kb = b.shape[1]
    n = wa.shape[1]
    bm = _tile(m, 1024, 8)
    bn = _tile(n, 1024, LANES)
    assert ga_col % bn == 0 and gb_col % bn == 0
    ga_blk, gb_blk = ga_col // bn, gb_col // bn
    return pl.pallas_call(
        _mm_gated_kernel,
        grid=(m // bm, n // bn),
        in_specs=[pl.BlockSpec((bm, ka), lambda i, j: (i, 0)),
                  pl.BlockSpec((bm, kb), lambda i, j: (i, 0)),
                  pl.BlockSpec((ka, bn), lambda i, j: (0, j)),
                  pl.BlockSpec((kb, bn), lambda i, j: (0, j)),
                  pl.BlockSpec((bm, bn), lambda i, j: (i, ga_blk + j)),
                  pl.BlockSpec((bm, bn), lambda i, j: (i, gb_blk + j))],
        out_specs=pl.BlockSpec((bm, bn), lambda i, j: (i, j)),
        out_shape=jax.ShapeDtypeStruct((m, n), BF16),
        compiler_params=_params("parallel", "parallel"),
        name="matmul_gated",
    )(a, b, wa, wb, proj, proj)


def _matmul_swiglu(x, row_ss, wg, wu):
    m, k = x.shape
    n = wg.shape[1]
    bm = _tile(m, 2048, 8)
    bn = _tile(n, 256, LANES)
    return pl.pallas_call(
        _mm_swiglu_kernel,
        grid=(m // bm, n // bn),
        in_specs=[pl.BlockSpec((bm, k), lambda i, j: (i, 0)),
                  pl.BlockSpec((k, bn), lambda i, j: (0, j)),
                  pl.BlockSpec((k, bn), lambda i, j: (0, j)),
                  pl.BlockSpec((bm, LANES), lambda i, j: (i, 0))],
        out_specs=pl.BlockSpec((bm, bn), lambda i, j: (i, j)),
        out_shape=jax.ShapeDtypeStruct((m, n), BF16),
        compiler_params=_params("parallel", "parallel"),
        name="matmul_swiglu",
    )(x, wg, wu, row_ss)


def _hgrn_tables(c, rows, pb):
    t = np.arange(rows)[:, None]
    s = np.arange(rows)[None, :]
    tri = ((t // c) == (s // c)) & (s <= t)
    t, s = t[:pb, :pb], s[:pb, :pb]
    masks = []
    l = 1
    while l < c:
        masks.append(((t // l) % 2 == 1) & ((s // l) == (t // l) - 1))
        l *= 2
    masks.append(t == s)
    return tri.astype(np.float32), np.stack(masks, 0).astype(np.float32)


def _hgrn_kernel(tri_ref, mask_ref, lbl_ref, q_ref, f_ref, v_ref, og_ref, gn_ref, s0_ref,
                 o_ref, sfin_ref, st_ref, oi_ref, qin_ref, u_ref, dec_ref,
                 *, c, n_chunks, rows, pb, layer, carry):
    t_idx = pl.program_id(2)
    d = HEAD_DIM_HGRN
    n_lv = c.bit_length() - 1
    g_chunks = rows // c
    n_pb = rows // pb
    nt = (((1,), (1,)), ((), ()))
    tn = (((0,), (0,)), ((), ()))

    if carry:
        @pl.when(t_idx == 0)
        def _():
            st_ref[...] = s0_ref[0, 0].T

    logits = lbl_ref[...].astype(F32)
    ex = jnp.exp(logits - jnp.max(logits, axis=0, keepdims=True))
    lb = jnp.sum(ex[:layer + 1], axis=0, keepdims=True) / jnp.sum(ex, axis=0, keepdims=True)
    gain = gn_ref[...].astype(F32)
    shape3 = (rows // SUBLANES, SUBLANES, d)
    sub = lax.broadcasted_iota(jnp.int32, (1, SUBLANES, d), 1)

    def block_rows(x, first, step, length):
        pieces = [jnp.broadcast_to(x[first + step * p:first + step * p + 1, :], (length, d))
                  for p in range(rows // length)]
        return pieces[0] if len(pieces) == 1 else jnp.concatenate(pieces, axis=0)

    def pair_ref(b, l):
        if l >= SUBLANES:
            return block_rows(b, l - 1, 2 * l, 2 * l)
        b3 = b.reshape(shape3)
        pick = lambda i: jnp.broadcast_to(b3[:, i:i + 1, :], shape3)
        r3 = pick(3) if l == 4 else jnp.where(sub < 4, pick(1), pick(5))
        return r3.reshape(rows, d)

    def odd_half(x_odd, x_even, l):
        if l >= SUBLANES:
            pieces = [(x_odd if p % 2 else x_even)[p * l:(p + 1) * l] for p in range(rows // l)]
            return jnp.concatenate(pieces, axis=0)
        return jnp.where((sub & l) != 0, x_odd.reshape(shape3), x_even.reshape(shape3)).reshape(rows, d)

    def phase_a_head(bi):
        rsel = pl.ds(pl.multiple_of(bi * rows, rows), rows)
        f = lb + (1.0 - lb) * jax.nn.sigmoid(f_ref[0, rsel, :])
        g = jnp.log(f)
        k = 1.0 - f
        qb = q_ref[0, rsel, :]
        q = qb.astype(F32)
        v = v_ref[0, rsel, :]

        g_hi = g.astype(BF16)
        r1 = g - g_hi.astype(F32)
        g_mid = r1.astype(BF16)
        g_lo = (r1 - g_mid.astype(F32)).astype(BF16)
        b3 = jnp.dot(tri_ref[...], jnp.concatenate([g_hi, g_mid, g_lo], axis=1),
                     preferred_element_type=F32)
        b = b3[:, 0:d] + b3[:, d:2 * d] + b3[:, 2 * d:3 * d]
        return f, k, qb, q, v, b

    def phase_a_tail(bi, head):
        rsel = pl.ds(pl.multiple_of(bi * rows, rows), rows)
        f, k, qb, q, v, b = head

        def scores(zl, zr, li):
            p = lax.dot_general(zl, zr, nt, preferred_element_type=F32)
            return [mask_ref[li] * p[i * pb:(i + 1) * pb, i * pb:(i + 1) * pb] for i in range(n_pb)]

        a = scores(qb, k.astype(BF16), n_lv)
        z = odd_half(q * f, k, 1).astype(BF16)
        a = [x + y for x, y in zip(a, scores(z, z, 0))]
        for li in range(1, n_lv):
            l = 1 << li
            w = jnp.exp(-jnp.abs(b - pair_ref(b, l)))
            z = (odd_half(q, k, l) * w).astype(BF16)
            a = [x + y for x, y in zip(a, scores(z, z, li))]
        for i in range(n_pb):
            oi_ref[pl.ds(pl.multiple_of(bi * rows + i * pb, pb), pb), :] = jnp.dot(
                a[i].astype(BF16), v[i * pb:(i + 1) * pb], preferred_element_type=F32)

        dec = jnp.exp(b)
        qin_ref[rsel, :] = (q * dec).astype(BF16)
        k_out = (k * jnp.exp(block_rows(b, c - 1, c, c) - b)).astype(BF16)
        for ci in range(g_chunks):
            cr = slice(ci * c, (ci + 1) * c)
            u_ref[bi * g_chunks + ci] = lax.dot_general(v[cr], k_out[cr], tn, preferred_element_type=F32)
            dec_ref[pl.ds(pl.multiple_of((bi * g_chunks + ci) * SUBLANES, SUBLANES), SUBLANES), :] = (
                jnp.broadcast_to(dec[ci * c + c - 1:ci * c + c, :], (SUBLANES, d)))

    def phase_bc(bi):
        st = st_ref[...] if carry else None
        outs = []
        for ci in range(g_chunks):
            chunk = bi * g_chunks + ci
            rsel = pl.ds(pl.multiple_of(chunk * c, c), c)
            if not carry:
                st = s0_ref[chunk, 0].T
            outs.append(oi_ref[rsel, :] + lax.dot_general(
                qin_ref[rsel, :], st.astype(BF16), nt, preferred_element_type=F32))
            st = st * dec_ref[pl.ds(pl.multiple_of(chunk * SUBLANES, SUBLANES), 1), :] + u_ref[chunk]
            if not carry:
                sfin_ref[chunk, 0] = st.T
        if carry:
            st_ref[...] = st
        rsel = pl.ds(pl.multiple_of(bi * rows, rows), rows)
        o = outs[0] if g_chunks == 1 else jnp.concatenate(outs, axis=0)
        y = o * lax.rsqrt(jnp.mean(o * o, axis=-1, keepdims=True) + EPS) * gain
        og = og_ref[0, rsel, :].astype(F32)
        o_ref[0, rsel, :] = (y * (og * jax.nn.sigmoid(og))).astype(o_ref.dtype)

    n_batches = n_chunks // g_chunks
    head = phase_a_head(0)
    for bi in range(n_batches):
        next_head = phase_a_head(bi + 1) if bi + 1 < n_batches else None
        phase_a_tail(bi, head)
        if bi > 0:
            phase_bc(bi - 1)
        head = next_head
    phase_bc(n_batches - 1)

    if carry:
        @pl.when(t_idx == pl.num_programs(2) - 1)
        def _():
            sfin_ref[0, 0] = st_ref[...].T


def _hgrn(proj3, fpre3, lb_logits, gain, s0, layer, q_col, v_col, og_col):
    bsz, t, _ = proj3.shape
    hw = fpre3.shape[-1]
    d = HEAD_DIM_HGRN
    nh = hw // d
    c = CHUNK if t % CHUNK == 0 else t
    assert c & (c - 1) == 0 and c >= 2 * SUBLANES and t % c == 0
    carry = t > c
    if not carry:
        out, s_fin = _hgrn_call(proj3.reshape(1, bsz * t, -1), fpre3.reshape(1, bsz * t, hw), lb_logits,
                                gain, s0, layer, q_col, v_col, og_col, c=c, carry=False)
        return out.reshape(bsz, t, hw), s_fin
    return _hgrn_call(proj3, fpre3, lb_logits, gain, s0, layer, q_col, v_col, og_col, c=c, carry=True)


def _hgrn_call(proj3, fpre3, lb_logits, gain, s0, layer, q_col, v_col, og_col, *, c, carry):
    bsz, t, _ = proj3.shape
    hw = fpre3.shape[-1]
    d = HEAD_DIM_HGRN
    nh = hw // d
    tt = _tile(t, 4096, c)
    n_chunks = tt // c
    rows = _tile(tt, 2 * LANES, c)
    pb = min(rows, LANES)
    tri, masks = _hgrn_tables(c, rows, pb)
    qb, vb, ob = q_col // d, v_col // d, og_col // d
    kern = functools.partial(_hgrn_kernel, c=c, n_chunks=n_chunks, rows=rows, pb=pb, layer=layer,
                             carry=carry)
    nl = lb_logits.shape[0]
    if carry:
        state_spec = pl.BlockSpec((1, 1, d, d), lambda b, h, i: (b, h, 0, 0))
    else:
        state_spec = pl.BlockSpec((n_chunks, 1, d, d), lambda b, h, i: (i, h, 0, 0))
    return pl.pallas_call(
        kern,
        grid=(bsz, nh, t // tt),
        in_specs=[pl.BlockSpec((rows, rows), lambda b, h, i: (0, 0)),
                  pl.BlockSpec(masks.shape, lambda b, h, i: (0, 0, 0)),
                  pl.BlockSpec((nl, d), lambda b, h, i: (0, h)),
                  pl.BlockSpec((1, tt, d), lambda b, h, i: (b, i, qb + h)),
                  pl.BlockSpec((1, tt, d), lambda b, h, i: (b, i, h)),
                  pl.BlockSpec((1, tt, d), lambda b, h, i: (b, i, vb + h)),
                  pl.BlockSpec((1, tt, d), lambda b, h, i: (b, i, ob + h)),
                  pl.BlockSpec((1, d), lambda b, h, i: (0, h)),
                  state_spec],
        out_specs=[pl.BlockSpec((1, tt, d), lambda b, h, i: (b, i, h)),
                   state_spec],
        out_shape=[jax.ShapeDtypeStruct((bsz, t, hw), BF16),
                   jax.ShapeDtypeStruct(s0.shape, F32)],
        scratch_shapes=[pltpu.VMEM((d, d), F32),
                        pltpu.VMEM((tt, d), F32),
                        pltpu.VMEM((tt, d), BF16),
                        pltpu.VMEM((n_chunks, d, d), F32),
                        pltpu.VMEM((n_chunks * SUBLANES, d), F32)],
        compiler_params=_params("parallel", "parallel", "arbitrary"),
        name="hgrn2",
    )(jnp.asarray(tri, BF16), jnp.asarray(masks, F32), lb_logits.astype(F32),
      proj3, fpre3, proj3, proj3, gain.reshape(1, hw).astype(F32), s0.astype(F32))


CONV_PAD = 8
CONV_BLOCK_ELEMS = 1024 * 1024


def _conv_kernel(cc_ref, ch_ref, cb_ref, w_ref, buf_ref, o_ref, nbuf_ref, u_ref, *, t, k, rb):
    u_ref[CONV_PAD - (k - 1):CONV_PAD, :] = buf_ref[0].astype(F32)
    for r in range(0, t, rb):
        u_ref[CONV_PAD + r:CONV_PAD + r + rb, :] = (
            cc_ref[0, r:r + rb, :].astype(F32) * ch_ref[0, r:r + rb, :].astype(F32))
    w = w_ref[...].astype(F32)
    for r in range(0, t, rb):
        z = w[0:1, :] * u_ref[CONV_PAD - (k - 1) + r:CONV_PAD - (k - 1) + r + rb, :]
        for j in range(1, k):
            s = CONV_PAD - (k - 1) + j + r
            z = z + w[j:j + 1, :] * u_ref[s:s + rb, :]
        o_ref[0, r:r + rb, :] = (cb_ref[0, r:r + rb, :].astype(F32) * z).astype(o_ref.dtype)
    nbuf_ref[0] = u_ref[CONV_PAD + t - (k - 1):CONV_PAD + t, :]


def _conv(proj3, conv_w, buf, ch_col, cb_col, cc_col):
    bsz, t, _ = proj3.shape
    k, cw = conv_w.shape
    assert k - 1 <= CONV_PAD
    d = _tile(cw, max(LANES, CONV_BLOCK_ELEMS // t), LANES)
    rb = _tile(t, max(8, 512 * LANES // d), 8)
    kern = functools.partial(_conv_kernel, t=t, k=k, rb=rb)
    return pl.pallas_call(
        kern,
        grid=(bsz, cw // d),
        in_specs=[pl.BlockSpec((1, t, d), lambda b, j: (b, 0, cc_col // d + j)),
                  pl.BlockSpec((1, t, d), lambda b, j: (b, 0, ch_col // d + j)),
                  pl.BlockSpec((1, t, d), lambda b, j: (b, 0, cb_col // d + j)),
                  pl.BlockSpec((k, d), lambda b, j: (0, j)),
                  pl.BlockSpec((1, k - 1, d), lambda b, j: (b, 0, j))],
        out_specs=[pl.BlockSpec((1, t, d), lambda b, j: (b, 0, j)),
                   pl.BlockSpec((1, k - 1, d), lambda b, j: (b, 0, j))],
        out_shape=[jax.ShapeDtypeStruct((bsz, t, cw), BF16),
                   jax.ShapeDtypeStruct((bsz, k - 1, cw), F32)],
        scratch_shapes=[pltpu.VMEM((CONV_PAD + t, d), F32)],
        compiler_params=_params("parallel", "parallel"),
        name="short_conv",
    )(proj3, proj3, proj3, conv_w.astype(F32), buf.astype(F32))


def _attn_kernel(q_ref, k_ref, v_ref, o_ref, kb_ref, vb_ref, *, scale, sub_rows):
    nt = (((1,), (1,)), ((), ()))

    @pl.when(pl.program_id(2) == 0)
    def _():
        kb_ref[...] = k_ref[0].astype(BF16)
        vb_ref[...] = v_ref[0].astype(BF16)

    subs = [pl.ds(r, sub_rows) for r in range(0, q_ref.shape[1], sub_rows)]
    scores = [lax.dot_general(q_ref[0, r, :], kb_ref[...], nt, preferred_element_type=F32) * scale
              for r in subs]
    probs = []
    for s in scores:
        p = jnp.exp(s - jnp.max(s, axis=-1, keepdims=True))
        probs.append((p / jnp.sum(p, axis=-1, keepdims=True)).astype(BF16))
    for r, p in zip(subs, probs):
        o_ref[0, r, :] = jnp.dot(p, vb_ref[...], preferred_element_type=F32).astype(o_ref.dtype)


def _attention(q3, mk4, mv4, mem_layer, n_heads):
    bsz, t, dm = q3.shape
    n_mem = mk4.shape[2]
    hd = dm // n_heads
    mem_spec = pl.BlockSpec((None, 1, n_mem, hd), lambda b, h, i: (mem_layer, b, 0, h))
    bt = _tile(t, 2048, 8)
    kern = functools.partial(_attn_kernel, scale=float(hd) ** -0.5, sub_rows=_tile(bt, 512, 8))
    return pl.pallas_call(
        kern,
        grid=(bsz, n_heads, t // bt),
        in_specs=[pl.BlockSpec((1, bt, hd), lambda b, h, i: (b, i, h)), mem_spec, mem_spec],
        out_specs=pl.BlockSpec((1, bt, hd), lambda b, h, i: (b, i, h)),
        out_shape=jax.ShapeDtypeStruct((bsz, t, dm), BF16),
        scratch_shapes=[pltpu.VMEM((n_mem, hd), BF16), pltpu.VMEM((n_mem, hd), BF16)],
        compiler_params=_params("parallel", "parallel", "arbitrary"),
        name="cross_attention",
    )(q3, mk4, mv4)


def _project(x, wts):
    hw = wts["hg_norm"].shape[0]
    h = _rmsnorm(x, wts["norm_mix"], BF16)
    if "w_in" in wts:
        pw = wts["w_in"].shape[1]
        proj, w_rest = _matmul(h, wts["w_in"], BF16, bn_pref=512, cols=[(0, hw), (2 * hw, pw - 2 * hw)],
                               emit_w=True)
        fpre, w_f = _matmul(h, wts["w_in"], F32, bn_pref=512, cols=[(hw, hw)], emit_w=True)
        wts = {k: v for k, v in wts.items() if k != "w_in"} | {"w_in_rest": w_rest, "w_in_f": w_f}
    elif "f32" in wts:
        names = list(wts["f32"])
        proj, cast = _matmul(h, wts["w_in_rest"], BF16, side=[wts["f32"][n] for n in names])
        fpre = _matmul(h, wts["w_in_f"], F32)
        wts = {k: v for k, v in wts.items() if k != "f32"} | dict(zip(names, cast))
    else:
        proj = _matmul(h, wts["w_in_rest"], BF16)
        fpre = _matmul(h, wts["w_in_f"], F32)
    return proj, fpre, wts


def _layer(x, proj, fpre, bsz, t, wts, lb_logits, layer, s0, buf, mem, n_xa_heads):
    m, dm = x.shape
    hw = wts["hg_norm"].shape[0]
    cw = wts["conv_w"].shape[1]
    q_col, v_col, og_col = 0, hw, 2 * hw
    ch_col, cb_col, cc_col = 3 * hw, 3 * hw + cw, 3 * hw + 2 * cw
    ga_col, gb_col = 3 * hw + 3 * cw, 3 * hw + 3 * cw + dm
    proj3 = proj.reshape(bsz, t, proj.shape[1])
    a_in, s_fin = _hgrn(proj3, fpre.reshape(bsz, t, hw), lb_logits, wts["hg_norm"], s0, layer,
                        q_col, v_col, og_col)
    bz, new_buf = _conv(proj3, wts["conv_w"], buf, ch_col, cb_col, cc_col)
    merged = _matmul_gated(a_in.reshape(m, hw), bz.reshape(m, cw), wts["w_a"], wts["w_b"],
                           proj, ga_col, gb_col)
    x, xg, ss = _matmul(merged, wts["w_o"], F32, bn_pref=512, res=x, norm_gain=wts["norm_xattn"])
    qx = _matmul(xg, wts["w_xq"], BF16, row_ss=ss)
    att = _attention(qx.reshape(bsz, t, dm), *mem, n_xa_heads)
    x, xg, ss = _matmul(att.reshape(m, dm), wts["w_xo"], F32, bn_pref=512, res=x,
                        norm_gain=wts["norm_ffn"])
    act = _matmul_swiglu(xg, ss, wts["w_gate"], wts["w_up"])
    x = _matmul(act, wts["w_down"], F32, bm_pref=512, bn_pref=512, res=x, cols_outer=True)
    return x, s_fin, new_buf


def kernel(x_prompt, x_sample, cache_mem_k, cache_mem_v, state_hgrn, state_conv, mem_prompt, norm_mix, w_in, lb_logits, hg_norm, conv_w, w_a, w_b, w_o, norm_xattn, norm_mem, w_xq, w_xk, w_xv, w_xo, norm_ffn, w_gate, w_up, w_down, norm_final):
    depth = norm_mix.shape[0]
    bp, tp, dm = x_prompt.shape
    bs, ts, _ = x_sample.shape
    hw = hg_norm.shape[1]
    n_mem = mem_prompt.shape[1]
    n_xa_heads = cache_mem_k.shape[3]
    nh, dk, dv = state_hgrn.shape[2:]
    assert dk == HEAD_DIM_HGRN and dv == HEAD_DIM_HGRN and nh * dk == hw

    xp = x_prompt.reshape(bp * tp, dm)
    xs = x_sample.reshape(bs * ts, dm)
    outs = {k: [] for k in ("mk", "mv", "sp", "cp", "ss", "cs")}
    for l in range(depth):
        w_in_l = w_in[l]
        wts = {
            "norm_mix": norm_mix[l], "hg_norm": hg_norm[l], "conv_w": conv_w[l],
            "norm_xattn": norm_xattn[l], "norm_ffn": norm_ffn[l],
            "w_in": w_in_l,
            "f32": {"w_a": w_a[l], "w_b": w_b[l], "w_o": w_o[l], "w_xq": w_xq[l], "w_xo": w_xo[l],
                    "w_gate": w_gate[l], "w_up": w_up[l], "w_down": w_down[l]},
        }
        mem_n = _rmsnorm(mem_prompt.reshape(bp * n_mem, dm), norm_mem[l], BF16)
        mk_p = _matmul(mem_n, w_xk[l], F32)
        mv_p = _matmul(mem_n, w_xv[l], F32)
        s0 = jnp.zeros((bp, nh, dk, dv), F32)
        buf0 = jnp.zeros((bp, conv_w.shape[1] - 1, conv_w.shape[2]), F32)
        mem_p = (mk_p.reshape(1, bp, n_mem, dm), mv_p.reshape(1, bp, n_mem, dm), 0)
        mem_s = (cache_mem_k.reshape(depth, bs, n_mem, dm), cache_mem_v.reshape(depth, bs, n_mem, dm), l)
        proj_s, fpre_s, wts = _project(xs, wts)
        proj_p, fpre_p, wts = _project(xp, wts)
        xp, s_p, buf_p = _layer(xp, proj_p, fpre_p, bp, tp, wts, lb_logits, l, s0, buf0, mem_p, n_xa_heads)
        xs, s_s, buf_s = _layer(xs, proj_s, fpre_s, bs, ts, wts, lb_logits, l, state_hgrn[l],
                                state_conv[l], mem_s, n_xa_heads)
        outs["mk"].append(mk_p.reshape(bp, n_mem, n_xa_heads, dm // n_xa_heads))
        outs["mv"].append(mv_p.reshape(bp, n_mem, n_xa_heads, dm // n_xa_heads))
        outs["sp"].append(s_p)
        outs["cp"].append(buf_p)
        outs["ss"].append(s_s)
        outs["cs"].append(buf_s)
    y_prompt = _rmsnorm(xp, norm_final, F32).reshape(bp, tp, dm)
    y_sample = _rmsnorm(xs, norm_final, F32).reshape(bs, ts, dm)
    return (y_prompt, y_sample, jnp.stack(outs["mk"]), jnp.stack(outs["mv"]), jnp.stack(outs["sp"]),
            jnp.stack(outs["cp"]), jnp.stack(outs["ss"]), jnp.stack(outs["cs"]))
```

```python
import functools

import numpy as np
import jax
import jax.numpy as jnp
from jax import lax
from jax.experimental import pallas as pl
from jax.experimental.pallas import tpu as pltpu

EPS = 1e-6
LANES = 128
SUBLANES = 8
HEAD_DIM_HGRN = 128
CHUNK = 64
VMEM_LIMIT_BYTES = 60 * 1024 * 1024
BF16_SUBLANES = 16
BF16 = jnp.bfloat16
F32 = jnp.float32

TILE_MATMUL = (1024, 1024)
TILE_MATMUL_WIDE_OUT = 512
TILE_SWIGLU = (2048, 256)
TILE_DOWN = (512, 512)
ROWS_RMSNORM = 512
ROWS_HGRN_BLOCK = 4096
ROWS_ATTN = 2048
ROWS_SUB = 512


def _params(*semantics):
    return pltpu.CompilerParams(dimension_semantics=semantics, vmem_limit_bytes=VMEM_LIMIT_BYTES)


def _tile(n, pref, mult):
    if n <= pref:
        return n
    t = (pref // mult) * mult
    while t >= mult:
        if n % t == 0:
            return t
        t -= mult
    raise ValueError(f"no tile for {n} (pref {pref}, mult {mult})")


def _rmsnorm_kernel(x_ref, g_ref, o_ref):
    x = x_ref[...].astype(F32)
    y = x * lax.rsqrt(jnp.mean(x * x, axis=-1, keepdims=True) + EPS)
    o_ref[...] = (y * g_ref[...]).astype(o_ref.dtype)


def _rmsnorm(x, g, out_dtype):
    m, d = x.shape
    bm = _tile(m, ROWS_RMSNORM, SUBLANES)
    return pl.pallas_call(
        _rmsnorm_kernel,
        grid=(m // bm,),
        in_specs=[pl.BlockSpec((bm, d), lambda i: (i, 0)),
                  pl.BlockSpec((1, d), lambda i: (0, 0))],
        out_specs=pl.BlockSpec((bm, d), lambda i: (i, 0)),
        out_shape=jax.ShapeDtypeStruct((m, d), out_dtype),
        compiler_params=_params("parallel"),
        name="rmsnorm",
    )(x, g.reshape(1, d).astype(F32))


def _row_scale(rs_ref, n_lanes):
    return pltpu.repeat(rs_ref[...], n_lanes // LANES, axis=1)


def _row_blocks(bm):
    sub = ROWS_SUB if bm % ROWS_SUB == 0 else bm
    return [pl.ds(r, sub) for r in range(0, bm, sub)]


def _mm_kernel(*refs, has_res, has_scale, has_gain, n_side, out_width, emit_w=False):
    refs = list(refs)
    x_ref, w_ref = refs.pop(0), refs.pop(0)
    r_ref = refs.pop(0) if has_res else None
    ss_in_ref = refs.pop(0) if has_scale else None
    g_ref = refs.pop(0) if has_gain else None
    src = [refs.pop(0) for _ in range(n_side)]
    o_ref = refs.pop(0)
    og_ref, ss_ref = (refs.pop(0), refs.pop(0)) if has_gain else (None, None)
    wcopy_ref = refs.pop(0) if emit_w else None
    dst = refs
    if has_gain:
        @pl.when(pl.program_id(1) == 0)
        def _():
            ss_ref[...] = jnp.zeros_like(ss_ref)

    w = w_ref[...].astype(BF16)
    if emit_w:
        wcopy_ref[...] = w
    for rows in _row_blocks(x_ref.shape[0]):
        acc = jnp.dot(x_ref[rows, :], w, preferred_element_type=F32)
        if has_scale:
            acc = acc * _row_scale(ss_in_ref.at[rows], acc.shape[1])
        if has_res:
            acc = r_ref[rows, :] + acc
        o_ref[rows, :] = acc.astype(o_ref.dtype)
        if has_gain:
            og_ref[rows, :] = (acc * g_ref[...]).astype(og_ref.dtype)
            sq = acc * acc
            ss_ref[rows, :] += functools.reduce(
                jnp.add, [sq[:, c:c + LANES] for c in range(0, sq.shape[1], LANES)])

    if has_gain:
        @pl.when(pl.program_id(1) == pl.num_programs(1) - 1)
        def _():
            total = jnp.sum(ss_ref[...], axis=-1, keepdims=True)
            ss_ref[...] = jnp.broadcast_to(lax.rsqrt(total * (1.0 / out_width) + EPS), ss_ref.shape)
    for src_ref, dst_ref in zip(src, dst):
        dst_ref[...] = src_ref[...].astype(BF16)


def _mm_gated_kernel(a_ref, b_ref, wa_ref, wb_ref, ga_ref, gb_ref, o_ref):
    for rows in _row_blocks(a_ref.shape[0]):
        a = jnp.dot(a_ref[rows, :], wa_ref[...], preferred_element_type=F32)
        b = jnp.dot(b_ref[rows, :], wb_ref[...], preferred_element_type=F32)
        ga = jax.nn.sigmoid(ga_ref[rows, :].astype(F32))
        gb = jax.nn.sigmoid(gb_ref[rows, :].astype(F32))
        o_ref[rows, :] = (ga * a + gb * b).astype(o_ref.dtype)


def _mm_swiglu_kernel(x_ref, wg_ref, wu_ref, ss_ref, o_ref):
    for rows in _row_blocks(x_ref.shape[0]):
        x = x_ref[rows, :]
        scale = _row_scale(ss_ref.at[rows], wg_ref.shape[1])
        g = jnp.dot(x, wg_ref[...], preferred_element_type=F32) * scale
        u = jnp.dot(x, wu_ref[...], preferred_element_type=F32) * scale
        o_ref[rows, :] = (g * jax.nn.sigmoid(g) * u).astype(o_ref.dtype)


def _matmul(x, w, out_dtype, *, bm_pref=TILE_MATMUL[0], bn_pref=TILE_MATMUL[1], res=None, cols=None,
            side=(), row_ss=None, norm_gain=None, cols_outer=False, emit_w=False):
    m, k = x.shape
    cols = cols or [(0, w.shape[1])]
    n = sum(width for _, width in cols)
    bm = _tile(m, bm_pref, SUBLANES)
    bn = _tile(int(functools.reduce(np.gcd, [v for seg in cols for v in seg if v])), bn_pref, LANES)
    steps, out_blk = [], 0
    for start, width in cols:
        steps.append((out_blk, start // bn - out_blk))
        out_blk += width // bn

    def w_block(j):
        blk = j + steps[0][1]
        for (first, shift), (_, prev) in zip(steps[1:], steps[:-1]):
            blk = blk + jnp.where(j >= first, shift - prev, 0)
        return blk

    ni, nj = m // bm, n // bn
    if cols_outer:
        assert not side and norm_gain is None and row_ss is None
        in_specs = [pl.BlockSpec((bm, k), lambda j, i: (i, 0)),
                    pl.BlockSpec((k, bn), lambda j, i: (0, w_block(j)))]
        args = [x, w]
        if res is not None:
            in_specs.append(pl.BlockSpec((bm, bn), lambda j, i: (i, j)))
            args.append(res)
        return pl.pallas_call(
            functools.partial(_mm_kernel, has_res=res is not None, has_scale=False, has_gain=False,
                              n_side=0, out_width=n),
            grid=(nj, ni),
            in_specs=in_specs,
            out_specs=[pl.BlockSpec((bm, bn), lambda j, i: (i, j))],
            out_shape=[jax.ShapeDtypeStruct((m, n), out_dtype)],
            compiler_params=_params("parallel", "parallel"),
            name="matmul_res" if res is not None else "matmul",
        )(*args)[0]
    in_specs = [pl.BlockSpec((bm, k), lambda i, j: (i, 0)),
                pl.BlockSpec((k, bn), lambda i, j: (0, w_block(j)))]
    args = [x, w]
    if res is not None:
        in_specs.append(pl.BlockSpec((bm, bn), lambda i, j: (i, j)))
        args.append(res)
    ss_spec = pl.BlockSpec((bm, LANES), lambda i, j: (i, 0))
    if row_ss is not None:
        in_specs.append(ss_spec)
        args.append(row_ss)
    out_specs = [pl.BlockSpec((bm, bn), lambda i, j: (i, j))]
    out_shape = [jax.ShapeDtypeStruct((m, n), out_dtype)]
    if norm_gain is not None:
        assert not side and cols == [(0, w.shape[1])]
        in_specs.append(pl.BlockSpec((1, bn), lambda i, j: (0, j)))
        args.append(norm_gain.reshape(1, n).astype(F32))
        out_specs += [pl.BlockSpec((bm, bn), lambda i, j: (i, j)), ss_spec]
        out_shape += [jax.ShapeDtypeStruct((m, n), BF16), jax.ShapeDtypeStruct((m, LANES), F32)]
    if emit_w:
        assert ni == 1 and not side and norm_gain is None
        out_specs.append(pl.BlockSpec((k, bn), lambda i, j: (0, j)))
        out_shape.append(jax.ShapeDtypeStruct((k, n), BF16))
    for sw in side:
        rows, width = sw.shape
        n_blocks = max(nb for nb in range(1, ni * nj + 1)
                       if rows % nb == 0 and (rows // nb) % BF16_SUBLANES == 0)
        spec = pl.BlockSpec((rows // n_blocks, width),
                            lambda i, j, n_blocks=n_blocks: (jnp.minimum(i * nj + j, n_blocks - 1), 0))
        in_specs.append(spec)
        args.append(sw)
        out_specs.append(spec)
        out_shape.append(jax.ShapeDtypeStruct((rows, width), BF16))
    outs = pl.pallas_call(
        functools.partial(_mm_kernel, has_res=res is not None, has_scale=row_ss is not None,
                          has_gain=norm_gain is not None, n_side=len(side), out_width=n,
                          emit_w=emit_w),
        grid=(ni, nj),
        in_specs=in_specs,
        out_specs=out_specs,
        out_shape=out_shape,
        compiler_params=(_params("arbitrary", "arbitrary") if side else
                         _params("parallel", "arbitrary") if norm_gain is not None else
                         _params("parallel", "parallel")),
        name="matmul_res" if res is not None else "matmul",
    )(*args)
    if side:
        return outs[0], list(outs[1:])
    return tuple(outs) if norm_gain is not None or emit_w else outs[0]


def _matmul_gated(a, b, wa, wb, proj, ga_col, gb_col):
    m, ka = a.shape
    kb = b.shape[1]
    n = wa.shape[1]
    bm = _tile(m, TILE_MATMUL[0], SUBLANES)
    bn = _tile(n, TILE_MATMUL[1], LANES)
    assert ga_col % bn == 0 and gb_col % bn == 0
    ga_blk, gb_blk = ga_col // bn, gb_col // bn
    return pl.pallas_call(
        _mm_gated_kernel,
        grid=(m // bm, n // bn),
        in_specs=[pl.BlockSpec((bm, ka), lambda i, j: (i, 0)),
                  pl.BlockSpec((bm, kb), lambda i, j: (i, 0)),
                  pl.BlockSpec((ka, bn), lambda i, j: (0, j)),
                  pl.BlockSpec((kb, bn), lambda i, j: (0, j)),
                  pl.BlockSpec((bm, bn), lambda i, j: (i, ga_blk + j)),
                  pl.BlockSpec((bm, bn), lambda i, j: (i, gb_blk + j))],
        out_specs=pl.BlockSpec((bm, bn), lambda i, j: (i, j)),
        out_shape=jax.ShapeDtypeStruct((m, n), BF16),
        compiler_params=_params("parallel", "parallel"),
        name="matmul_gated",
    )(a, b, wa, wb, proj, proj)


def _matmul_swiglu(x, row_ss, wg, wu):
    m, k = x.shape
    n = wg.shape[1]
    bm = _tile(m, TILE_SWIGLU[0], SUBLANES)
    bn = _tile(n, TILE_SWIGLU[1], LANES)
    return pl.pallas_call(
        _mm_swiglu_kernel,
        grid=(m // bm, n // bn),
        in_specs=[pl.BlockSpec((bm, k), lambda i, j: (i, 0)),
                  pl.BlockSpec((k, bn), lambda i, j: (0, j)),
                  pl.BlockSpec((k, bn), lambda i, j: (0, j)),
                  pl.BlockSpec((bm, LANES), lambda i, j: (i, 0))],
        out_specs=pl.BlockSpec((bm, bn), lambda i, j: (i, j)),
        out_shape=jax.ShapeDtypeStruct((m, n), BF16),
        compiler_params=_params("parallel", "parallel"),
        name="matmul_swiglu",
    )(x, wg, wu, row_ss)


def _hgrn_tables(c, rows, pb):
    t = np.arange(rows)[:, None]
    s = np.arange(rows)[None, :]
    tri = ((t // c) == (s // c)) & (s <= t)
    t, s = t[:pb, :pb], s[:pb, :pb]
    masks = []
    l = 1
    while l < c:
        masks.append(((t // l) % 2 == 1) & ((s // l) == (t // l) - 1))
        l *= 2
    masks.append(t == s)
    return tri.astype(np.float32), np.stack(masks, 0).astype(np.float32)


def _hgrn_kernel(tri_ref, mask_ref, lbl_ref, q_ref, f_ref, v_ref, og_ref, gn_ref, s0_ref,
                 o_ref, sfin_ref, st_ref, oi_ref, qin_ref, u_ref, dec_ref,
                 *, c, n_chunks, rows, pb, layer, carry):
    t_idx = pl.program_id(2)
    d = HEAD_DIM_HGRN
    n_lv = c.bit_length() - 1
    g_chunks = rows // c
    n_pb = rows // pb
    nt = (((1,), (1,)), ((), ()))
    tn = (((0,), (0,)), ((), ()))

    if carry:
        @pl.when(t_idx == 0)
        def _():
            st_ref[...] = s0_ref[0, 0].T

    logits = lbl_ref[...].astype(F32)
    ex = jnp.exp(logits - jnp.max(logits, axis=0, keepdims=True))
    lb = jnp.sum(ex[:layer + 1], axis=0, keepdims=True) / jnp.sum(ex, axis=0, keepdims=True)
    gain = gn_ref[...].astype(F32)
    shape3 = (rows // SUBLANES, SUBLANES, d)
    sub = lax.broadcasted_iota(jnp.int32, (1, SUBLANES, d), 1)

    def block_rows(x, first, step, length):
        pieces = [jnp.broadcast_to(x[first + step * p:first + step * p + 1, :], (length, d))
                  for p in range(rows // length)]
        return pieces[0] if len(pieces) == 1 else jnp.concatenate(pieces, axis=0)

    def pair_ref(b, l):
        if l >= SUBLANES:
            return block_rows(b, l - 1, 2 * l, 2 * l)
        b3 = b.reshape(shape3)
        pick = lambda i: jnp.broadcast_to(b3[:, i:i + 1, :], shape3)
        r3 = pick(3) if l == 4 else jnp.where(sub < 4, pick(1), pick(5))
        return r3.reshape(rows, d)

    def odd_half(x_odd, x_even, l):
        if l >= SUBLANES:
            pieces = [(x_odd if p % 2 else x_even)[p * l:(p + 1) * l] for p in range(rows // l)]
            return jnp.concatenate(pieces, axis=0)
        return jnp.where((sub & l) != 0, x_odd.reshape(shape3), x_even.reshape(shape3)).reshape(rows, d)

    def phase_a_head(bi):
        rsel = pl.ds(pl.multiple_of(bi * rows, rows), rows)
        f = lb + (1.0 - lb) * jax.nn.sigmoid(f_ref[0, rsel, :])
        g = jnp.log(f)
        k = 1.0 - f
        qb = q_ref[0, rsel, :]
        q = qb.astype(F32)
        v = v_ref[0, rsel, :]

        g_hi = g.astype(BF16)
        r1 = g - g_hi.astype(F32)
        g_mid = r1.astype(BF16)
        g_lo = (r1 - g_mid.astype(F32)).astype(BF16)
        b3 = jnp.dot(tri_ref[...], jnp.concatenate([g_hi, g_mid, g_lo], axis=1),
                     preferred_element_type=F32)
        b = b3[:, 0:d] + b3[:, d:2 * d] + b3[:, 2 * d:3 * d]
        return f, k, qb, q, v, b

    def phase_a_tail(bi, head):
        rsel = pl.ds(pl.multiple_of(bi * rows, rows), rows)
        f, k, qb, q, v, b = head

        def scores(zl, zr, li):
            p = lax.dot_general(zl, zr, nt, preferred_element_type=F32)
            return [mask_ref[li] * p[i * pb:(i + 1) * pb, i * pb:(i + 1) * pb] for i in range(n_pb)]

        a = scores(qb, k.astype(BF16), n_lv)
        z = odd_half(q * f, k, 1).astype(BF16)
        a = [x + y for x, y in zip(a, scores(z, z, 0))]
        for li in range(1, n_lv):
            l = 1 << li
            w = jnp.exp(-jnp.abs(b - pair_ref(b, l)))
            z = (odd_half(q, k, l) * w).astype(BF16)
            a = [x + y for x, y in zip(a, scores(z, z, li))]
        for i in range(n_pb):
            oi_ref[pl.ds(pl.multiple_of(bi * rows + i * pb, pb), pb), :] = jnp.dot(
                a[i].astype(BF16), v[i * pb:(i + 1) * pb], preferred_element_type=F32)

        dec = jnp.exp(b)
        qin_ref[rsel, :] = (q * dec).astype(BF16)
        k_out = (k * jnp.exp(block_rows(b, c - 1, c, c) - b)).astype(BF16)
        for ci in range(g_chunks):
            cr = slice(ci * c, (ci + 1) * c)
            u_ref[bi * g_chunks + ci] = lax.dot_general(v[cr], k_out[cr], tn, preferred_element_type=F32)
            dec_ref[pl.ds(pl.multiple_of((bi * g_chunks + ci) * SUBLANES, SUBLANES), SUBLANES), :] = (
                jnp.broadcast_to(dec[ci * c + c - 1:ci * c + c, :], (SUBLANES, d)))

    def phase_bc(bi):
        st = st_ref[...] if carry else None
        outs = []
        for ci in range(g_chunks):
            chunk = bi * g_chunks + ci
            rsel = pl.ds(pl.multiple_of(chunk * c, c), c)
            if not carry:
                st = s0_ref[chunk, 0].T
            outs.append(oi_ref[rsel, :] + lax.dot_general(
                qin_ref[rsel, :], st.astype(BF16), nt, preferred_element_type=F32))
            st = st * dec_ref[pl.ds(pl.multiple_of(chunk * SUBLANES, SUBLANES), 1), :] + u_ref[chunk]
            if not carry:
                sfin_ref[chunk, 0] = st.T
        if carry:
            st_ref[...] = st
        rsel = pl.ds(pl.multiple_of(bi * rows, rows), rows)
        o = outs[0] if g_chunks == 1 else jnp.concatenate(outs, axis=0)
        y = o * lax.rsqrt(jnp.mean(o * o, axis=-1, keepdims=True) + EPS) * gain
        og = og_ref[0, rsel, :].astype(F32)
        o_ref[0, rsel, :] = (y * (og * jax.nn.sigmoid(og))).astype(o_ref.dtype)

    n_batches = n_chunks // g_chunks
    head = phase_a_head(0)
    for bi in range(n_batches):
        next_head = phase_a_head(bi + 1) if bi + 1 < n_batches else None
        phase_a_tail(bi, head)
        if bi > 0:
            phase_bc(bi - 1)
        head = next_head
    phase_bc(n_batches - 1)

    if carry:
        @pl.when(t_idx == pl.num_programs(2) - 1)
        def _():
            sfin_ref[0, 0] = st_ref[...].T


def _hgrn(proj3, fpre3, lb_logits, gain, s0, layer, q_col, v_col, og_col):
    bsz, t, _ = proj3.shape
    hw = fpre3.shape[-1]
    d = HEAD_DIM_HGRN
    nh = hw // d
    c = CHUNK if t % CHUNK == 0 else t
    assert c & (c - 1) == 0 and c >= 2 * SUBLANES and t % c == 0
    carry = t > c
    if not carry:
        out, s_fin = _hgrn_call(proj3.reshape(1, bsz * t, -1), fpre3.reshape(1, bsz * t, hw), lb_logits,
                                gain, s0, layer, q_col, v_col, og_col, c=c, carry=False)
        return out.reshape(bsz, t, hw), s_fin
    return _hgrn_call(proj3, fpre3, lb_logits, gain, s0, layer, q_col, v_col, og_col, c=c, carry=True)


def _hgrn_call(proj3, fpre3, lb_logits, gain, s0, layer, q_col, v_col, og_col, *, c, carry):
    bsz, t, _ = proj3.shape
    hw = fpre3.shape[-1]
    d = HEAD_DIM_HGRN
    nh = hw // d
    tt = _tile(t, ROWS_HGRN_BLOCK, c)
    n_chunks = tt // c
    rows = _tile(tt, 2 * LANES, c)
    pb = min(rows, LANES)
    tri, masks = _hgrn_tables(c, rows, pb)
    qb, vb, ob = q_col // d, v_col // d, og_col // d
    kern = functools.partial(_hgrn_kernel, c=c, n_chunks=n_chunks, rows=rows, pb=pb, layer=layer,
                             carry=carry)
    nl = lb_logits.shape[0]
    if carry:
        state_spec = pl.BlockSpec((1, 1, d, d), lambda b, h, i: (b, h, 0, 0))
    else:
        state_spec = pl.BlockSpec((n_chunks, 1, d, d), lambda b, h, i: (i, h, 0, 0))
    return pl.pallas_call(
        kern,
        grid=(bsz, nh, t // tt),
        in_specs=[pl.BlockSpec((rows, rows), lambda b, h, i: (0, 0)),
                  pl.BlockSpec(masks.shape, lambda b, h, i: (0, 0, 0)),
                  pl.BlockSpec((nl, d), lambda b, h, i: (0, h)),
                  pl.BlockSpec((1, tt, d), lambda b, h, i: (b, i, qb + h)),
                  pl.BlockSpec((1, tt, d), lambda b, h, i: (b, i, h)),
                  pl.BlockSpec((1, tt, d), lambda b, h, i: (b, i, vb + h)),
                  pl.BlockSpec((1, tt, d), lambda b, h, i: (b, i, ob + h)),
                  pl.BlockSpec((1, d), lambda b, h, i: (0, h)),
                  state_spec],
        out_specs=[pl.BlockSpec((1, tt, d), lambda b, h, i: (b, i, h)),
                   state_spec],
        out_shape=[jax.ShapeDtypeStruct((bsz, t, hw), BF16),
                   jax.ShapeDtypeStruct(s0.shape, F32)],
        scratch_shapes=[pltpu.VMEM((d, d), F32),
                        pltpu.VMEM((tt, d), F32),
                        pltpu.VMEM((tt, d), BF16),
                        pltpu.VMEM((n_chunks, d, d), F32),
                        pltpu.VMEM((n_chunks * SUBLANES, d), F32)],
        compiler_params=_params("parallel", "parallel", "arbitrary"),
        name="hgrn2",
    )(jnp.asarray(tri, BF16), jnp.asarray(masks, F32), lb_logits.astype(F32),
      proj3, fpre3, proj3, proj3, gain.reshape(1, hw).astype(F32), s0.astype(F32))


CONV_PAD = 8
CONV_BLOCK_ELEMS = 1024 * 1024


def _conv_kernel(cc_ref, ch_ref, cb_ref, w_ref, buf_ref, o_ref, nbuf_ref, u_ref, *, t, k, rb):
    u_ref[CONV_PAD - (k - 1):CONV_PAD, :] = buf_ref[0].astype(F32)
    for r in range(0, t, rb):
        u_ref[CONV_PAD + r:CONV_PAD + r + rb, :] = (
            cc_ref[0, r:r + rb, :].astype(F32) * ch_ref[0, r:r + rb, :].astype(F32))
    w = w_ref[...].astype(F32)
    for r in range(0, t, rb):
        z = w[0:1, :] * u_ref[CONV_PAD - (k - 1) + r:CONV_PAD - (k - 1) + r + rb, :]
        for j in range(1, k):
            s = CONV_PAD - (k - 1) + j + r
            z = z + w[j:j + 1, :] * u_ref[s:s + rb, :]
        o_ref[0, r:r + rb, :] = (cb_ref[0, r:r + rb, :].astype(F32) * z).astype(o_ref.dtype)
    nbuf_ref[0] = u_ref[CONV_PAD + t - (k - 1):CONV_PAD + t, :]


def _conv(proj3, conv_w, buf, ch_col, cb_col, cc_col):
    bsz, t, _ = proj3.shape
    k, cw = conv_w.shape
    assert k - 1 <= CONV_PAD
    d = _tile(cw, max(LANES, CONV_BLOCK_ELEMS // t), LANES)
    rb = _tile(t, max(SUBLANES, ROWS_SUB * LANES // d), SUBLANES)
    kern = functools.partial(_conv_kernel, t=t, k=k, rb=rb)
    return pl.pallas_call(
        kern,
        grid=(bsz, cw // d),
        in_specs=[pl.BlockSpec((1, t, d), lambda b, j: (b, 0, cc_col // d + j)),
                  pl.BlockSpec((1, t, d), lambda b, j: (b, 0, ch_col // d + j)),
                  pl.BlockSpec((1, t, d), lambda b, j: (b, 0, cb_col // d + j)),
                  pl.BlockSpec((k, d), lambda b, j: (0, j)),
                  pl.BlockSpec((1, k - 1, d), lambda b, j: (b, 0, j))],
        out_specs=[pl.BlockSpec((1, t, d), lambda b, j: (b, 0, j)),
                   pl.BlockSpec((1, k - 1, d), lambda b, j: (b, 0, j))],
        out_shape=[jax.ShapeDtypeStruct((bsz, t, cw), BF16),
                   jax.ShapeDtypeStruct((bsz, k - 1, cw), F32)],
        scratch_shapes=[pltpu.VMEM((CONV_PAD + t, d), F32)],
        compiler_params=_params("parallel", "parallel"),
        name="short_conv",
    )(proj3, proj3, proj3, conv_w.astype(F32), buf.astype(F32))


def _attn_kernel(q_ref, k_ref, v_ref, o_ref, kb_ref, vb_ref, *, scale, sub_rows):
    nt = (((1,), (1,)), ((), ()))

    @pl.when(pl.program_id(2) == 0)
    def _():
        kb_ref[...] = k_ref[0].astype(BF16)
        vb_ref[...] = v_ref[0].astype(BF16)

    subs = [pl.ds(r, sub_rows) for r in range(0, q_ref.shape[1], sub_rows)]
    scores = [lax.dot_general(q_ref[0, r, :], kb_ref[...], nt, preferred_element_type=F32) * scale
              for r in subs]
    probs = []
    for s in scores:
        p = jnp.exp(s - jnp.max(s, axis=-1, keepdims=True))
        probs.append((p / jnp.sum(p, axis=-1, keepdims=True)).astype(BF16))
    for r, p in zip(subs, probs):
        o_ref[0, r, :] = jnp.dot(p, vb_ref[...], preferred_element_type=F32).astype(o_ref.dtype)


def _attention(q3, mk4, mv4, mem_layer, n_heads):
    bsz, t, dm = q3.shape
    n_mem = mk4.shape[2]
    hd = dm // n_heads
    mem_spec = pl.BlockSpec((None, 1, n_mem, hd), lambda b, h, i: (mem_layer, b, 0, h))
    bt = _tile(t, ROWS_ATTN, SUBLANES)
    kern = functools.partial(_attn_kernel, scale=float(hd) ** -0.5, sub_rows=_tile(bt, ROWS_SUB, SUBLANES))
    return pl.pallas_call(
        kern,
        grid=(bsz, n_heads, t // bt),
        in_specs=[pl.BlockSpec((1, bt, hd), lambda b, h, i: (b, i, h)), mem_spec, mem_spec],
        out_specs=pl.BlockSpec((1, bt, hd), lambda b, h, i: (b, i, h)),
        out_shape=jax.ShapeDtypeStruct((bsz, t, dm), BF16),
        scratch_shapes=[pltpu.VMEM((n_mem, hd), BF16), pltpu.VMEM((n_mem, hd), BF16)],
        compiler_params=_params("parallel", "parallel", "arbitrary"),
        name="cross_attention",
    )(q3, mk4, mv4)


def _project(x, wts):
    hw = wts["hg_norm"].shape[0]
    h = _rmsnorm(x, wts["norm_mix"], BF16)
    if "w_in" in wts:
        pw = wts["w_in"].shape[1]
        proj, w_rest = _matmul(h, wts["w_in"], BF16, bn_pref=TILE_MATMUL_WIDE_OUT,
                               cols=[(0, hw), (2 * hw, pw - 2 * hw)], emit_w=True)
        fpre, w_f = _matmul(h, wts["w_in"], F32, bn_pref=TILE_MATMUL_WIDE_OUT, cols=[(hw, hw)], emit_w=True)
        wts = {k: v for k, v in wts.items() if k != "w_in"} | {"w_in_rest": w_rest, "w_in_f": w_f}
    elif "f32" in wts:
        names = list(wts["f32"])
        proj, cast = _matmul(h, wts["w_in_rest"], BF16, side=[wts["f32"][n] for n in names])
        fpre = _matmul(h, wts["w_in_f"], F32)
        wts = {k: v for k, v in wts.items() if k != "f32"} | dict(zip(names, cast))
    else:
        proj = _matmul(h, wts["w_in_rest"], BF16)
        fpre = _matmul(h, wts["w_in_f"], F32)
    return proj, fpre, wts


def _layer(x, proj, fpre, bsz, t, wts, lb_logits, layer, s0, buf, mem, n_xa_heads):
    m, dm = x.shape
    hw = wts["hg_norm"].shape[0]
    cw = wts["conv_w"].shape[1]
    q_col, v_col, og_col = 0, hw, 2 * hw
    ch_col, cb_col, cc_col = 3 * hw, 3 * hw + cw, 3 * hw + 2 * cw
    ga_col, gb_col = 3 * hw + 3 * cw, 3 * hw + 3 * cw + dm
    proj3 = proj.reshape(bsz, t, proj.shape[1])
    a_in, s_fin = _hgrn(proj3, fpre.reshape(bsz, t, hw), lb_logits, wts["hg_norm"], s0, layer,
                        q_col, v_col, og_col)
    bz, new_buf = _conv(proj3, wts["conv_w"], buf, ch_col, cb_col, cc_col)
    merged = _matmul_gated(a_in.reshape(m, hw), bz.reshape(m, cw), wts["w_a"], wts["w_b"],
                           proj, ga_col, gb_col)
    x, xg, ss = _matmul(merged, wts["w_o"], F32, bn_pref=TILE_MATMUL_WIDE_OUT, res=x,
                        norm_gain=wts["norm_xattn"])
    qx = _matmul(xg, wts["w_xq"], BF16, row_ss=ss)
    att = _attention(qx.reshape(bsz, t, dm), *mem, n_xa_heads)
    x, xg, ss = _matmul(att.reshape(m, dm), wts["w_xo"], F32, bn_pref=TILE_MATMUL_WIDE_OUT, res=x,
                        norm_gain=wts["norm_ffn"])
    act = _matmul_swiglu(xg, ss, wts["w_gate"], wts["w_up"])
    x = _matmul(act, wts["w_down"], F32, bm_pref=TILE_DOWN[0], bn_pref=TILE_DOWN[1], res=x, cols_outer=True)
    return x, s_fin, new_buf


def kernel(x_prompt, x_sample, cache_mem_k, cache_mem_v, state_hgrn, state_conv, mem_prompt, norm_mix, w_in, lb_logits, hg_norm, conv_w, w_a, w_b, w_o, norm_xattn, norm_mem, w_xq, w_xk, w_xv, w_xo, norm_ffn, w_gate, w_up, w_down, norm_final):
    depth = norm_mix.shape[0]
    bp, tp, dm = x_prompt.shape
    bs, ts, _ = x_sample.shape
    hw = hg_norm.shape[1]
    n_mem = mem_prompt.shape[1]
    n_xa_heads = cache_mem_k.shape[3]
    nh, dk, dv = state_hgrn.shape[2:]
    assert dk == HEAD_DIM_HGRN and dv == HEAD_DIM_HGRN and nh * dk == hw

    xp = x_prompt.reshape(bp * tp, dm)
    xs = x_sample.reshape(bs * ts, dm)
    outs = {k: [] for k in ("mk", "mv", "sp", "cp", "ss", "cs")}
    for l in range(depth):
        w_in_l = w_in[l]
        wts = {
            "norm_mix": norm_mix[l], "hg_norm": hg_norm[l], "conv_w": conv_w[l],
            "norm_xattn": norm_xattn[l], "norm_ffn": norm_ffn[l],
            "w_in": w_in_l,
            "f32": {"w_a": w_a[l], "w_b": w_b[l], "w_o": w_o[l], "w_xq": w_xq[l], "w_xo": w_xo[l],
                    "w_gate": w_gate[l], "w_up": w_up[l], "w_down": w_down[l]},
        }
        mem_n = _rmsnorm(mem_prompt.reshape(bp * n_mem, dm), norm_mem[l], BF16)
        mk_p = _matmul(mem_n, w_xk[l], F32)
        mv_p = _matmul(mem_n, w_xv[l], F32)
        s0 = jnp.zeros((bp, nh, dk, dv), F32)
        buf0 = jnp.zeros((bp, conv_w.shape[1] - 1, conv_w.shape[2]), F32)
        mem_p = (mk_p.reshape(1, bp, n_mem, dm), mv_p.reshape(1, bp, n_mem, dm), 0)
        mem_s = (cache_mem_k.reshape(depth, bs, n_mem, dm), cache_mem_v.reshape(depth, bs, n_mem, dm), l)
        proj_s, fpre_s, wts = _project(xs, wts)
        proj_p, fpre_p, wts = _project(xp, wts)
        xp, s_p, buf_p = _layer(xp, proj_p, fpre_p, bp, tp, wts, lb_logits, l, s0, buf0, mem_p, n_xa_heads)
        xs, s_s, buf_s = _layer(xs, proj_s, fpre_s, bs, ts, wts, lb_logits, l, state_hgrn[l],
                                state_conv[l], mem_s, n_xa_heads)
        outs["mk"].append(mk_p.reshape(bp, n_mem, n_xa_heads, dm // n_xa_heads))
        outs["mv"].append(mv_p.reshape(bp, n_mem, n_xa_heads, dm // n_xa_heads))
        outs["sp"].append(s_p)
        outs["cp"].append(buf_p)
        outs["ss"].append(s_s)
        outs["cs"].append(buf_s)
    y_prompt = _rmsnorm(xp, norm_final, F32).reshape(bp, tp, dm)
    y_sample = _rmsnorm(xs, norm_final, F32).reshape(bs, ts, dm)
    return (y_prompt, y_sample, jnp.stack(outs["mk"]), jnp.stack(outs["mv"]), jnp.stack(outs["sp"]),
            jnp.stack(outs["cp"]), jnp.stack(outs["ss"]), jnp.stack(outs["cs"]))
```

```python
import functools

import numpy as np
import jax
import jax.numpy as jnp
from jax import lax
from jax.experimental import pallas as pl
from jax.experimental.pallas import tpu as pltpu

EPS = 1e-6
LANES = 128
SUBLANES = 8
HEAD_DIM_HGRN = 128
CHUNK = 64
VMEM_LIMIT_BYTES = 60 * 1024 * 1024
BF16_SUBLANES = 16
BF16 = jnp.bfloat16
F32 = jnp.float32

TILE_MATMUL = (1024, 1024)
TILE_MATMUL_WIDE_OUT = 512
TILE_SWIGLU = (2048, 256)
TILE_DOWN = (512, 512)
ROWS_RMSNORM = 512
ROWS_HGRN_BLOCK = 4096
ROWS_ATTN = 2048
ROWS_SUB = 512


def _params(*semantics):
    return pltpu.CompilerParams(dimension_semantics=semantics, vmem_limit_bytes=VMEM_LIMIT_BYTES)


def _tile(n, pref, mult):
    if n <= pref:
        return n
    t = (pref // mult) * mult
    while t >= mult:
        if n % t == 0:
            return t
        t -= mult
    raise ValueError(f"no tile for {n} (pref {pref}, mult {mult})")


def _rmsnorm_kernel(x_ref, g_ref, o_ref):
    x = x_ref[...].astype(F32)
    y = x * lax.rsqrt(jnp.mean(x * x, axis=-1, keepdims=True) + EPS)
    o_ref[...] = (y * g_ref[...]).astype(o_ref.dtype)


def _rmsnorm(x, g, out_dtype):
    m, d = x.shape
    bm = _tile(m, ROWS_RMSNORM, SUBLANES)
    return pl.pallas_call(
        _rmsnorm_kernel,
        grid=(m // bm,),
        in_specs=[pl.BlockSpec((bm, d), lambda i: (i, 0)),
                  pl.BlockSpec((1, d), lambda i: (0, 0))],
        out_specs=pl.BlockSpec((bm, d), lambda i: (i, 0)),
        out_shape=jax.ShapeDtypeStruct((m, d), out_dtype),
        compiler_params=_params("parallel"),
        name="rmsnorm",
    )(x, g.reshape(1, d).astype(F32))


def _row_scale(rs_ref, n_lanes):
    return jnp.concatenate([rs_ref[...]] * (n_lanes // LANES), axis=1)


def _row_blocks(bm):
    sub = ROWS_SUB if bm % ROWS_SUB == 0 else bm
    return [pl.ds(r, sub) for r in range(0, bm, sub)]


def _mm_kernel(*refs, has_res, has_scale, has_gain, n_side, out_width, emit_w=False):
    refs = list(refs)
    x_ref, w_ref = refs.pop(0), refs.pop(0)
    r_ref = refs.pop(0) if has_res else None
    ss_in_ref = refs.pop(0) if has_scale else None
    g_ref = refs.pop(0) if has_gain else None
    src = [refs.pop(0) for _ in range(n_side)]
    o_ref = refs.pop(0)
    og_ref, ss_ref = (refs.pop(0), refs.pop(0)) if has_gain else (None, None)
    wcopy_ref = refs.pop(0) if emit_w else None
    dst = refs
    if has_gain:
        @pl.when(pl.program_id(1) == 0)
        def _():
            ss_ref[...] = jnp.zeros_like(ss_ref)

    w = w_ref[...].astype(BF16)
    if emit_w:
        wcopy_ref[...] = w
    for rows in _row_blocks(x_ref.shape[0]):
        acc = jnp.dot(x_ref[rows, :], w, preferred_element_type=F32)
        if has_scale:
            acc = acc * _row_scale(ss_in_ref.at[rows], acc.shape[1])
        if has_res:
            acc = r_ref[rows, :] + acc
        o_ref[rows, :] = acc.astype(o_ref.dtype)
        if has_gain:
            og_ref[rows, :] = (acc * g_ref[...]).astype(og_ref.dtype)
            sq = acc * acc
            ss_ref[rows, :] += functools.reduce(
                jnp.add, [sq[:, c:c + LANES] for c in range(0, sq.shape[1], LANES)])

    if has_gain:
        @pl.when(pl.program_id(1) == pl.num_programs(1) - 1)
        def _():
            total = jnp.sum(ss_ref[...], axis=-1, keepdims=True)
            ss_ref[...] = jnp.broadcast_to(lax.rsqrt(total * (1.0 / out_width) + EPS), ss_ref.shape)
    for src_ref, dst_ref in zip(src, dst):
        dst_ref[...] = src_ref[...].astype(BF16)


def _mm_gated_kernel(a_ref, b_ref, wa_ref, wb_ref, ga_ref, gb_ref, o_ref):
    for rows in _row_blocks(a_ref.shape[0]):
        a = jnp.dot(a_ref[rows, :], wa_ref[...], preferred_element_type=F32)
        b = jnp.dot(b_ref[rows, :], wb_ref[...], preferred_element_type=F32)
        ga = jax.nn.sigmoid(ga_ref[rows, :].astype(F32))
        gb = jax.nn.sigmoid(gb_ref[rows, :].astype(F32))
        o_ref[rows, :] = (ga * a + gb * b).astype(o_ref.dtype)


def _mm_swiglu_kernel(x_ref, wg_ref, wu_ref, ss_ref, o_ref):
    for rows in _row_blocks(x_ref.shape[0]):
        x = x_ref[rows, :]
        scale = _row_scale(ss_ref.at[rows], wg_ref.shape[1])
        g = jnp.dot(x, wg_ref[...], preferred_element_type=F32) * scale
        u = jnp.dot(x, wu_ref[...], preferred_element_type=F32) * scale
        o_ref[rows, :] = (g * jax.nn.sigmoid(g) * u).astype(o_ref.dtype)


def _matmul(x, w, out_dtype, *, bm_pref=TILE_MATMUL[0], bn_pref=TILE_MATMUL[1], res=None, cols=None,
            side=(), row_ss=None, norm_gain=None, cols_outer=False, emit_w=False):
    m, k = x.shape
    cols = cols or [(0, w.shape[1])]
    n = sum(width for _, width in cols)
    bm = _tile(m, bm_pref, SUBLANES)
    bn = _tile(int(functools.reduce(np.gcd, [v for seg in cols for v in seg if v])), bn_pref, LANES)
    steps, out_blk = [], 0
    for start, width in cols:
        steps.append((out_blk, start // bn - out_blk))
        out_blk += width // bn

    def w_block(j):
        blk = j + steps[0][1]
        for (first, shift), (_, prev) in zip(steps[1:], steps[:-1]):
            blk = blk + jnp.where(j >= first, shift - prev, 0)
        return blk

    ni, nj = m // bm, n // bn
    if cols_outer:
        assert not side and norm_gain is None and row_ss is None
        in_specs = [pl.BlockSpec((bm, k), lambda j, i: (i, 0)),
                    pl.BlockSpec((k, bn), lambda j, i: (0, w_block(j)))]
        args = [x, w]
        if res is not None:
            in_specs.append(pl.BlockSpec((bm, bn), lambda j, i: (i, j)))
            args.append(res)
        return pl.pallas_call(
            functools.partial(_mm_kernel, has_res=res is not None, has_scale=False, has_gain=False,
                              n_side=0, out_width=n),
            grid=(nj, ni),
            in_specs=in_specs,
            out_specs=[pl.BlockSpec((bm, bn), lambda j, i: (i, j))],
            out_shape=[jax.ShapeDtypeStruct((m, n), out_dtype)],
            compiler_params=_params("parallel", "parallel"),
            name="matmul_res" if res is not None else "matmul",
        )(*args)[0]
    in_specs = [pl.BlockSpec((bm, k), lambda i, j: (i, 0)),
                pl.BlockSpec((k, bn), lambda i, j: (0, w_block(j)))]
    args = [x, w]
    if res is not None:
        in_specs.append(pl.BlockSpec((bm, bn), lambda i, j: (i, j)))
        args.append(res)
    ss_spec = pl.BlockSpec((bm, LANES), lambda i, j: (i, 0))
    if row_ss is not None:
        in_specs.append(ss_spec)
        args.append(row_ss)
    out_specs = [pl.BlockSpec((bm, bn), lambda i, j: (i, j))]
    out_shape = [jax.ShapeDtypeStruct((m, n), out_dtype)]
    if norm_gain is not None:
        assert not side and cols == [(0, w.shape[1])]
        in_specs.append(pl.BlockSpec((1, bn), lambda i, j: (0, j)))
        args.append(norm_gain.reshape(1, n).astype(F32))
        out_specs += [pl.BlockSpec((bm, bn), lambda i, j: (i, j)), ss_spec]
        out_shape += [jax.ShapeDtypeStruct((m, n), BF16), jax.ShapeDtypeStruct((m, LANES), F32)]
    if emit_w:
        assert ni == 1 and not side and norm_gain is None
        out_specs.append(pl.BlockSpec((k, bn), lambda i, j: (0, j)))
        out_shape.append(jax.ShapeDtypeStruct((k, n), BF16))
    for sw in side:
        rows, width = sw.shape
        n_blocks = max(nb for nb in range(1, ni * nj + 1)
                       if rows % nb == 0 and (rows // nb) % BF16_SUBLANES == 0)
        spec = pl.BlockSpec((rows // n_blocks, width),
                            lambda i, j, n_blocks=n_blocks: (jnp.minimum(i * nj + j, n_blocks - 1), 0))
        in_specs.append(spec)
        args.append(sw)
        out_specs.append(spec)
        out_shape.append(jax.ShapeDtypeStruct((rows, width), BF16))
    outs = pl.pallas_call(
        functools.partial(_mm_kernel, has_res=res is not None, has_scale=row_ss is not None,
                          has_gain=norm_gain is not None, n_side=len(side), out_width=n,
                          emit_w=emit_w),
        grid=(ni, nj),
        in_specs=in_specs,
        out_specs=out_specs,
        out_shape=out_shape,
        compiler_params=(_params("arbitrary", "arbitrary") if side else
                         _params("parallel", "arbitrary") if norm_gain is not None else
                         _params("parallel", "parallel")),
        name="matmul_res" if res is not None else "matmul",
    )(*args)
    if side:
        return outs[0], list(outs[1:])
    return tuple(outs) if norm_gain is not None or emit_w else outs[0]


def _matmul_gated(a, b, wa, wb, proj, ga_col, gb_col):
    m, ka = a.shape
    kb = b.shape[1]
    n = wa.shape[1]
    bm = _tile(m, TILE_MATMUL[0], SUBLANES)
    bn = _tile(n, TILE_MATMUL[1], LANES)
    assert ga_col % bn == 0 and gb_col % bn == 0
    ga_blk, gb_blk = ga_col // bn, gb_col // bn
    return pl.pallas_call(
        _mm_gated_kernel,
        grid=(m // bm, n // bn),
        in_specs=[pl.BlockSpec((bm, ka), lambda i, j: (i, 0)),
                  pl.BlockSpec((bm, kb), lambda i, j: (i, 0)),
                  pl.BlockSpec((ka, bn), lambda i, j: (0, j)),
                  pl.BlockSpec((kb, bn), lambda i, j: (0, j)),
                  pl.BlockSpec((bm, bn), lambda i, j: (i, ga_blk + j)),
                  pl.BlockSpec((bm, bn), lambda i, j: (i, gb_blk + j))],
        out_specs=pl.BlockSpec((bm, bn), lambda i, j: (i, j)),
        out_shape=jax.ShapeDtypeStruct((m, n), BF16),
        compiler_params=_params("parallel", "parallel"),
        name="matmul_gated",
    )(a, b, wa, wb, proj, proj)


def _matmul_swiglu(x, row_ss, wg, wu):
    m, k = x.shape
    n = wg.shape[1]
    bm = _tile(m, TILE_SWIGLU[0], SUBLANES)
    bn = _tile(n, TILE_SWIGLU[1], LANES)
    return pl.pallas_call(
        _mm_swiglu_kernel,
        grid=(m // bm, n // bn),
        in_specs=[pl.BlockSpec((bm, k), lambda i, j: (i, 0)),
                  pl.BlockSpec((k, bn), lambda i, j: (0, j)),
                  pl.BlockSpec((k, bn), lambda i, j: (0, j)),
                  pl.BlockSpec((bm, LANES), lambda i, j: (i, 0))],
        out_specs=pl.BlockSpec((bm, bn), lambda i, j: (i, j)),
        out_shape=jax.ShapeDtypeStruct((m, n), BF16),
        compiler_params=_params("parallel", "parallel"),
        name="matmul_swiglu",
    )(x, wg, wu, row_ss)


def _hgrn_tables(c, rows, pb):
    t = np.arange(rows)[:, None]
    s = np.arange(rows)[None, :]
    tri = ((t // c) == (s // c)) & (s <= t)
    t, s = t[:pb, :pb], s[:pb, :pb]
    masks = []
    l = 1
    while l < c:
        masks.append(((t // l) % 2 == 1) & ((s // l) == (t // l) - 1))
        l *= 2
    masks.append(t == s)
    return tri.astype(np.float32), np.stack(masks, 0).astype(np.float32)


def _hgrn_kernel(tri_ref, mask_ref, lbl_ref, q_ref, f_ref, v_ref, og_ref, gn_ref, s0_ref,
                 o_ref, sfin_ref, st_ref, oi_ref, qin_ref, u_ref, dec_ref,
                 *, c, n_chunks, rows, pb, layer, carry):
    t_idx = pl.program_id(2)
    d = HEAD_DIM_HGRN
    n_lv = c.bit_length() - 1
    g_chunks = rows // c
    n_pb = rows // pb
    nt = (((1,), (1,)), ((), ()))
    tn = (((0,), (0,)), ((), ()))

    if carry:
        @pl.when(t_idx == 0)
        def _():
            st_ref[...] = s0_ref[0, 0].T

    logits = lbl_ref[...].astype(F32)
    ex = jnp.exp(logits - jnp.max(logits, axis=0, keepdims=True))
    lb = jnp.sum(ex[:layer + 1], axis=0, keepdims=True) / jnp.sum(ex, axis=0, keepdims=True)
    gain = gn_ref[...].astype(F32)
    shape3 = (rows // SUBLANES, SUBLANES, d)
    sub = lax.broadcasted_iota(jnp.int32, (1, SUBLANES, d), 1)

    def block_rows(x, first, step, length):
        pieces = [jnp.broadcast_to(x[first + step * p:first + step * p + 1, :], (length, d))
                  for p in range(rows // length)]
        return pieces[0] if len(pieces) == 1 else jnp.concatenate(pieces, axis=0)

    def pair_ref(b, l):
        if l >= SUBLANES:
            return block_rows(b, l - 1, 2 * l, 2 * l)
        b3 = b.reshape(shape3)
        pick = lambda i: jnp.broadcast_to(b3[:, i:i + 1, :], shape3)
        r3 = pick(3) if l == 4 else jnp.where(sub < 4, pick(1), pick(5))
        return r3.reshape(rows, d)

    def odd_half(x_odd, x_even, l):
        if l >= SUBLANES:
            pieces = [(x_odd if p % 2 else x_even)[p * l:(p + 1) * l] for p in range(rows // l)]
            return jnp.concatenate(pieces, axis=0)
        return jnp.where((sub & l) != 0, x_odd.reshape(shape3), x_even.reshape(shape3)).reshape(rows, d)

    def phase_a_head(bi):
        rsel = pl.ds(pl.multiple_of(bi * rows, rows), rows)
        f = lb + (1.0 - lb) * jax.nn.sigmoid(f_ref[0, rsel, :])
        g = jnp.log(f)
        k = 1.0 - f
        qb = q_ref[0, rsel, :]
        q = qb.astype(F32)
        v = v_ref[0, rsel, :]

        g_hi = g.astype(BF16)
        r1 = g - g_hi.astype(F32)
        g_mid = r1.astype(BF16)
        g_lo = (r1 - g_mid.astype(F32)).astype(BF16)
        b3 = jnp.dot(tri_ref[...], jnp.concatenate([g_hi, g_mid, g_lo], axis=1),
                     preferred_element_type=F32)
        b = b3[:, 0:d] + b3[:, d:2 * d] + b3[:, 2 * d:3 * d]
        return f, k, qb, q, v, b

    def phase_a_tail(bi, head):
        rsel = pl.ds(pl.multiple_of(bi * rows, rows), rows)
        f, k, qb, q, v, b = head

        def scores(zl, zr, li):
            p = lax.dot_general(zl, zr, nt, preferred_element_type=F32)
            return [mask_ref[li] * p[i * pb:(i + 1) * pb, i * pb:(i + 1) * pb] for i in range(n_pb)]

        a = scores(qb, k.astype(BF16), n_lv)
        z = odd_half(q * f, k, 1).astype(BF16)
        a = [x + y for x, y in zip(a, scores(z, z, 0))]
        for li in range(1, n_lv):
            l = 1 << li
            w = jnp.exp(-jnp.abs(b - pair_ref(b, l)))
            z = (odd_half(q, k, l) * w).astype(BF16)
            a = [x + y for x, y in zip(a, scores(z, z, li))]
        for i in range(n_pb):
            oi_ref[pl.ds(pl.multiple_of(bi * rows + i * pb, pb), pb), :] = jnp.dot(
                a[i].astype(BF16), v[i * pb:(i + 1) * pb], preferred_element_type=F32)

        dec = jnp.exp(b)
        qin_ref[rsel, :] = (q * dec).astype(BF16)
        k_out = (k * jnp.exp(block_rows(b, c - 1, c, c) - b)).astype(BF16)
        for ci in range(g_chunks):
            cr = slice(ci * c, (ci + 1) * c)
            u_ref[bi * g_chunks + ci] = lax.dot_general(v[cr], k_out[cr], tn, preferred_element_type=F32)
            dec_ref[pl.ds(pl.multiple_of((bi * g_chunks + ci) * SUBLANES, SUBLANES), SUBLANES), :] = (
                jnp.broadcast_to(dec[ci * c + c - 1:ci * c + c, :], (SUBLANES, d)))

    def phase_bc(bi):
        st = st_ref[...] if carry else None
        outs = []
        for ci in range(g_chunks):
            chunk = bi * g_chunks + ci
            rsel = pl.ds(pl.multiple_of(chunk * c, c), c)
            if not carry:
                st = s0_ref[chunk, 0].T
            outs.append(oi_ref[rsel, :] + lax.dot_general(
                qin_ref[rsel, :], st.astype(BF16), nt, preferred_element_type=F32))
            st = st * dec_ref[pl.ds(pl.multiple_of(chunk * SUBLANES, SUBLANES), 1), :] + u_ref[chunk]
            if not carry:
                sfin_ref[chunk, 0] = st.T
        if carry:
            st_ref[...] = st
        rsel = pl.ds(pl.multiple_of(bi * rows, rows), rows)
        o = outs[0] if g_chunks == 1 else jnp.concatenate(outs, axis=0)
        y = o * lax.rsqrt(jnp.mean(o * o, axis=-1, keepdims=True) + EPS) * gain
        og = og_ref[0, rsel, :].astype(F32)
        o_ref[0, rsel, :] = (y * (og * jax.nn.sigmoid(og))).astype(o_ref.dtype)

    n_batches = n_chunks // g_chunks
    head = phase_a_head(0)
    for bi in range(n_batches):
        next_head = phase_a_head(bi + 1) if bi + 1 < n_batches else None
        phase_a_tail(bi, head)
        if bi > 0:
            phase_bc(bi - 1)
        head = next_head
    phase_bc(n_batches - 1)

    if carry:
        @pl.when(t_idx == pl.num_programs(2) - 1)
        def _():
            sfin_ref[0, 0] = st_ref[...].T


def _hgrn(proj3, fpre3, lb_logits, gain, s0, layer, q_col, v_col, og_col):
    bsz, t, _ = proj3.shape
    hw = fpre3.shape[-1]
    d = HEAD_DIM_HGRN
    nh = hw // d
    c = CHUNK if t % CHUNK == 0 else t
    assert c & (c - 1) == 0 and c >= 2 * SUBLANES and t % c == 0
    carry = t > c
    if not carry:
        out, s_fin = _hgrn_call(proj3.reshape(1, bsz * t, -1), fpre3.reshape(1, bsz * t, hw), lb_logits,
                                gain, s0, layer, q_col, v_col, og_col, c=c, carry=False)
        return out.reshape(bsz, t, hw), s_fin
    return _hgrn_call(proj3, fpre3, lb_logits, gain, s0, layer, q_col, v_col, og_col, c=c, carry=True)


def _hgrn_call(proj3, fpre3, lb_logits, gain, s0, layer, q_col, v_col, og_col, *, c, carry):
    bsz, t, _ = proj3.shape
    hw = fpre3.shape[-1]
    d = HEAD_DIM_HGRN
    nh = hw // d
    tt = _tile(t, ROWS_HGRN_BLOCK, c)
    n_chunks = tt // c
    rows = _tile(tt, 2 * LANES, c)
    pb = min(rows, LANES)
    tri, masks = _hgrn_tables(c, rows, pb)
    qb, vb, ob = q_col // d, v_col // d, og_col // d
    kern = functools.partial(_hgrn_kernel, c=c, n_chunks=n_chunks, rows=rows, pb=pb, layer=layer,
                             carry=carry)
    nl = lb_logits.shape[0]
    if carry:
        state_spec = pl.BlockSpec((1, 1, d, d), lambda b, h, i: (b, h, 0, 0))
    else:
        state_spec = pl.BlockSpec((n_chunks, 1, d, d), lambda b, h, i: (i, h, 0, 0))
    return pl.pallas_call(
        kern,
        grid=(bsz, nh, t // tt),
        in_specs=[pl.BlockSpec((rows, rows), lambda b, h, i: (0, 0)),
                  pl.BlockSpec(masks.shape, lambda b, h, i: (0, 0, 0)),
                  pl.BlockSpec((nl, d), lambda b, h, i: (0, h)),
                  pl.BlockSpec((1, tt, d), lambda b, h, i: (b, i, qb + h)),
                  pl.BlockSpec((1, tt, d), lambda b, h, i: (b, i, h)),
                  pl.BlockSpec((1, tt, d), lambda b, h, i: (b, i, vb + h)),
                  pl.BlockSpec((1, tt, d), lambda b, h, i: (b, i, ob + h)),
                  pl.BlockSpec((1, d), lambda b, h, i: (0, h)),
                  state_spec],
        out_specs=[pl.BlockSpec((1, tt, d), lambda b, h, i: (b, i, h)),
                   state_spec],
        out_shape=[jax.ShapeDtypeStruct((bsz, t, hw), BF16),
                   jax.ShapeDtypeStruct(s0.shape, F32)],
        scratch_shapes=[pltpu.VMEM((d, d), F32),
                        pltpu.VMEM((tt, d), F32),
                        pltpu.VMEM((tt, d), BF16),
                        pltpu.VMEM((n_chunks, d, d), F32),
                        pltpu.VMEM((n_chunks * SUBLANES, d), F32)],
        compiler_params=_params("parallel", "parallel", "arbitrary"),
        name="hgrn2",
    )(jnp.asarray(tri, BF16), jnp.asarray(masks, F32), lb_logits.astype(F32),
      proj3, fpre3, proj3, proj3, gain.reshape(1, hw).astype(F32), s0.astype(F32))


CONV_PAD = 8
CONV_BLOCK_ELEMS = 1024 * 1024


def _conv_kernel(cc_ref, ch_ref, cb_ref, w_ref, buf_ref, o_ref, nbuf_ref, u_ref, *, t, k, rb):
    u_ref[CONV_PAD - (k - 1):CONV_PAD, :] = buf_ref[0].astype(F32)
    for r in range(0, t, rb):
        u_ref[CONV_PAD + r:CONV_PAD + r + rb, :] = (
            cc_ref[0, r:r + rb, :].astype(F32) * ch_ref[0, r:r + rb, :].astype(F32))
    w = w_ref[...].astype(F32)
    for r in range(0, t, rb):
        z = w[0:1, :] * u_ref[CONV_PAD - (k - 1) + r:CONV_PAD - (k - 1) + r + rb, :]
        for j in range(1, k):
            s = CONV_PAD - (k - 1) + j + r
            z = z + w[j:j + 1, :] * u_ref[s:s + rb, :]
        o_ref[0, r:r + rb, :] = (cb_ref[0, r:r + rb, :].astype(F32) * z).astype(o_ref.dtype)
    nbuf_ref[0] = u_ref[CONV_PAD + t - (k - 1):CONV_PAD + t, :]


def _conv(proj3, conv_w, buf, ch_col, cb_col, cc_col):
    bsz, t, _ = proj3.shape
    k, cw = conv_w.shape
    assert k - 1 <= CONV_PAD
    d = _tile(cw, max(LANES, CONV_BLOCK_ELEMS // t), LANES)
    rb = _tile(t, max(SUBLANES, ROWS_SUB * LANES // d), SUBLANES)
    kern = functools.partial(_conv_kernel, t=t, k=k, rb=rb)
    return pl.pallas_call(
        kern,
        grid=(bsz, cw // d),
        in_specs=[pl.BlockSpec((1, t, d), lambda b, j: (b, 0, cc_col // d + j)),
                  pl.BlockSpec((1, t, d), lambda b, j: (b, 0, ch_col // d + j)),
                  pl.BlockSpec((1, t, d), lambda b, j: (b, 0, cb_col // d + j)),
                  pl.BlockSpec((k, d), lambda b, j: (0, j)),
                  pl.BlockSpec((1, k - 1, d), lambda b, j: (b, 0, j))],
        out_specs=[pl.BlockSpec((1, t, d), lambda b, j: (b, 0, j)),
                   pl.BlockSpec((1, k - 1, d), lambda b, j: (b, 0, j))],
        out_shape=[jax.ShapeDtypeStruct((bsz, t, cw), BF16),
                   jax.ShapeDtypeStruct((bsz, k - 1, cw), F32)],
        scratch_shapes=[pltpu.VMEM((CONV_PAD + t, d), F32)],
        compiler_params=_params("parallel", "parallel"),
        name="short_conv",
    )(proj3, proj3, proj3, conv_w.astype(F32), buf.astype(F32))


def _attn_kernel(q_ref, k_ref, v_ref, o_ref, kb_ref, vb_ref, *, scale, sub_rows):
    @pl.when(pl.program_id(2) == 0)
    def _():
        kb_ref[...] = k_ref[0].astype(BF16)
        vb_ref[...] = v_ref[0].astype(BF16)

    _attn_tile(q_ref, kb_ref, vb_ref, o_ref, scale, sub_rows)


def _attn_head_split_kernel(q_ref, k_hbm, v_hbm, o_ref, kf_ref, vf_ref, kb_ref, vb_ref, sem,
                            *, scale, layer, sub_rows):
    @pl.when(pl.program_id(2) == 0)
    def _():
        b, h = pl.program_id(0), pl.program_id(1)
        copies = [pltpu.make_async_copy(src.at[layer, b, :, h, :], dst, sem.at[i])
                  for i, (src, dst) in enumerate(((k_hbm, kf_ref), (v_hbm, vf_ref)))]
        for cp in copies:
            cp.start()
        for cp in copies:
            cp.wait()
        kb_ref[...] = kf_ref[...].astype(BF16)
        vb_ref[...] = vf_ref[...].astype(BF16)

    _attn_tile(q_ref, kb_ref, vb_ref, o_ref, scale, sub_rows)


def _attn_tile(q_ref, kb_ref, vb_ref, o_ref, scale, sub_rows):
    nt = (((1,), (1,)), ((), ()))
    subs = [pl.ds(r, sub_rows) for r in range(0, q_ref.shape[1], sub_rows)]
    scores = [lax.dot_general(q_ref[0, r, :], kb_ref[...], nt, preferred_element_type=F32) * scale
              for r in subs]
    probs = []
    for s in scores:
        p = jnp.exp(s - jnp.max(s, axis=-1, keepdims=True))
        probs.append((p / jnp.sum(p, axis=-1, keepdims=True)).astype(BF16))
    for r, p in zip(subs, probs):
        o_ref[0, r, :] = jnp.dot(p, vb_ref[...], preferred_element_type=F32).astype(o_ref.dtype)


def _attention(q3, mk4, mv4, mem_layer, n_heads):
    bsz, t, dm = q3.shape
    n_mem = mk4.shape[2]
    hd = dm // n_heads
    if mk4.ndim == 5:
        bt = _tile(t, ROWS_ATTN, SUBLANES)
        kern = functools.partial(_attn_head_split_kernel, scale=float(hd) ** -0.5, layer=mem_layer,
                                 sub_rows=_tile(bt, ROWS_SUB, SUBLANES))
        return pl.pallas_call(
            kern,
            grid=(bsz, n_heads, t // bt),
            in_specs=[pl.BlockSpec((1, bt, hd), lambda b, h, i: (b, i, h)),
                      pl.BlockSpec(memory_space=pl.ANY), pl.BlockSpec(memory_space=pl.ANY)],
            out_specs=pl.BlockSpec((1, bt, hd), lambda b, h, i: (b, i, h)),
            out_shape=jax.ShapeDtypeStruct((bsz, t, dm), BF16),
            scratch_shapes=[pltpu.VMEM((n_mem, hd), F32), pltpu.VMEM((n_mem, hd), F32),
                            pltpu.VMEM((n_mem, hd), BF16), pltpu.VMEM((n_mem, hd), BF16),
                            pltpu.SemaphoreType.DMA((2,))],
            compiler_params=_params("arbitrary", "arbitrary", "arbitrary"),
            name="cross_attention_head_split",
        )(q3, mk4, mv4)
    mem_spec = pl.BlockSpec((None, 1, n_mem, hd), lambda b, h, i: (mem_layer, b, 0, h))
    bt = _tile(t, ROWS_ATTN, SUBLANES)
    kern = functools.partial(_attn_kernel, scale=float(hd) ** -0.5, sub_rows=_tile(bt, ROWS_SUB, SUBLANES))
    return pl.pallas_call(
        kern,
        grid=(bsz, n_heads, t // bt),
        in_specs=[pl.BlockSpec((1, bt, hd), lambda b, h, i: (b, i, h)), mem_spec, mem_spec],
        out_specs=pl.BlockSpec((1, bt, hd), lambda b, h, i: (b, i, h)),
        out_shape=jax.ShapeDtypeStruct((bsz, t, dm), BF16),
        scratch_shapes=[pltpu.VMEM((n_mem, hd), BF16), pltpu.VMEM((n_mem, hd), BF16)],
        compiler_params=_params("parallel", "parallel", "arbitrary"),
        name="cross_attention",
    )(q3, mk4, mv4)


def _project(x, wts):
    hw = wts["hg_norm"].shape[0]
    h = _rmsnorm(x, wts["norm_mix"], BF16)
    if "w_in" in wts:
        pw = wts["w_in"].shape[1]
        proj, w_rest = _matmul(h, wts["w_in"], BF16, bn_pref=TILE_MATMUL_WIDE_OUT,
                               cols=[(0, hw), (2 * hw, pw - 2 * hw)], emit_w=True)
        fpre, w_f = _matmul(h, wts["w_in"], F32, bn_pref=TILE_MATMUL_WIDE_OUT, cols=[(hw, hw)], emit_w=True)
        wts = {k: v for k, v in wts.items() if k != "w_in"} | {"w_in_rest": w_rest, "w_in_f": w_f}
    elif "f32" in wts:
        names = list(wts["f32"])
        proj, cast = _matmul(h, wts["w_in_rest"], BF16, side=[wts["f32"][n] for n in names])
        fpre = _matmul(h, wts["w_in_f"], F32)
        wts = {k: v for k, v in wts.items() if k != "f32"} | dict(zip(names, cast))
    else:
        proj = _matmul(h, wts["w_in_rest"], BF16)
        fpre = _matmul(h, wts["w_in_f"], F32)
    return proj, fpre, wts


def _layer(x, proj, fpre, bsz, t, wts, lb_logits, layer, s0, buf, mem, n_xa_heads):
    m, dm = x.shape
    hw = wts["hg_norm"].shape[0]
    cw = wts["conv_w"].shape[1]
    q_col, v_col, og_col = 0, hw, 2 * hw
    ch_col, cb_col, cc_col = 3 * hw, 3 * hw + cw, 3 * hw + 2 * cw
    ga_col, gb_col = 3 * hw + 3 * cw, 3 * hw + 3 * cw + dm
    proj3 = proj.reshape(bsz, t, proj.shape[1])
    a_in, s_fin = _hgrn(proj3, fpre.reshape(bsz, t, hw), lb_logits, wts["hg_norm"], s0, layer,
                        q_col, v_col, og_col)
    bz, new_buf = _conv(proj3, wts["conv_w"], buf, ch_col, cb_col, cc_col)
    merged = _matmul_gated(a_in.reshape(m, hw), bz.reshape(m, cw), wts["w_a"], wts["w_b"],
                           proj, ga_col, gb_col)
    x, xg, ss = _matmul(merged, wts["w_o"], F32, bn_pref=TILE_MATMUL_WIDE_OUT, res=x,
                        norm_gain=wts["norm_xattn"])
    qx = _matmul(xg, wts["w_xq"], BF16, row_ss=ss)
    att = _attention(qx.reshape(bsz, t, dm), *mem, n_xa_heads)
    x, xg, ss = _matmul(att.reshape(m, dm), wts["w_xo"], F32, bn_pref=TILE_MATMUL_WIDE_OUT, res=x,
                        norm_gain=wts["norm_ffn"])
    act = _matmul_swiglu(xg, ss, wts["w_gate"], wts["w_up"])
    x = _matmul(act, wts["w_down"], F32, bm_pref=TILE_DOWN[0], bn_pref=TILE_DOWN[1], res=x, cols_outer=True)
    return x, s_fin, new_buf


def kernel(x_prompt, x_sample, cache_mem_k, cache_mem_v, state_hgrn, state_conv, mem_prompt, norm_mix, w_in, lb_logits, hg_norm, conv_w, w_a, w_b, w_o, norm_xattn, norm_mem, w_xq, w_xk, w_xv, w_xo, norm_ffn, w_gate, w_up, w_down, norm_final):
    depth = norm_mix.shape[0]
    bp, tp, dm = x_prompt.shape
    bs, ts, _ = x_sample.shape
    hw = hg_norm.shape[1]
    n_mem = mem_prompt.shape[1]
    n_xa_heads = cache_mem_k.shape[3]
    nh, dk, dv = state_hgrn.shape[2:]
    assert dk == HEAD_DIM_HGRN and dv == HEAD_DIM_HGRN and nh * dk == hw

    xp = x_prompt.reshape(bp * tp, dm)
    xs = x_sample.reshape(bs * ts, dm)
    outs = {k: [] for k in ("mk", "mv", "sp", "cp", "ss", "cs")}
    for l in range(depth):
        w_in_l = w_in[l]
        wts = {
            "norm_mix": norm_mix[l], "hg_norm": hg_norm[l], "conv_w": conv_w[l],
            "norm_xattn": norm_xattn[l], "norm_ffn": norm_ffn[l],
            "w_in": w_in_l,
            "f32": {"w_a": w_a[l], "w_b": w_b[l], "w_o": w_o[l], "w_xq": w_xq[l], "w_xo": w_xo[l],
                    "w_gate": w_gate[l], "w_up": w_up[l], "w_down": w_down[l]},
        }
        mem_n = _rmsnorm(mem_prompt.reshape(bp * n_mem, dm), norm_mem[l], BF16)
        mk_p = _matmul(mem_n, w_xk[l], F32)
        mv_p = _matmul(mem_n, w_xv[l], F32)
        s0 = jnp.zeros((bp, nh, dk, dv), F32)
        buf0 = jnp.zeros((bp, conv_w.shape[1] - 1, conv_w.shape[2]), F32)
        mem_p = (mk_p.reshape(1, bp, n_mem, dm), mv_p.reshape(1, bp, n_mem, dm), 0)
        mem_s = (cache_mem_k, cache_mem_v, l)
        proj_s, fpre_s, wts = _project(xs, wts)
        proj_p, fpre_p, wts = _project(xp, wts)
        xp, s_p, buf_p = _layer(xp, proj_p, fpre_p, bp, tp, wts, lb_logits, l, s0, buf0, mem_p, n_xa_heads)
        xs, s_s, buf_s = _layer(xs, proj_s, fpre_s, bs, ts, wts, lb_logits, l, state_hgrn[l],
                                state_conv[l], mem_s, n_xa_heads)
        outs["mk"].append(mk_p.reshape(bp, n_mem, n_xa_heads, dm // n_xa_heads))
        outs["mv"].append(mv_p.reshape(bp, n_mem, n_xa_heads, dm // n_xa_heads))
        outs["sp"].append(s_p)
        outs["cp"].append(buf_p)
        outs["ss"].append(s_s)
        outs["cs"].append(buf_s)
    y_prompt = _rmsnorm(xp, norm_final, F32).reshape(bp, tp, dm)
    y_sample = _rmsnorm(xs, norm_final, F32).reshape(bs, ts, dm)
    return (y_prompt, y_sample, jnp.stack(outs["mk"]), jnp.stack(outs["mv"]), jnp.stack(outs["sp"]),
            jnp.stack(outs["cp"]), jnp.stack(outs["ss"]), jnp.stack(outs["cs"]))
```

```python
import functools

import numpy as np
import jax
import jax.numpy as jnp
from jax import lax
from jax.experimental import pallas as pl
from jax.experimental.pallas import tpu as pltpu

EPS = 1e-6
LANES = 128
SUBLANES = 8
HEAD_DIM_HGRN = 128
CHUNK = 64
VMEM_LIMIT_BYTES = 60 * 1024 * 1024
BF16_SUBLANES = 16
BF16 = jnp.bfloat16
F32 = jnp.float32

TILE_MATMUL = (1024, 1024)
TILE_MATMUL_WIDE_OUT = 512
TILE_SWIGLU = (2048, 256)
TILE_DOWN = (512, 512)
ROWS_RMSNORM = 512
ROWS_HGRN_BLOCK = 4096
ROWS_ATTN = 2048
ROWS_SUB = 512


def _params(*semantics):
    return pltpu.CompilerParams(dimension_semantics=semantics, vmem_limit_bytes=VMEM_LIMIT_BYTES)


def _tile(n, pref, mult):
    if n <= pref:
        return n
    t = (pref // mult) * mult
    while t >= mult:
        if n % t == 0:
            return t
        t -= mult
    raise ValueError(f"no tile for {n} (pref {pref}, mult {mult})")


def _rmsnorm_kernel(x_ref, g_ref, o_ref):
    x = x_ref[...].astype(F32)
    y = x * lax.rsqrt(jnp.mean(x * x, axis=-1, keepdims=True) + EPS)
    o_ref[...] = (y * g_ref[...]).astype(o_ref.dtype)


def _rmsnorm(x, g, out_dtype):
    m, d = x.shape
    bm = _tile(m, ROWS_RMSNORM, SUBLANES)
    return pl.pallas_call(
        _rmsnorm_kernel,
        grid=(m // bm,),
        in_specs=[pl.BlockSpec((bm, d), lambda i: (i, 0)),
                  pl.BlockSpec((1, d), lambda i: (0, 0))],
        out_specs=pl.BlockSpec((bm, d), lambda i: (i, 0)),
        out_shape=jax.ShapeDtypeStruct((m, d), out_dtype),
        compiler_params=_params("parallel"),
        name="rmsnorm",
    )(x, g.reshape(1, d).astype(F32))


def _row_scale(rs_ref, n_lanes):
    return jnp.concatenate([rs_ref[...]] * (n_lanes // LANES), axis=1)


def _row_blocks(bm):
    sub = ROWS_SUB if bm % ROWS_SUB == 0 else bm
    return [pl.ds(r, sub) for r in range(0, bm, sub)]


def _mm_kernel(*refs, has_res, has_scale, has_gain, n_side, out_width, emit_w=False):
    refs = list(refs)
    x_ref, w_ref = refs.pop(0), refs.pop(0)
    r_ref = refs.pop(0) if has_res else None
    ss_in_ref = refs.pop(0) if has_scale else None
    g_ref = refs.pop(0) if has_gain else None
    src = [refs.pop(0) for _ in range(n_side)]
    o_ref = refs.pop(0)
    og_ref, ss_ref = (refs.pop(0), refs.pop(0)) if has_gain else (None, None)
    wcopy_ref = refs.pop(0) if emit_w else None
    dst = refs
    if has_gain:
        @pl.when(pl.program_id(1) == 0)
        def _():
            ss_ref[...] = jnp.zeros_like(ss_ref)

    w = w_ref[...].astype(BF16)
    if emit_w:
        wcopy_ref[...] = w
    for rows in _row_blocks(x_ref.shape[0]):
        acc = jnp.dot(x_ref[rows, :], w, preferred_element_type=F32)
        if has_scale:
            acc = acc * _row_scale(ss_in_ref.at[rows], acc.shape[1])
        if has_res:
            acc = r_ref[rows, :] + acc
        o_ref[rows, :] = acc.astype(o_ref.dtype)
        if has_gain:
            og_ref[rows, :] = (acc * g_ref[...]).astype(og_ref.dtype)
            sq = acc * acc
            ss_ref[rows, :] += functools.reduce(
                jnp.add, [sq[:, c:c + LANES] for c in range(0, sq.shape[1], LANES)])

    if has_gain:
        @pl.when(pl.program_id(1) == pl.num_programs(1) - 1)
        def _():
            total = jnp.sum(ss_ref[...], axis=-1, keepdims=True)
            ss_ref[...] = jnp.broadcast_to(lax.rsqrt(total * (1.0 / out_width) + EPS), ss_ref.shape)
    for src_ref, dst_ref in zip(src, dst):
        dst_ref[...] = src_ref[...].astype(BF16)


def _mm_gated_kernel(a_ref, b_ref, wa_ref, wb_ref, ga_ref, gb_ref, o_ref):
    for rows in _row_blocks(a_ref.shape[0]):
        a = jnp.dot(a_ref[rows, :], wa_ref[...], preferred_element_type=F32)
        b = jnp.dot(b_ref[rows, :], wb_ref[...], preferred_element_type=F32)
        ga = jax.nn.sigmoid(ga_ref[rows, :].astype(F32))
        gb = jax.nn.sigmoid(gb_ref[rows, :].astype(F32))
        o_ref[rows, :] = (ga * a + gb * b).astype(o_ref.dtype)


def _mm_swiglu_kernel(x_ref, wg_ref, wu_ref, ss_ref, o_ref):
    for rows in _row_blocks(x_ref.shape[0]):
        x = x_ref[rows, :]
        scale = _row_scale(ss_ref.at[rows], wg_ref.shape[1])
        g = jnp.dot(x, wg_ref[...], preferred_element_type=F32) * scale
        u = jnp.dot(x, wu_ref[...], preferred_element_type=F32) * scale
        o_ref[rows, :] = (g * jax.nn.sigmoid(g) * u).astype(o_ref.dtype)


def _matmul(x, w, out_dtype, *, bm_pref=TILE_MATMUL[0], bn_pref=TILE_MATMUL[1], res=None, cols=None,
            side=(), row_ss=None, norm_gain=None, cols_outer=False, emit_w=False):
    m, k = x.shape
    cols = cols or [(0, w.shape[1])]
    n = sum(width for _, width in cols)
    bm = _tile(m, bm_pref, SUBLANES)
    bn = _tile(int(functools.reduce(np.gcd, [v for seg in cols for v in seg if v])), bn_pref, LANES)
    steps, out_blk = [], 0
    for start, width in cols:
        steps.append((out_blk, start // bn - out_blk))
        out_blk += width // bn

    def w_block(j):
        blk = j + steps[0][1]
        for (first, shift), (_, prev) in zip(steps[1:], steps[:-1]):
            blk = blk + jnp.where(j >= first, shift - prev, 0)
        return blk

    ni, nj = m // bm, n // bn
    if cols_outer:
        assert not side and norm_gain is None and row_ss is None
        in_specs = [pl.BlockSpec((bm, k), lambda j, i: (i, 0)),
                    pl.BlockSpec((k, bn), lambda j, i: (0, w_block(j)))]
        args = [x, w]
        if res is not None:
            in_specs.append(pl.BlockSpec((bm, bn), lambda j, i: (i, j)))
            args.append(res)
        return pl.pallas_call(
            functools.partial(_mm_kernel, has_res=res is not None, has_scale=False, has_gain=False,
                              n_side=0, out_width=n),
            grid=(nj, ni),
            in_specs=in_specs,
            out_specs=[pl.BlockSpec((bm, bn), lambda j, i: (i, j))],
            out_shape=[jax.ShapeDtypeStruct((m, n), out_dtype)],
            compiler_params=_params("parallel", "parallel"),
            name="matmul_res" if res is not None else "matmul",
        )(*args)[0]
    in_specs = [pl.BlockSpec((bm, k), lambda i, j: (i, 0)),
                pl.BlockSpec((k, bn), lambda i, j: (0, w_block(j)))]
    args = [x, w]
    if res is not None:
        in_specs.append(pl.BlockSpec((bm, bn), lambda i, j: (i, j)))
        args.append(res)
    ss_spec = pl.BlockSpec((bm, LANES), lambda i, j: (i, 0))
    if row_ss is not None:
        in_specs.append(ss_spec)
        args.append(row_ss)
    out_specs = [pl.BlockSpec((bm, bn), lambda i, j: (i, j))]
    out_shape = [jax.ShapeDtypeStruct((m, n), out_dtype)]
    if norm_gain is not None:
        assert not side and cols == [(0, w.shape[1])]
        in_specs.append(pl.BlockSpec((1, bn), lambda i, j: (0, j)))
        args.append(norm_gain.reshape(1, n).astype(F32))
        out_specs += [pl.BlockSpec((bm, bn), lambda i, j: (i, j)), ss_spec]
        out_shape += [jax.ShapeDtypeStruct((m, n), BF16), jax.ShapeDtypeStruct((m, LANES), F32)]
    if emit_w:
        assert ni == 1 and not side and norm_gain is None
        out_specs.append(pl.BlockSpec((k, bn), lambda i, j: (0, j)))
        out_shape.append(jax.ShapeDtypeStruct((k, n), BF16))
    for sw in side:
        rows, width = sw.shape
        n_blocks = max(nb for nb in range(1, ni * nj + 1)
                       if rows % nb == 0 and (rows // nb) % BF16_SUBLANES == 0)
        spec = pl.BlockSpec((rows // n_blocks, width),
                            lambda i, j, n_blocks=n_blocks: (jnp.minimum(i * nj + j, n_blocks - 1), 0))
        in_specs.append(spec)
        args.append(sw)
        out_specs.append(spec)
        out_shape.append(jax.ShapeDtypeStruct((rows, width), BF16))
    outs = pl.pallas_call(
        functools.partial(_mm_kernel, has_res=res is not None, has_scale=row_ss is not None,
                          has_gain=norm_gain is not None, n_side=len(side), out_width=n,
                          emit_w=emit_w),
        grid=(ni, nj),
        in_specs=in_specs,
        out_specs=out_specs,
        out_shape=out_shape,
        compiler_params=(_params("arbitrary", "arbitrary") if side else
                         _params("parallel", "arbitrary") if norm_gain is not None else
                         _params("parallel", "parallel")),
        name="matmul_res" if res is not None else "matmul",
    )(*args)
    if side:
        return outs[0], list(outs[1:])
    return tuple(outs) if norm_gain is not None or emit_w else outs[0]


def _matmul_gated(a, b, wa, wb, proj, ga_col, gb_col):
    m, ka = a.shape
    kb = b.shape[1]
    n = wa.shape[1]
    bm = _tile(m, TILE_MATMUL[0], SUBLANES)
    bn = _tile(n, TILE_MATMUL[1], LANES)
    assert ga_col % bn == 0 and gb_col % bn == 0
    ga_blk, gb_blk = ga_col // bn, gb_col // bn
    return pl.pallas_call(
        _mm_gated_kernel,
        grid=(m // bm, n // bn),
        in_specs=[pl.BlockSpec((bm, ka), lambda i, j: (i, 0)),
                  pl.BlockSpec((bm, kb), lambda i, j: (i, 0)),
                  pl.BlockSpec((ka, bn), lambda i, j: (0, j)),
                  pl.BlockSpec((kb, bn), lambda i, j: (0, j)),
                  pl.BlockSpec((bm, bn), lambda i, j: (i, ga_blk + j)),
                  pl.BlockSpec((bm, bn), lambda i, j: (i, gb_blk + j))],
        out_specs=pl.BlockSpec((bm, bn), lambda i, j: (i, j)),
        out_shape=jax.ShapeDtypeStruct((m, n), BF16),
        compiler_params=_params("parallel", "parallel"),
        name="matmul_gated",
    )(a, b, wa, wb, proj, proj)


def _matmul_swiglu(x, row_ss, wg, wu):
    m, k = x.shape
    n = wg.shape[1]
    bm = _tile(m, TILE_SWIGLU[0], SUBLANES)
    bn = _tile(n, TILE_SWIGLU[1], LANES)
    return pl.pallas_call(
        _mm_swiglu_kernel,
        grid=(m // bm, n // bn),
        in_specs=[pl.BlockSpec((bm, k), lambda i, j: (i, 0)),
                  pl.BlockSpec((k, bn), lambda i, j: (0, j)),
                  pl.BlockSpec((k, bn), lambda i, j: (0, j)),
                  pl.BlockSpec((bm, LANES), lambda i, j: (i, 0))],
        out_specs=pl.BlockSpec((bm, bn), lambda i, j: (i, j)),
        out_shape=jax.ShapeDtypeStruct((m, n), BF16),
        compiler_params=_params("parallel", "parallel"),
        name="matmul_swiglu",
    )(x, wg, wu, row_ss)


def _hgrn_tables(c, rows, pb):
    t = np.arange(rows)[:, None]
    s = np.arange(rows)[None, :]
    tri = ((t // c) == (s // c)) & (s <= t)
    t, s = t[:pb, :pb], s[:pb, :pb]
    masks = []
    l = 1
    while l < c:
        masks.append(((t // l) % 2 == 1) & ((s // l) == (t // l) - 1))
        l *= 2
    masks.append(t == s)
    return tri.astype(np.float32), np.stack(masks, 0).astype(np.float32)


def _hgrn_kernel(tri_ref, mask_ref, lbl_ref, q_ref, f_ref, v_ref, og_ref, gn_ref, s0_ref,
                 o_ref, sfin_ref, st_ref, oi_ref, qin_ref, u_ref, dec_ref,
                 *, c, n_chunks, rows, pb, layer, carry):
    t_idx = pl.program_id(2)
    d = HEAD_DIM_HGRN
    n_lv = c.bit_length() - 1
    g_chunks = rows // c
    n_pb = rows // pb
    nt = (((1,), (1,)), ((), ()))
    tn = (((0,), (0,)), ((), ()))

    if carry:
        @pl.when(t_idx == 0)
        def _():
            st_ref[...] = s0_ref[0, 0].T

    logits = lbl_ref[...].astype(F32)
    ex = jnp.exp(logits - jnp.max(logits, axis=0, keepdims=True))
    lb = jnp.sum(ex[:layer + 1], axis=0, keepdims=True) / jnp.sum(ex, axis=0, keepdims=True)
    gain = gn_ref[...].astype(F32)
    shape3 = (rows // SUBLANES, SUBLANES, d)
    sub = lax.broadcasted_iota(jnp.int32, (1, SUBLANES, d), 1)

    def block_rows(x, first, step, length):
        pieces = [jnp.broadcast_to(x[first + step * p:first + step * p + 1, :], (length, d))
                  for p in range(rows // length)]
        return pieces[0] if len(pieces) == 1 else jnp.concatenate(pieces, axis=0)

    def pair_ref(b, l):
        if l >= SUBLANES:
            return block_rows(b, l - 1, 2 * l, 2 * l)
        b3 = b.reshape(shape3)
        pick = lambda i: jnp.broadcast_to(b3[:, i:i + 1, :], shape3)
        r3 = pick(3) if l == 4 else jnp.where(sub < 4, pick(1), pick(5))
        return r3.reshape(rows, d)

    def odd_half(x_odd, x_even, l):
        if l >= SUBLANES:
            pieces = [(x_odd if p % 2 else x_even)[p * l:(p + 1) * l] for p in range(rows // l)]
            return jnp.concatenate(pieces, axis=0)
        return jnp.where((sub & l) != 0, x_odd.reshape(shape3), x_even.reshape(shape3)).reshape(rows, d)

    def phase_a_head(bi):
        rsel = pl.ds(pl.multiple_of(bi * rows, rows), rows)
        f = lb + (1.0 - lb) * jax.nn.sigmoid(f_ref[0, rsel, :])
        g = jnp.log(f)
        k = 1.0 - f
        qb = q_ref[0, rsel, :]
        q = qb.astype(F32)
        v = v_ref[0, rsel, :]

        g_hi = g.astype(BF16)
        r1 = g - g_hi.astype(F32)
        g_mid = r1.astype(BF16)
        g_lo = (r1 - g_mid.astype(F32)).astype(BF16)
        b3 = jnp.dot(tri_ref[...], jnp.concatenate([g_hi, g_mid, g_lo], axis=1),
                     preferred_element_type=F32)
        b = b3[:, 0:d] + b3[:, d:2 * d] + b3[:, 2 * d:3 * d]
        return f, k, qb, q, v, b

    def phase_a_tail(bi, head):
        rsel = pl.ds(pl.multiple_of(bi * rows, rows), rows)
        f, k, qb, q, v, b = head

        def scores(zl, zr, li):
            p = lax.dot_general(zl, zr, nt, preferred_element_type=F32)
            return [mask_ref[li] * p[i * pb:(i + 1) * pb, i * pb:(i + 1) * pb] for i in range(n_pb)]

        a = scores(qb, k.astype(BF16), n_lv)
        z = odd_half(q * f, k, 1).astype(BF16)
        a = [x + y for x, y in zip(a, scores(z, z, 0))]
        for li in range(1, n_lv):
            l = 1 << li
            w = jnp.exp(-jnp.abs(b - pair_ref(b, l)))
            z = (odd_half(q, k, l) * w).astype(BF16)
            a = [x + y for x, y in zip(a, scores(z, z, li))]
        for i in range(n_pb):
            oi_ref[pl.ds(pl.multiple_of(bi * rows + i * pb, pb), pb), :] = jnp.dot(
                a[i].astype(BF16), v[i * pb:(i + 1) * pb], preferred_element_type=F32)

        dec = jnp.exp(b)
        qin_ref[rsel, :] = (q * dec).astype(BF16)
        k_out = (k * jnp.exp(block_rows(b, c - 1, c, c) - b)).astype(BF16)
        for ci in range(g_chunks):
            cr = slice(ci * c, (ci + 1) * c)
            u_ref[bi * g_chunks + ci] = lax.dot_general(v[cr], k_out[cr], tn, preferred_element_type=F32)
            dec_ref[pl.ds(pl.multiple_of((bi * g_chunks + ci) * SUBLANES, SUBLANES), SUBLANES), :] = (
                jnp.broadcast_to(dec[ci * c + c - 1:ci * c + c, :], (SUBLANES, d)))

    def phase_bc(bi):
        st = st_ref[...] if carry else None
        outs = []
        for ci in range(g_chunks):
            chunk = bi * g_chunks + ci
            rsel = pl.ds(pl.multiple_of(chunk * c, c), c)
            if not carry:
                st = s0_ref[chunk, 0].T
            outs.append(oi_ref[rsel, :] + lax.dot_general(
                qin_ref[rsel, :], st.astype(BF16), nt, preferred_element_type=F32))
            st = st * dec_ref[pl.ds(pl.multiple_of(chunk * SUBLANES, SUBLANES), 1), :] + u_ref[chunk]
            if not carry:
                sfin_ref[chunk, 0] = st.T
        if carry:
            st_ref[...] = st
        rsel = pl.ds(pl.multiple_of(bi * rows, rows), rows)
        o = outs[0] if g_chunks == 1 else jnp.concatenate(outs, axis=0)
        y = o * lax.rsqrt(jnp.mean(o * o, axis=-1, keepdims=True) + EPS) * gain
        og = og_ref[0, rsel, :].astype(F32)
        o_ref[0, rsel, :] = (y * (og * jax.nn.sigmoid(og))).astype(o_ref.dtype)

    n_batches = n_chunks // g_chunks
    head = phase_a_head(0)
    for bi in range(n_batches):
        next_head = phase_a_head(bi + 1) if bi + 1 < n_batches else None
        phase_a_tail(bi, head)
        if bi > 0:
            phase_bc(bi - 1)
        head = next_head
    phase_bc(n_batches - 1)

    if carry:
        @pl.when(t_idx == pl.num_programs(2) - 1)
        def _():
            sfin_ref[0, 0] = st_ref[...].T


def _hgrn(proj3, fpre3, lb_logits, gain, s0, layer, q_col, v_col, og_col):
    bsz, t, _ = proj3.shape
    hw = fpre3.shape[-1]
    d = HEAD_DIM_HGRN
    nh = hw // d
    c = CHUNK if t % CHUNK == 0 else t
    assert c & (c - 1) == 0 and c >= 2 * SUBLANES and t % c == 0
    carry = t > c
    if not carry:
        out, s_fin = _hgrn_call(proj3.reshape(1, bsz * t, -1), fpre3.reshape(1, bsz * t, hw), lb_logits,
                                gain, s0, layer, q_col, v_col, og_col, c=c, carry=False)
        return out.reshape(bsz, t, hw), s_fin
    return _hgrn_call(proj3, fpre3, lb_logits, gain, s0, layer, q_col, v_col, og_col, c=c, carry=True)


def _hgrn_call(proj3, fpre3, lb_logits, gain, s0, layer, q_col, v_col, og_col, *, c, carry):
    bsz, t, _ = proj3.shape
    hw = fpre3.shape[-1]
    d = HEAD_DIM_HGRN
    nh = hw // d
    tt = _tile(t, ROWS_HGRN_BLOCK, c)
    n_chunks = tt // c
    rows = _tile(tt, 2 * LANES, c)
    pb = min(rows, LANES)
    tri, masks = _hgrn_tables(c, rows, pb)
    qb, vb, ob = q_col // d, v_col // d, og_col // d
    kern = functools.partial(_hgrn_kernel, c=c, n_chunks=n_chunks, rows=rows, pb=pb, layer=layer,
                             carry=carry)
    nl = lb_logits.shape[0]
    if carry:
        state_spec = pl.BlockSpec((1, 1, d, d), lambda b, h, i: (b, h, 0, 0))
    else:
        state_spec = pl.BlockSpec((n_chunks, 1, d, d), lambda b, h, i: (i, h, 0, 0))
    return pl.pallas_call(
        kern,
        grid=(bsz, nh, t // tt),
        in_specs=[pl.BlockSpec((rows, rows), lambda b, h, i: (0, 0)),
                  pl.BlockSpec(masks.shape, lambda b, h, i: (0, 0, 0)),
                  pl.BlockSpec((nl, d), lambda b, h, i: (0, h)),
                  pl.BlockSpec((1, tt, d), lambda b, h, i: (b, i, qb + h)),
                  pl.BlockSpec((1, tt, d), lambda b, h, i: (b, i, h)),
                  pl.BlockSpec((1, tt, d), lambda b, h, i: (b, i, vb + h)),
                  pl.BlockSpec((1, tt, d), lambda b, h, i: (b, i, ob + h)),
                  pl.BlockSpec((1, d), lambda b, h, i: (0, h)),
                  state_spec],
        out_specs=[pl.BlockSpec((1, tt, d), lambda b, h, i: (b, i, h)),
                   state_spec],
        out_shape=[jax.ShapeDtypeStruct((bsz, t, hw), BF16),
                   jax.ShapeDtypeStruct(s0.shape, F32)],
        scratch_shapes=[pltpu.VMEM((d, d), F32),
                        pltpu.VMEM((tt, d), F32),
                        pltpu.VMEM((tt, d), BF16),
                        pltpu.VMEM((n_chunks, d, d), F32),
                        pltpu.VMEM((n_chunks * SUBLANES, d), F32)],
        compiler_params=_params("parallel", "parallel", "arbitrary"),
        name="hgrn2",
    )(jnp.asarray(tri, BF16), jnp.asarray(masks, F32), lb_logits.astype(F32),
      proj3, fpre3, proj3, proj3, gain.reshape(1, hw).astype(F32), s0.astype(F32))


CONV_PAD = 8
CONV_BLOCK_ELEMS = 2 * 1024 * 1024


def _conv_kernel(cc_ref, ch_ref, cb_ref, w_ref, buf_ref, o_ref, nbuf_ref, u_ref, *, t, k, rb):
    u_ref[CONV_PAD - (k - 1):CONV_PAD, :] = buf_ref[0].astype(F32)
    for r in range(0, t, rb):
        u_ref[CONV_PAD + r:CONV_PAD + r + rb, :] = (
            cc_ref[0, r:r + rb, :].astype(F32) * ch_ref[0, r:r + rb, :].astype(F32))
    w = w_ref[...].astype(F32)
    for r in range(0, t, rb):
        z = w[0:1, :] * u_ref[CONV_PAD - (k - 1) + r:CONV_PAD - (k - 1) + r + rb, :]
        for j in range(1, k):
            s = CONV_PAD - (k - 1) + j + r
            z = z + w[j:j + 1, :] * u_ref[s:s + rb, :]
        o_ref[0, r:r + rb, :] = (cb_ref[0, r:r + rb, :].astype(F32) * z).astype(o_ref.dtype)
    nbuf_ref[0] = u_ref[CONV_PAD + t - (k - 1):CONV_PAD + t, :]


def _conv(proj3, conv_w, buf, ch_col, cb_col, cc_col):
    bsz, t, _ = proj3.shape
    k, cw = conv_w.shape
    assert k - 1 <= CONV_PAD
    d = _tile(cw, max(LANES, CONV_BLOCK_ELEMS // t), LANES)
    rb = _tile(t, max(SUBLANES, ROWS_SUB * LANES // d), SUBLANES)
    kern = functools.partial(_conv_kernel, t=t, k=k, rb=rb)
    return pl.pallas_call(
        kern,
        grid=(bsz, cw // d),
        in_specs=[pl.BlockSpec((1, t, d), lambda b, j: (b, 0, cc_col // d + j)),
                  pl.BlockSpec((1, t, d), lambda b, j: (b, 0, ch_col // d + j)),
                  pl.BlockSpec((1, t, d), lambda b, j: (b, 0, cb_col // d + j)),
                  pl.BlockSpec((k, d), lambda b, j: (0, j)),
                  pl.BlockSpec((1, k - 1, d), lambda b, j: (b, 0, j))],
        out_specs=[pl.BlockSpec((1, t, d), lambda b, j: (b, 0, j)),
                   pl.BlockSpec((1, k - 1, d), lambda b, j: (b, 0, j))],
        out_shape=[jax.ShapeDtypeStruct((bsz, t, cw), BF16),
                   jax.ShapeDtypeStruct((bsz, k - 1, cw), F32)],
        scratch_shapes=[pltpu.VMEM((CONV_PAD + t, d), F32)],
        compiler_params=_params("parallel", "parallel"),
        name="short_conv",
    )(proj3, proj3, proj3, conv_w.astype(F32), buf.astype(F32))


def _attn_kernel(q_ref, k_ref, v_ref, o_ref, kb_ref, vb_ref, *, scale, sub_rows):
    @pl.when(pl.program_id(2) == 0)
    def _():
        kb_ref[...] = k_ref[0].astype(BF16)
        vb_ref[...] = v_ref[0].astype(BF16)

    _attn_tile(q_ref, kb_ref, vb_ref, o_ref, scale, sub_rows)


def _attn_head_split_kernel(q_ref, k_hbm, v_hbm, o_ref, kf_ref, vf_ref, kb_ref, vb_ref, sem,
                            *, scale, layer, sub_rows, n_heads, n_pairs):
    pair = pl.program_id(0) * n_heads + pl.program_id(1)

    def slab_copies(n):
        slot = n % 2
        return [pltpu.make_async_copy(src.at[layer, n // n_heads, :, n % n_heads, :], dst.at[slot],
                                      sem.at[slot, i])
                for i, (src, dst) in enumerate(((k_hbm, kf_ref), (v_hbm, vf_ref)))]

    @pl.when(pl.program_id(2) == 0)
    def _():
        @pl.when(pair == 0)
        def _():
            for cp in slab_copies(pair):
                cp.start()

        for cp in slab_copies(pair):
            cp.wait()

        @pl.when(pair + 1 < n_pairs)
        def _():
            for cp in slab_copies(pair + 1):
                cp.start()

        kb_ref[...] = kf_ref[pair % 2].astype(BF16)
        vb_ref[...] = vf_ref[pair % 2].astype(BF16)

    _attn_tile(q_ref, kb_ref, vb_ref, o_ref, scale, sub_rows)


def _attn_tile(q_ref, kb_ref, vb_ref, o_ref, scale, sub_rows):
    nt = (((1,), (1,)), ((), ()))
    subs = [pl.ds(r, sub_rows) for r in range(0, q_ref.shape[1], sub_rows)]
    scores = [lax.dot_general(q_ref[0, r, :], kb_ref[...], nt, preferred_element_type=F32) * scale
              for r in subs]
    probs = []
    for s in scores:
        p = jnp.exp(s - jnp.max(s, axis=-1, keepdims=True))
        probs.append((p / jnp.sum(p, axis=-1, keepdims=True)).astype(BF16))
    for r, p in zip(subs, probs):
        o_ref[0, r, :] = jnp.dot(p, vb_ref[...], preferred_element_type=F32).astype(o_ref.dtype)


def _attention(q3, mk4, mv4, mem_layer, n_heads):
    bsz, t, dm = q3.shape
    n_mem = mk4.shape[2]
    hd = dm // n_heads
    if mk4.ndim == 5:
        bt = _tile(t, ROWS_ATTN, SUBLANES)
        kern = functools.partial(_attn_head_split_kernel, scale=float(hd) ** -0.5, layer=mem_layer,
                                 sub_rows=_tile(bt, ROWS_SUB, SUBLANES), n_heads=n_heads,
                                 n_pairs=bsz * n_heads)
        return pl.pallas_call(
            kern,
            grid=(bsz, n_heads, t // bt),
            in_specs=[pl.BlockSpec((1, bt, hd), lambda b, h, i: (b, i, h)),
                      pl.BlockSpec(memory_space=pl.ANY), pl.BlockSpec(memory_space=pl.ANY)],
            out_specs=pl.BlockSpec((1, bt, hd), lambda b, h, i: (b, i, h)),
            out_shape=jax.ShapeDtypeStruct((bsz, t, dm), BF16),
            scratch_shapes=[pltpu.VMEM((2, n_mem, hd), F32), pltpu.VMEM((2, n_mem, hd), F32),
                            pltpu.VMEM((n_mem, hd), BF16), pltpu.VMEM((n_mem, hd), BF16),
                            pltpu.SemaphoreType.DMA((2, 2))],
            compiler_params=_params("arbitrary", "arbitrary", "arbitrary"),
            name="cross_attention_head_split",
        )(q3, mk4, mv4)
    mem_spec = pl.BlockSpec((None, 1, n_mem, hd), lambda b, h, i: (mem_layer, b, 0, h))
    bt = _tile(t, ROWS_ATTN, SUBLANES)
    kern = functools.partial(_attn_kernel, scale=float(hd) ** -0.5, sub_rows=_tile(bt, ROWS_SUB, SUBLANES))
    return pl.pallas_call(
        kern,
        grid=(bsz, n_heads, t // bt),
        in_specs=[pl.BlockSpec((1, bt, hd), lambda b, h, i: (b, i, h)), mem_spec, mem_spec],
        out_specs=pl.BlockSpec((1, bt, hd), lambda b, h, i: (b, i, h)),
        out_shape=jax.ShapeDtypeStruct((bsz, t, dm), BF16),
        scratch_shapes=[pltpu.VMEM((n_mem, hd), BF16), pltpu.VMEM((n_mem, hd), BF16)],
        compiler_params=_params("parallel", "parallel", "arbitrary"),
        name="cross_attention",
    )(q3, mk4, mv4)


def _project(x, wts):
    hw = wts["hg_norm"].shape[0]
    h = _rmsnorm(x, wts["norm_mix"], BF16)
    if "w_in" in wts:
        pw = wts["w_in"].shape[1]
        proj, w_rest = _matmul(h, wts["w_in"], BF16, bn_pref=TILE_MATMUL_WIDE_OUT,
                               cols=[(0, hw), (2 * hw, pw - 2 * hw)], emit_w=True)
        fpre, w_f = _matmul(h, wts["w_in"], F32, bn_pref=TILE_MATMUL_WIDE_OUT, cols=[(hw, hw)], emit_w=True)
        wts = {k: v for k, v in wts.items() if k != "w_in"} | {"w_in_rest": w_rest, "w_in_f": w_f}
    elif "f32" in wts:
        names = list(wts["f32"])
        proj, cast = _matmul(h, wts["w_in_rest"], BF16, side=[wts["f32"][n] for n in names])
        fpre = _matmul(h, wts["w_in_f"], F32)
        wts = {k: v for k, v in wts.items() if k != "f32"} | dict(zip(names, cast))
    else:
        proj = _matmul(h, wts["w_in_rest"], BF16)
        fpre = _matmul(h, wts["w_in_f"], F32)
    return proj, fpre, wts


def _layer(x, proj, fpre, bsz, t, wts, lb_logits, layer, s0, buf, mem, n_xa_heads):
    m, dm = x.shape
    hw = wts["hg_norm"].shape[0]
    cw = wts["conv_w"].shape[1]
    q_col, v_col, og_col = 0, hw, 2 * hw
    ch_col, cb_col, cc_col = 3 * hw, 3 * hw + cw, 3 * hw + 2 * cw
    ga_col, gb_col = 3 * hw + 3 * cw, 3 * hw + 3 * cw + dm
    proj3 = proj.reshape(bsz, t, proj.shape[1])
    a_in, s_fin = _hgrn(proj3, fpre.reshape(bsz, t, hw), lb_logits, wts["hg_norm"], s0, layer,
                        q_col, v_col, og_col)
    bz, new_buf = _conv(proj3, wts["conv_w"], buf, ch_col, cb_col, cc_col)
    merged = _matmul_gated(a_in.reshape(m, hw), bz.reshape(m, cw), wts["w_a"], wts["w_b"],
                           proj, ga_col, gb_col)
    x, xg, ss = _matmul(merged, wts["w_o"], F32, bn_pref=TILE_MATMUL_WIDE_OUT, res=x,
                        norm_gain=wts["norm_xattn"])
    qx = _matmul(xg, wts["w_xq"], BF16, row_ss=ss)
    att = _attention(qx.reshape(bsz, t, dm), *mem, n_xa_heads)
    x, xg, ss = _matmul(att.reshape(m, dm), wts["w_xo"], F32, bn_pref=TILE_MATMUL_WIDE_OUT, res=x,
                        norm_gain=wts["norm_ffn"])
    act = _matmul_swiglu(xg, ss, wts["w_gate"], wts["w_up"])
    x = _matmul(act, wts["w_down"], F32, bm_pref=TILE_DOWN[0], bn_pref=TILE_DOWN[1], res=x, cols_outer=True)
    return x, s_fin, new_buf


def kernel(x_prompt, x_sample, cache_mem_k, cache_mem_v, state_hgrn, state_conv, mem_prompt, norm_mix, w_in, lb_logits, hg_norm, conv_w, w_a, w_b, w_o, norm_xattn, norm_mem, w_xq, w_xk, w_xv, w_xo, norm_ffn, w_gate, w_up, w_down, norm_final):
    depth = norm_mix.shape[0]
    bp, tp, dm = x_prompt.shape
    bs, ts, _ = x_sample.shape
    hw = hg_norm.shape[1]
    n_mem = mem_prompt.shape[1]
    n_xa_heads = cache_mem_k.shape[3]
    nh, dk, dv = state_hgrn.shape[2:]
    assert dk == HEAD_DIM_HGRN and dv == HEAD_DIM_HGRN and nh * dk == hw

    xp = x_prompt.reshape(bp * tp, dm)
    xs = x_sample.reshape(bs * ts, dm)
    outs = {k: [] for k in ("mk", "mv", "sp", "cp", "ss", "cs")}
    for l in range(depth):
        w_in_l = w_in[l]
        wts = {
            "norm_mix": norm_mix[l], "hg_norm": hg_norm[l], "conv_w": conv_w[l],
            "norm_xattn": norm_xattn[l], "norm_ffn": norm_ffn[l],
            "w_in": w_in_l,
            "f32": {"w_a": w_a[l], "w_b": w_b[l], "w_o": w_o[l], "w_xq": w_xq[l], "w_xo": w_xo[l],
                    "w_gate": w_gate[l], "w_up": w_up[l], "w_down": w_down[l]},
        }
        mem_n = _rmsnorm(mem_prompt.reshape(bp * n_mem, dm), norm_mem[l], BF16)
        mk_p = _matmul(mem_n, w_xk[l], F32)
        mv_p = _matmul(mem_n, w_xv[l], F32)
        s0 = jnp.zeros((bp, nh, dk, dv), F32)
        buf0 = jnp.zeros((bp, conv_w.shape[1] - 1, conv_w.shape[2]), F32)
        mem_p = (mk_p.reshape(1, bp, n_mem, dm), mv_p.reshape(1, bp, n_mem, dm), 0)
        mem_s = (cache_mem_k, cache_mem_v, l)
        proj_s, fpre_s, wts = _project(xs, wts)
        proj_p, fpre_p, wts = _project(xp, wts)
        xp, s_p, buf_p = _layer(xp, proj_p, fpre_p, bp, tp, wts, lb_logits, l, s0, buf0, mem_p, n_xa_heads)
        xs, s_s, buf_s = _layer(xs, proj_s, fpre_s, bs, ts, wts, lb_logits, l, state_hgrn[l],
                                state_conv[l], mem_s, n_xa_heads)
        outs["mk"].append(mk_p.reshape(bp, n_mem, n_xa_heads, dm // n_xa_heads))
        outs["mv"].append(mv_p.reshape(bp, n_mem, n_xa_heads, dm // n_xa_heads))
        outs["sp"].append(s_p)
        outs["cp"].append(buf_p)
        outs["ss"].append(s_s)
        outs["cs"].append(buf_s)
    y_prompt = _rmsnorm(xp, norm_final, F32).reshape(bp, tp, dm)
    y_sample = _rmsnorm(xs, norm_final, F32).reshape(bs, ts, dm)
    return (y_prompt, y_sample, jnp.stack(outs["mk"]), jnp.stack(outs["mv"]), jnp.stack(outs["sp"]),
            jnp.stack(outs["cp"]), jnp.stack(outs["ss"]), jnp.stack(outs["cs"]))
```
